```python
import jax, jax.numpy as jnp
from jax import lax
import numpy as np

D_MODEL = 1024
BATCH = 2
SEQ = 8192
DEPTH = 2
DEC_BATCH = 128
DEC_SEQ = 4
PAST_LEN = 16384
PAGE_SIZE = 128

N_META = 16
D_CONV = D_MODEL // 2
CONV_W = 3
HEAD_DIM = 64
N_HEADS = (D_MODEL // 2) // HEAD_DIM
N_KV = 2
GROUP = N_HEADS // N_KV
WINDOW = 128
BLOCK = 128
Q_DIM = N_HEADS * HEAD_DIM
KV_DIM = N_KV * HEAD_DIM
POOL_WINDOWS = (2, 4, 8, 16)
N_POOL_GROUPS = len(POOL_WINDOWS)
POOL_GROUP_DIM = D_MODEL // N_POOL_GROUPS
POOL_MAX = 16
D_FF = 4 * D_MODEL
EPS = 1e-6
NEG = -1e30
N_EVEN = (DEPTH + 1) // 2
N_ODD = DEPTH // 2
D_IN_EVEN = 3 * D_CONV + Q_DIM + 2 * KV_DIM
D_MIX_EVEN = D_CONV + Q_DIM
SPLITS = (D_CONV, 2 * D_CONV, 3 * D_CONV, 3 * D_CONV + Q_DIM, 3 * D_CONV + Q_DIM + KV_DIM)

kernel_name = 'hybrid_conv_swa_pool_decoder'


def rmsnorm(x, g):
    xf = x.astype(jnp.float32)
    y = xf * lax.rsqrt(jnp.mean(xf * xf, axis=-1, keepdims=True) + EPS)
    return (y * g.astype(jnp.float32)).astype(x.dtype)


def short_conv(xa, gc, gb, buf, conv_w):
    u = gc * xa
    ue = jnp.concatenate([buf.astype(u.dtype), u], axis=1)
    T = u.shape[1]
    y = ue[:, 0:T] * conv_w[0]
    for j in range(1, CONV_W):
        y = y + ue[:, j:j + T] * conv_w[j]
    return gb * y, ue[:, -(CONV_W - 1):]


def sink_softmax_weights(s, sk, dtype):
    m = jnp.maximum(jnp.max(s, axis=-1, keepdims=True), sk)
    p = jnp.exp(s - m)
    denom = jnp.sum(p, axis=-1, keepdims=True) + jnp.exp(sk - m)
    return (p / denom).astype(dtype)


def swa_prompt(q, k, v, sinks):
    B, L = q.shape[:2]
    pad_front = (-N_META) % BLOCK
    lp0 = pad_front + L
    pad_back = (-lp0) % BLOCK
    lp = lp0 + pad_back
    nb = lp // BLOCK
    qp = jnp.pad(q, ((0, 0), (pad_front, pad_back), (0, 0), (0, 0)))
    qp = qp.reshape(B, nb, BLOCK, N_KV, GROUP, HEAD_DIM)
    kvpad = ((0, 0), (pad_front + BLOCK, pad_back), (0, 0), (0, 0))
    kp = jnp.pad(k, kvpad).reshape(B, nb + 1, BLOCK, N_KV, HEAD_DIM)
    vp = jnp.pad(v, kvpad).reshape(B, nb + 1, BLOCK, N_KV, HEAD_DIM)
    kb = jnp.concatenate([kp[:, :-1], kp[:, 1:]], axis=2)
    vb = jnp.concatenate([vp[:, :-1], vp[:, 1:]], axis=2)
    s = jnp.einsum('bnqkgd,bnskd->bnkgqs', qp, kb, preferred_element_type=jnp.float32) * (HEAD_DIM ** -0.5)
    qi = jnp.arange(nb)[:, None] * BLOCK + jnp.arange(BLOCK)[None, :]
    kj = jnp.arange(nb)[:, None] * BLOCK - BLOCK + jnp.arange(2 * BLOCK)[None, :]
    d = qi[:, :, None] - kj[:, None, :]
    valid = (d >= 0) & (d <= WINDOW) & (kj[:, None, :] >= pad_front)
    s = jnp.where(valid[None, :, None, None], s, NEG)
    sk = sinks.astype(jnp.float32).reshape(N_KV, GROUP)[None, None, :, :, None, None]
    p = sink_softmax_weights(s, sk, v.dtype)
    o = jnp.einsum('bnkgqs,bnskd->bnqkgd', p, vb)
    return o.reshape(B, lp, Q_DIM)[:, pad_front:pad_front + L]


def swa_sample(q, k, v, ck, cv, sinks):
    Bd, T = q.shape[:2]
    wb = ck.shape[1]
    ke = jnp.concatenate([ck.astype(k.dtype), k], axis=1)
    ve = jnp.concatenate([cv.astype(v.dtype), v], axis=1)
    qg = q.reshape(Bd, T, N_KV, GROUP, HEAD_DIM)
    s = jnp.einsum('btkgd,bskd->bkgts', qg, ke, preferred_element_type=jnp.float32) * (HEAD_DIM ** -0.5)
    d = (wb + jnp.arange(T))[:, None] - jnp.arange(wb + T)[None, :]
    valid = (d >= 0) & (d <= WINDOW)
    s = jnp.where(valid[None, None, None], s, NEG)
    sk = sinks.astype(jnp.float32).reshape(N_KV, GROUP)[None, :, :, None, None]
    p = sink_softmax_weights(s, sk, v.dtype)
    o = jnp.einsum('bkgts,bskd->btkgd', p, ve).reshape(Bd, T, Q_DIM)
    return o, ke[:, -wb:], ve[:, -wb:]


def pool_mix(h, buf, pos0, w_pool, scale):
    B, T, D = h.shape
    P = POOL_MAX - 1
    he = jnp.concatenate([buf.astype(h.dtype), h], axis=1)
    cs = jnp.cumsum(he.astype(jnp.float32), axis=1)
    cs = jnp.pad(cs, ((0, 0), (1, 0), (0, 0)))
    end = cs[:, P + 1:]
    pos = pos0 + jnp.arange(T)
    groups = []
    for gi, w in enumerate(POOL_WINDOWS):
        sl = slice(gi * POOL_GROUP_DIM, (gi + 1) * POOL_GROUP_DIM)
        start = cs[:, P + 1 - w:P + 1 - w + T, sl]
        cnt = jnp.minimum(w, pos + 1).astype(jnp.float32)[None, :, None]
        groups.append((end[..., sl] - start) / cnt)
    pooled = jnp.concatenate(groups, axis=-1)
    dlt = (pooled - h.astype(jnp.float32)).astype(h.dtype).reshape(B, T, N_POOL_GROUPS, POOL_GROUP_DIM)
    y = jnp.einsum('btgc,gce->btge', dlt, w_pool).reshape(B, T, D)
    return y * scale, he[:, -P:]


def sqrelu_mlp(x, w_up, w_down):
    return jnp.square(jax.nn.relu(x @ w_up)) @ w_down


def trunk(x, pos0, conv_st, k_st, v_st, pool_st, is_prompt, norm_mix, norm_mlp, w_in_even, conv_w,
          q_norm, k_norm, attn_sinks, w_out_even, w_pool, pool_scale, w_up, w_down):
    B, T, _ = x.shape
    conv_new, k_new, v_new, pool_new = [], [], [], []
    for layer in range(DEPTH):
        i = layer // 2
        h = rmsnorm(x, norm_mix[layer])
        if layer % 2 == 0:
            z = h @ w_in_even[i]
            xa, gc, gb, q, k, v = jnp.split(z, SPLITS, axis=-1)
            ya, cbuf = short_conv(xa, gc, gb, conv_st[i], conv_w[i])
            q = rmsnorm(q.reshape(B, T, N_HEADS, HEAD_DIM), q_norm[i])
            k = rmsnorm(k.reshape(B, T, N_KV, HEAD_DIM), k_norm[i])
            v = v.reshape(B, T, N_KV, HEAD_DIM)
            if is_prompt:
                yb = swa_prompt(q, k, v, attn_sinks[i])
                kbuf, vbuf = k[:, -WINDOW:], v[:, -WINDOW:]
            else:
                yb, kbuf, vbuf = swa_sample(q, k, v, k_st[i], v_st[i], attn_sinks[i])
            x = x + jnp.concatenate([ya, yb], axis=-1) @ w_out_even[i]
            conv_new.append(cbuf)
            k_new.append(kbuf)
            v_new.append(vbuf)
        else:
            yc, pbuf = pool_mix(h, pool_st[i], pos0, w_pool[i], pool_scale[i])
            x = x + yc
            pool_new.append(pbuf)
        x = x + sqrelu_mlp(rmsnorm(x, norm_mlp[layer]), w_up[layer], w_down[layer])
    return x, jnp.stack(conv_new), jnp.stack(k_new), jnp.stack(v_new), jnp.stack(pool_new)


def setup_inputs(seed: int = 0) -> dict:
    key = jax.random.key(seed)
    ks = jax.random.split(key, 20)
    f32 = jnp.float32

    def nrm(k, shape, scale):
        return jax.random.normal(k, shape, f32) * scale

    win_rows = min(WINDOW, PAST_LEN)
    return {
        'x_prompt': nrm(ks[0], (BATCH, SEQ, D_MODEL), 1.0),
        'x_sample': nrm(ks[1], (DEC_BATCH, DEC_SEQ, D_MODEL), 1.0),
        'state_conv': nrm(ks[2], (N_EVEN, DEC_BATCH, CONV_W - 1, D_CONV), 1.0),
        'cache_k_win': nrm(ks[3], (N_EVEN, DEC_BATCH, win_rows, N_KV, HEAD_DIM), 1.0),
        'cache_v_win': nrm(ks[4], (N_EVEN, DEC_BATCH, win_rows, N_KV, HEAD_DIM), 1.0),
        'state_pool': nrm(ks[5], (N_ODD, DEC_BATCH, POOL_MAX - 1, D_MODEL), 1.0),
        'meta_tokens': nrm(ks[6], (N_META, D_MODEL), 1.0),
        'norm_mix': 1.0 + nrm(ks[7], (DEPTH, D_MODEL), 0.02),
        'norm_mlp': 1.0 + nrm(ks[8], (DEPTH, D_MODEL), 0.02),
        'w_in_even': nrm(ks[9], (N_EVEN, D_MODEL, D_IN_EVEN), D_MODEL ** -0.5),
        'conv_w': nrm(ks[10], (N_EVEN, CONV_W, D_CONV), CONV_W ** -0.5),
        'q_norm': 1.0 + nrm(ks[11], (N_EVEN, HEAD_DIM), 0.02),
        'k_norm': 1.0 + nrm(ks[12], (N_EVEN, HEAD_DIM), 0.02),
        'attn_sinks': nrm(ks[13], (N_EVEN, N_HEADS), 0.5),
        'w_out_even': nrm(ks[14], (N_EVEN, D_MIX_EVEN, D_MODEL), D_MIX_EVEN ** -0.5),
        'w_pool': nrm(ks[15], (N_ODD, N_POOL_GROUPS, POOL_GROUP_DIM, POOL_GROUP_DIM), POOL_GROUP_DIM ** -0.5),
        'pool_scale': 1.0 + nrm(ks[16], (N_ODD, D_MODEL), 0.1),
        'w_up': nrm(ks[17], (DEPTH, D_MODEL, D_FF), D_MODEL ** -0.5),
        'w_down': nrm(ks[18], (DEPTH, D_FF, D_MODEL), D_FF ** -0.5),
    }


def reference(x_prompt, x_sample, state_conv, cache_k_win, cache_v_win, state_pool, meta_tokens,
              norm_mix, norm_mlp, w_in_even, conv_w, q_norm, k_norm, attn_sinks, w_out_even,
              w_pool, pool_scale, w_up, w_down):
    weights = (norm_mix, norm_mlp, w_in_even, conv_w, q_norm, k_norm, attn_sinks, w_out_even,
               w_pool, pool_scale, w_up, w_down)
    B = x_prompt.shape[0]
    dt = x_prompt.dtype
    meta = jnp.broadcast_to(meta_tokens.astype(dt)[None], (B, N_META, D_MODEL))
    xp = jnp.concatenate([meta, x_prompt], axis=1)
    zero_conv = jnp.zeros((N_EVEN, B, CONV_W - 1, D_CONV), dt)
    zero_pool = jnp.zeros((N_ODD, B, POOL_MAX - 1, D_MODEL), dt)
    hp, conv_p, k_p, v_p, pool_p = trunk(xp, 0, zero_conv, None, None, zero_pool, True, *weights)
    y_prompt = hp[:, N_META:]
    y_sample, conv_s, k_s, v_s, pool_s = trunk(x_sample, PAST_LEN, state_conv, cache_k_win, cache_v_win,
                                               state_pool, False, *weights)
    return (y_prompt, y_sample, conv_p, conv_s, k_p, k_s, v_p, v_s, pool_p, pool_s)
```

```python
import functools

import jax
import jax.numpy as jnp
from jax import lax
from jax.experimental import pallas as pl
from jax.experimental.pallas import tpu as pltpu

F32 = jnp.float32
BF16 = jnp.bfloat16

D_MODEL = 1024
D_CONV = 512
CONV_W = 3
HEAD_DIM = 64
N_HEADS = 8
N_KV = 2
GROUP = N_HEADS // N_KV
WINDOW = 128
BLOCK = 128
Q_DIM = N_HEADS * HEAD_DIM
KV_DIM = N_KV * HEAD_DIM
POOL_WINDOWS = (2, 4, 8, 16)
POOL_GROUP_DIM = D_MODEL // len(POOL_WINDOWS)
POOL_MAX = 16
D_FF = 4 * D_MODEL
D_IN_EVEN = 3 * D_CONV + Q_DIM + 2 * KV_DIM
N_META = 16
EPS = 1e-6
NEG = -1e30

META_PAD = BLOCK - N_META
SEQ_TILE = 512
FF_CHUNK = 1024
CONV_HDR = 8
SAMPLE_BB = 16
SAMPLE_KEYS = 144
VMEM_LIMIT = 56 * 1024 * 1024


def _dot(a, b):
    return jnp.dot(a, b, preferred_element_type=F32)


def _dot_nt(a, b):
    return lax.dot_general(a, b, (((1,), (1,)), ((), ())), preferred_element_type=F32)


def _rms(x, g):
    ms = jnp.mean(x * x, axis=-1, keepdims=True)
    return x * lax.rsqrt(ms + EPS) * g


def _head_rms(x, seg, g):
    x2 = x * x
    hi = x2.astype(BF16)
    lo = (x2 - hi.astype(F32)).astype(BF16)
    ms = _dot(hi, seg) + _dot(lo, seg)
    return x * lax.rsqrt(ms + EPS) * g


def _mlp(x, g, wup_ref, wdn_ref):
    xn = _rms(x, g).astype(BF16)
    acc = x
    for c in range(D_FF // FF_CHUNK):
        cols = slice(c * FF_CHUNK, (c + 1) * FF_CHUNK)
        h = _dot(xn, wup_ref[:, cols])
        a = jnp.square(jnp.maximum(h, 0.0)).astype(BF16)
        acc = acc + _dot(a, wdn_ref[cols, :])
    return acc


def _in_proj(x, g, win_ref, segq_ref, segk_ref, qg, kg):
    hn = _rms(x, g).astype(BF16)
    z = _dot(hn, win_ref[...])
    xa = z[:, 0:D_CONV]
    gc = z[:, D_CONV:2 * D_CONV]
    gb = z[:, 2 * D_CONV:3 * D_CONV]
    q = z[:, 3 * D_CONV:3 * D_CONV + Q_DIM]
    k = z[:, 3 * D_CONV + Q_DIM:3 * D_CONV + Q_DIM + KV_DIM]
    v = z[:, 3 * D_CONV + Q_DIM + KV_DIM:]
    qn = _head_rms(q, segq_ref[...], qg) * (HEAD_DIM ** -0.5)
    kn = _head_rms(k, segk_ref[...], kg)
    return gc * xa, gb, qn, kn, v


def _l0_tile(x, rows, sinks_ref, gmix, win_ref, segq_ref, segk_ref, qg, kg, cw_ref, wout_ref, gmlp,
             wup_ref, wdn_ref, u_scr, k_scr, v_scr, kb_scr, band_scr):
    u, gb, qn, kn, v = _in_proj(x, gmix, win_ref, segq_ref, segk_ref, qg, kg)

    u_scr[CONV_HDR:CONV_HDR + rows, :] = u
    u1 = u_scr[CONV_HDR - 1:CONV_HDR - 1 + rows, :]
    u2 = u_scr[CONV_HDR - 2:CONV_HDR - 2 + rows, :]
    ya = gb * (u2 * cw_ref[0:1, :] + u1 * cw_ref[1:2, :] + u * cw_ref[2:3, :])
    u_scr[CONV_HDR - 2:CONV_HDR, :] = u_scr[CONV_HDR - 2 + rows:CONV_HDR + rows, :]

    k_scr[BLOCK:BLOCK + rows, :] = kn
    v_scr[BLOCK:BLOCK + rows, :] = v
    yb_blocks = []
    for i in range(rows // BLOCK):
        qb = qn[i * BLOCK:(i + 1) * BLOCK, :]
        bias = band_scr[...] + kb_scr[:, i * BLOCK:(i + 2) * BLOCK]
        heads = []
        for j in range(N_KV):
            lanes = slice(j * HEAD_DIM, (j + 1) * HEAD_DIM)
            k2 = k_scr[i * BLOCK:(i + 2) * BLOCK, lanes].astype(BF16)
            v2 = v_scr[i * BLOCK:(i + 2) * BLOCK, lanes].astype(BF16)
            qs = jnp.concatenate(
                [qb[:, (j * GROUP + g) * HEAD_DIM:(j * GROUP + g + 1) * HEAD_DIM] for g in range(GROUP)],
                axis=0).astype(BF16)
            sk = jnp.concatenate(
                [jnp.full((BLOCK, 1), sinks_ref[j * GROUP + g], F32) for g in range(GROUP)], axis=0)
            s = _dot_nt(qs, k2) + bias
            m = jnp.maximum(jnp.max(s, axis=-1, keepdims=True), sk)
            p = jnp.exp(s - m)
            den = jnp.sum(p, axis=-1, keepdims=True) + jnp.exp(sk - m)
            o = _dot(p.astype(BF16), v2) / den
            heads += [o[g * BLOCK:(g + 1) * BLOCK, :] for g in range(GROUP)]
        yb_blocks.append(jnp.concatenate(heads, axis=-1))
    yb = yb_blocks[0] if len(yb_blocks) == 1 else jnp.concatenate(yb_blocks, axis=0)
    k_scr[0:BLOCK, :] = k_scr[rows:rows + BLOCK, :]
    v_scr[0:BLOCK, :] = v_scr[rows:rows + BLOCK, :]

    mix = jnp.concatenate([ya, yb], axis=-1).astype(BF16)
    x1 = x + _dot(mix, wout_ref[...])
    return _mlp(x1, gmlp, wup_ref, wdn_ref)


def _l0_prompt_kernel(sinks_ref, x_ref, meta_ref, gmix_ref, win_ref, segq_ref, segk_ref, qg_ref, kg_ref,
                      cw_ref, wout_ref, gmlp_ref, wup_ref, wdn_ref,
                      y_ref, ymeta_ref, conv_ref, kout_ref, vout_ref,
                      u_scr, k_scr, v_scr, kb_scr, band_scr):
    s = pl.program_id(1)
    tile = functools.partial(
        _l0_tile, sinks_ref=sinks_ref, gmix=gmix_ref[...], win_ref=win_ref, segq_ref=segq_ref,
        segk_ref=segk_ref, qg=qg_ref[...], kg=kg_ref[...], cw_ref=cw_ref, wout_ref=wout_ref,
        gmlp=gmlp_ref[...], wup_ref=wup_ref, wdn_ref=wdn_ref, u_scr=u_scr, k_scr=k_scr, v_scr=v_scr,
        kb_scr=kb_scr, band_scr=band_scr)

    @pl.when(s == 0)
    def _meta():
        r = lax.broadcasted_iota(jnp.int32, (GROUP * BLOCK, 2 * BLOCK), 0) & (BLOCK - 1)
        c = lax.broadcasted_iota(jnp.int32, (GROUP * BLOCK, 2 * BLOCK), 1)
        band_scr[...] = jnp.where((c >= r) & (c <= r + WINDOW), 0.0, NEG)
        u_scr[0:CONV_HDR, :] = jnp.zeros((CONV_HDR, D_CONV), F32)
        k_scr[0:BLOCK, :] = jnp.zeros((BLOCK, KV_DIM), F32)
        v_scr[0:BLOCK, :] = jnp.zeros((BLOCK, KV_DIM), F32)
        lane = lax.broadcasted_iota(jnp.int32, (1, BLOCK), 1)
        meta_bias = jnp.where(lane >= META_PAD, 0.0, NEG)
        kb_scr[:, 0:BLOCK] = jnp.full((1, BLOCK), NEG, F32)
        kb_scr[:, BLOCK:2 * BLOCK] = meta_bias
        ymeta_ref[...] = tile(meta_ref[...], BLOCK)
        kb_scr[:, 0:BLOCK] = meta_bias
        kb_scr[:, BLOCK:] = jnp.zeros((1, SEQ_TILE), F32)

    @pl.when(s > 0)
    def _main():
        y_ref[...] = tile(x_ref[...], SEQ_TILE)
        kb_scr[:, 0:BLOCK] = jnp.zeros((1, BLOCK), F32)

    @pl.when(s == pl.num_programs(1) - 1)
    def _state():
        conv_ref[...] = u_scr[CONV_HDR - 2:CONV_HDR, :]
        kout_ref[...] = k_scr[0:BLOCK, :]
        vout_ref[...] = v_scr[0:BLOCK, :]


def _l1_tile(x, rows, is_meta, gmix, wpool_ref, pscale, gmlp, wup_ref, wdn_ref, h_scr):
    h = _rms(x, gmix)
    h_scr[POOL_MAX:POOL_MAX + rows, :] = h
    ys = []
    for gi, w in enumerate(POOL_WINDOWS):
        cols = slice(gi * POOL_GROUP_DIM, (gi + 1) * POOL_GROUP_DIM)
        hg = h[:, cols]
        acc = hg
        for j in range(1, w):
            acc = acc + h_scr[POOL_MAX - j:POOL_MAX - j + rows, cols]
        if is_meta:
            r = lax.broadcasted_iota(jnp.int32, (rows, 1), 0)
            pooled = acc / jnp.clip(r - (META_PAD - 1), 1, w).astype(F32)
        else:
            pooled = acc * (1.0 / w)
        ys.append(_dot((pooled - hg).astype(BF16), wpool_ref[gi]))
    h_scr[0:POOL_MAX, :] = h_scr[rows:rows + POOL_MAX, :]
    x1 = x + jnp.concatenate(ys, axis=-1) * pscale
    return _mlp(x1, gmlp, wup_ref, wdn_ref)


def _l1_prompt_kernel(x_ref, meta_ref, gmix_ref, wpool_ref, pscale_ref, gmlp_ref, wup_ref, wdn_ref,
                      y_ref, pool_ref, h_scr):
    s = pl.program_id(1)
    tile = functools.partial(_l1_tile, gmix=gmix_ref[...], wpool_ref=wpool_ref, pscale=pscale_ref[...],
                             gmlp=gmlp_ref[...], wup_ref=wup_ref, wdn_ref=wdn_ref, h_scr=h_scr)

    @pl.when(s == 0)
    def _meta():
        h_scr[0:POOL_MAX, :] = jnp.zeros((POOL_MAX, D_MODEL), F32)
        tile(meta_ref[...], BLOCK, True)

    @pl.when(s > 0)
    def _main():
        y_ref[...] = tile(x_ref[...], SEQ_TILE, False)

    @pl.when(s == pl.num_programs(1) - 1)
    def _state():
        pool_ref[...] = h_scr[1:POOL_MAX, :]


def _s0_pre_kernel(x_ref, st_ref, gmix_ref, win_ref, segq_ref, segk_ref, qg_ref, kg_ref, cw_ref,
                   ya_ref, q_ref, k_ref, v_ref, conv_ref):
    n_seq = st_ref.shape[1]
    n_t = x_ref.shape[0] // n_seq
    u, gb, qn, kn, v = _in_proj(x_ref[...], gmix_ref[...], win_ref, segq_ref, segk_ref, qg_ref[...],
                                kg_ref[...])
    ue = [st_ref[0], st_ref[1]] + [u[t * n_seq:(t + 1) * n_seq, :] for t in range(n_t)]
    for t in range(n_t):
        y = ue[t] * cw_ref[0:1, :] + ue[t + 1] * cw_ref[1:2, :] + ue[t + 2] * cw_ref[2:3, :]
        ya_ref[t * n_seq:(t + 1) * n_seq, :] = gb[t * n_seq:(t + 1) * n_seq, :] * y
    q_ref[...] = qn
    k_ref[...] = kn
    v_ref[...] = v
    conv_ref[0] = ue[-2]
    conv_ref[1] = ue[-1]


def _s0_attn_kernel(q_ref, ck_ref, cv_ref, kn_ref, vn_ref, sk_ref, o_ref, kout_ref, vout_ref, ke_scr, ve_scr):
    bb, n_t = kn_ref.shape[0], kn_ref.shape[1]
    wb = ck_ref.shape[1]
    pad = SAMPLE_KEYS - wb - n_t
    for src, new, scr, out in ((ck_ref, kn_ref, ke_scr, kout_ref), (cv_ref, vn_ref, ve_scr, vout_ref)):
        scr[:, 0:wb, :] = src[...]
        scr[:, wb:wb + n_t, :] = new[...]
        scr[:, wb + n_t:, :] = jnp.zeros((bb, pad, KV_DIM), F32)
        out[...] = scr[:, n_t:n_t + wb, :]
    rows = n_t * GROUP
    t = lax.broadcasted_iota(jnp.int32, (rows, SAMPLE_KEYS), 0) // GROUP
    c = lax.broadcasted_iota(jnp.int32, (rows, SAMPLE_KEYS), 1)
    dist = wb + t - c
    bias = jnp.where((dist >= 0) & (dist <= WINDOW), 0.0, NEG)[None]
    for j in range(N_KV):
        lanes = slice(j * HEAD_DIM, (j + 1) * HEAD_DIM)
        q = q_ref[:, j].astype(BF16)
        k = ke_scr[:, :, lanes].astype(BF16)
        v = ve_scr[:, :, lanes].astype(BF16)
        sk = sk_ref[j][None]
        s = jnp.einsum('bqd,bkd->bqk', q, k, preferred_element_type=F32) + bias
        m = jnp.maximum(jnp.max(s, axis=-1, keepdims=True), sk)
        p = jnp.exp(s - m)
        den = jnp.sum(p, axis=-1, keepdims=True) + jnp.exp(sk - m)
        o = jnp.einsum('bqk,bkd->bqd', p.astype(BF16), v, preferred_element_type=F32)
        o_ref[:, j] = o / den


def _s0_post_kernel(x_ref, ya_ref, yb_ref, wout_ref, gmlp_ref, wup_ref, wdn_ref, y_ref):
    mix = jnp.concatenate([ya_ref[...], yb_ref[...]], axis=-1).astype(BF16)
    x1 = x_ref[...] + _dot(mix, wout_ref[...])
    y_ref[...] = _mlp(x1, gmlp_ref[...], wup_ref, wdn_ref)


def _s1_kernel(x_ref, buf_ref, gmix_ref, wpool_ref, pscale_ref, gmlp_ref, wup_ref, wdn_ref, y_ref, pool_ref):
    n_buf, n_seq = buf_ref.shape[0], buf_ref.shape[1]
    n_t = x_ref.shape[0] // n_seq
    x = x_ref[...]
    h = _rms(x, gmix_ref[...])
    he = [buf_ref[i] for i in range(n_buf)] + [h[t * n_seq:(t + 1) * n_seq, :] for t in range(n_t)]
    for i in range(n_buf):
        pool_ref[i] = he[n_t + i]
    ys = []
    for gi, w in enumerate(POOL_WINDOWS):
        cols = slice(gi * POOL_GROUP_DIM, (gi + 1) * POOL_GROUP_DIM)
        dl = []
        for t in range(n_t):
            acc = he[n_buf + t][:, cols]
            for j in range(1, w):
                acc = acc + he[n_buf + t - j][:, cols]
            dl.append(acc * (1.0 / w) - he[n_buf + t][:, cols])
        ys.append(_dot(jnp.concatenate(dl, axis=0).astype(BF16), wpool_ref[gi]))
    x1 = x + jnp.concatenate(ys, axis=-1) * pscale_ref[...]
    y_ref[...] = _mlp(x1, gmlp_ref[...], wup_ref, wdn_ref)


def _resident(shape, n_grid):
    zeros = (0,) * len(shape)
    index_map = {1: lambda i: zeros, 2: lambda i, j: zeros}[n_grid]
    return pl.BlockSpec(shape, index_map, pipeline_mode=pl.Buffered(1))


def _params(n_grid):
    return pltpu.CompilerParams(dimension_semantics=("arbitrary",) * n_grid, vmem_limit_bytes=VMEM_LIMIT)


def _l0_prompt(x, meta_pad, sinks, gmix, win, segq, segk, qg, kg, cw, wout, gmlp, wup, wdn):
    n_b, seq, _ = x.shape
    n_tiles = seq // SEQ_TILE
    res = functools.partial(_resident, n_grid=2)
    tile_spec = pl.BlockSpec((None, SEQ_TILE, D_MODEL), lambda b, s: (b, jnp.maximum(s - 1, 0), 0))
    per_b = lambda r, c: pl.BlockSpec((None, r, c), lambda b, s: (b, 0, 0))
    return pl.pallas_call(
        _l0_prompt_kernel,
        grid=(n_b, n_tiles + 1),
        in_specs=[pl.BlockSpec(memory_space=pltpu.SMEM), tile_spec, res((BLOCK, D_MODEL)), res((1, D_MODEL)),
                  res((D_MODEL, D_IN_EVEN)), res((Q_DIM, Q_DIM)), res((KV_DIM, KV_DIM)), res((1, Q_DIM)),
                  res((1, KV_DIM)), res((CONV_W, D_CONV)), res((D_MODEL, D_MODEL)), res((1, D_MODEL)),
                  res((D_MODEL, D_FF)), res((D_FF, D_MODEL))],
        out_specs=[tile_spec, per_b(BLOCK, D_MODEL), per_b(CONV_W - 1, D_CONV), per_b(WINDOW, KV_DIM),
                   per_b(WINDOW, KV_DIM)],
        out_shape=[jax.ShapeDtypeStruct((n_b, seq, D_MODEL), F32),
                   jax.ShapeDtypeStruct((n_b, BLOCK, D_MODEL), F32),
                   jax.ShapeDtypeStruct((n_b, CONV_W - 1, D_CONV), F32),
                   jax.ShapeDtypeStruct((n_b, WINDOW, KV_DIM), F32),
                   jax.ShapeDtypeStruct((n_b, WINDOW, KV_DIM), F32)],
        scratch_shapes=[pltpu.VMEM((CONV_HDR + SEQ_TILE, D_CONV), F32),
                        pltpu.VMEM((BLOCK + SEQ_TILE, KV_DIM), F32),
                        pltpu.VMEM((BLOCK + SEQ_TILE, KV_DIM), F32),
                        pltpu.VMEM((1, BLOCK + SEQ_TILE), F32),
                        pltpu.VMEM((GROUP * BLOCK, 2 * BLOCK), F32)],
        compiler_params=_params(2),
        name="l0_prompt",
    )(sinks, x, meta_pad, gmix, win, segq, segk, qg, kg, cw, wout, gmlp, wup, wdn)


def _l1_prompt(x, xmeta, gmix, wpool, pscale, gmlp, wup, wdn):
    n_b, seq, _ = x.shape
    n_tiles = seq // SEQ_TILE
    res = functools.partial(_resident, n_grid=2)
    tile_spec = pl.BlockSpec((None, SEQ_TILE, D_MODEL), lambda b, s: (b, jnp.maximum(s - 1, 0), 0))
    per_b = lambda r, c: pl.BlockSpec((None, r, c), lambda b, s: (b, 0, 0))
    return pl.pallas_call(
        _l1_prompt_kernel,
        grid=(n_b, n_tiles + 1),
        in_specs=[tile_spec, per_b(BLOCK, D_MODEL), res((1, D_MODEL)),
                  res((len(POOL_WINDOWS), POOL_GROUP_DIM, POOL_GROUP_DIM)), res((1, D_MODEL)),
                  res((1, D_MODEL)), res((D_MODEL, D_FF)), res((D_FF, D_MODEL))],
        out_specs=[tile_spec, per_b(POOL_MAX - 1, D_MODEL)],
        out_shape=[jax.ShapeDtypeStruct((n_b, seq, D_MODEL), F32),
                   jax.ShapeDtypeStruct((n_b, POOL_MAX - 1, D_MODEL), F32)],
        scratch_shapes=[pltpu.VMEM((POOL_MAX + SEQ_TILE, D_MODEL), F32)],
        compiler_params=_params(2),
        name="l1_prompt",
    )(x, xmeta, gmix, wpool, pscale, gmlp, wup, wdn)


def _single_step(kernel, name, out_shape, *args):
    res = functools.partial(_resident, n_grid=1)
    return pl.pallas_call(
        kernel,
        grid=(1,),
        in_specs=[res(a.shape) for a in args],
        out_specs=[pl.BlockSpec(o.shape, lambda i, n=len(o.shape): (0,) * n) for o in out_shape],
        out_shape=out_shape,
        compiler_params=_params(1),
        name=name,
    )(*args)


def _s0_attn(q4, ck, cv, kn, vn, sk_rows):
    n_seq, wb, _ = ck.shape
    n_t = kn.shape[1]
    rows = n_t * GROUP
    blk = lambda *tail: pl.BlockSpec((SAMPLE_BB,) + tail, lambda i: (i,) + (0,) * len(tail))
    return pl.pallas_call(
        _s0_attn_kernel,
        grid=(n_seq // SAMPLE_BB,),
        in_specs=[blk(N_KV, rows, HEAD_DIM), blk(wb, KV_DIM), blk(wb, KV_DIM), blk(n_t, KV_DIM),
                  blk(n_t, KV_DIM), pl.BlockSpec((N_KV, rows, 1), lambda i: (0, 0, 0))],
        out_specs=[blk(N_KV, rows, HEAD_DIM), blk(wb, KV_DIM), blk(wb, KV_DIM)],
        out_shape=[jax.ShapeDtypeStruct((n_seq, N_KV, rows, HEAD_DIM), F32),
                   jax.ShapeDtypeStruct((n_seq, wb, KV_DIM), F32),
                   jax.ShapeDtypeStruct((n_seq, wb, KV_DIM), F32)],
        scratch_shapes=[pltpu.VMEM((SAMPLE_BB, SAMPLE_KEYS, KV_DIM), F32),
                        pltpu.VMEM((SAMPLE_BB, SAMPLE_KEYS, KV_DIM), F32)],
        compiler_params=_params(1),
        name="s0_attn",
    )(q4, ck, cv, kn, vn, sk_rows)


def kernel(x_prompt, x_sample, state_conv, cache_k_win, cache_v_win, state_pool, meta_tokens, norm_mix, norm_mlp, w_in_even, conv_w, q_norm, k_norm, attn_sinks, w_out_even, w_pool, pool_scale, w_up, w_down):
    n_b = x_prompt.shape[0]
    n_seq, n_t, _ = x_sample.shape
    wb = cache_k_win.shape[2]
    assert x_prompt.shape[1] % SEQ_TILE == 0 and n_seq % SAMPLE_BB == 0
    assert wb == WINDOW and wb + n_t <= SAMPLE_KEYS

    win = w_in_even[0].astype(BF16)
    wout = w_out_even[0].astype(BF16)
    wpool = w_pool[0].astype(BF16)
    wup = [w_up[l].astype(BF16) for l in range(2)]
    wdn = [w_down[l].astype(BF16) for l in range(2)]
    gmix = [norm_mix[l][None, :] for l in range(2)]
    gmlp = [norm_mlp[l][None, :] for l in range(2)]
    qg = jnp.tile(q_norm[0], N_HEADS)[None, :]
    kg = jnp.tile(k_norm[0], N_KV)[None, :]
    segq = jnp.kron(jnp.eye(N_HEADS, dtype=F32), jnp.full((HEAD_DIM, HEAD_DIM), 1.0 / HEAD_DIM, F32)).astype(BF16)
    segk = segq[:KV_DIM, :KV_DIM]
    cw = conv_w[0]
    sinks = attn_sinks[0]
    pscale = pool_scale[0][None, :]
    meta_pad = jnp.pad(meta_tokens, ((META_PAD, 0), (0, 0)))

    x2, x2_meta, conv_p, k_p, v_p = _l0_prompt(x_prompt, meta_pad, sinks, gmix[0], win, segq, segk, qg, kg, cw,
                                               wout, gmlp[0], wup[0], wdn[0])
    y_prompt, pool_p = _l1_prompt(x2, x2_meta, gmix[1], wpool, pscale, gmlp[1], wup[1], wdn[1])

    n_rows = n_t * n_seq
    xs = jnp.transpose(x_sample, (1, 0, 2)).reshape(n_rows, D_MODEL)
    st_conv = jnp.transpose(state_conv[0], (1, 0, 2))
    sds = lambda *shape: jax.ShapeDtypeStruct(shape, F32)
    ya, qn, kn, vn, conv_s = _single_step(
        _s0_pre_kernel, "s0_pre",
        [sds(n_rows, D_CONV), sds(n_rows, Q_DIM), sds(n_rows, KV_DIM), sds(n_rows, KV_DIM),
         sds(CONV_W - 1, n_seq, D_CONV)],
        xs, st_conv, gmix[0], win, segq, segk, qg, kg, cw)
    q4 = qn.reshape(n_t, n_seq, N_KV, GROUP, HEAD_DIM).transpose(1, 2, 0, 3, 4)
    q4 = q4.reshape(n_seq, N_KV, n_t * GROUP, HEAD_DIM)
    kn_b = kn.reshape(n_t, n_seq, KV_DIM).transpose(1, 0, 2)
    vn_b = vn.reshape(n_t, n_seq, KV_DIM).transpose(1, 0, 2)
    sk_rows = jnp.tile(sinks.reshape(N_KV, 1, GROUP), (1, n_t, 1)).reshape(N_KV, n_t * GROUP, 1)
    o4, k_s, v_s = _s0_attn(q4, cache_k_win[0].reshape(n_seq, wb, KV_DIM),
                            cache_v_win[0].reshape(n_seq, wb, KV_DIM), kn_b, vn_b, sk_rows)
    yb = o4.reshape(n_seq, N_KV, n_t, GROUP, HEAD_DIM).transpose(2, 0, 1, 3, 4).reshape(n_rows, Q_DIM)
    xs2, = _single_step(_s0_post_kernel, "s0_post", [sds(n_rows, D_MODEL)],
                        xs, ya, yb, wout, gmlp[0], wup[0], wdn[0])
    buf_t = jnp.transpose(state_pool[0], (1, 0, 2))
    ys, pool_s = _single_step(_s1_kernel, "s1", [sds(n_rows, D_MODEL), sds(POOL_MAX - 1, n_seq, D_MODEL)],
                              xs2, buf_t, gmix[1], wpool, pscale, gmlp[1], wup[1], wdn[1])
    y_sample = ys.reshape(n_t, n_seq, D_MODEL).transpose(1, 0, 2)

    kv5 = lambda a: a.reshape(1, a.shape[0], WINDOW, N_KV, HEAD_DIM)
    return (y_prompt, y_sample,
            conv_p[None], jnp.transpose(conv_s, (1, 0, 2))[None],
            kv5(k_p), kv5(k_s), kv5(v_p), kv5(v_s),
            pool_p[None], jnp.transpose(pool_s, (1, 0, 2))[None])
```

```python
import functools

import jax
import jax.numpy as jnp
from jax import lax
from jax.experimental import pallas as pl
from jax.experimental.pallas import tpu as pltpu

F32 = jnp.float32
BF16 = jnp.bfloat16

D_MODEL = 1024
D_CONV = 512
CONV_W = 3
HEAD_DIM = 64
N_HEADS = 8
N_KV = 2
GROUP = N_HEADS // N_KV
WINDOW = 128
BLOCK = 128
Q_DIM = N_HEADS * HEAD_DIM
KV_DIM = N_KV * HEAD_DIM
POOL_WINDOWS = (2, 4, 8, 16)
POOL_GROUP_DIM = D_MODEL // len(POOL_WINDOWS)
POOL_MAX = 16
D_FF = 4 * D_MODEL
D_IN_EVEN = 3 * D_CONV + Q_DIM + 2 * KV_DIM
N_META = 16
EPS = 1e-6
NEG = -1e30

META_PAD = BLOCK - N_META
SEQ_TILE = 512
FF_CHUNK = 1024
CONV_HDR = 8
SAMPLE_BB = 16
SAMPLE_KEYS = 144
VMEM_LIMIT = 56 * 1024 * 1024


def _dot(a, b):
    return jnp.dot(a, b, preferred_element_type=F32)


def _dot_nt(a, b):
    return lax.dot_general(a, b, (((1,), (1,)), ((), ())), preferred_element_type=F32)


def _dot_tn(a, b):
    return lax.dot_general(a, b, (((0,), (0,)), ((), ())), preferred_element_type=F32)


def _rms(x, g):
    ms = jnp.mean(x * x, axis=-1, keepdims=True)
    return x * lax.rsqrt(ms + EPS) * g


def _head_rms(x, seg, g):
    x2 = x * x
    hi = x2.astype(BF16)
    lo = (x2 - hi.astype(F32)).astype(BF16)
    ms = _dot(hi, seg) + _dot(lo, seg)
    return x * lax.rsqrt(ms + EPS) * g


def _mlp(x, g, wup_ref, wdn_ref):
    xn = _rms(x, g).astype(BF16)
    acc = x
    for c in range(D_FF // FF_CHUNK):
        cols = slice(c * FF_CHUNK, (c + 1) * FF_CHUNK)
        h = _dot(xn, wup_ref[:, cols])
        a = jnp.square(jnp.maximum(h, 0.0)).astype(BF16)
        acc = acc + _dot(a, wdn_ref[cols, :])
    return acc


def _in_proj(x, g, win_ref, segq_ref, segk_ref, qg, kg):
    hn = _rms(x, g).astype(BF16)
    z = _dot(hn, win_ref[...])
    xa = z[:, 0:D_CONV]
    gc = z[:, D_CONV:2 * D_CONV]
    gb = z[:, 2 * D_CONV:3 * D_CONV]
    q = z[:, 3 * D_CONV:3 * D_CONV + Q_DIM]
    k = z[:, 3 * D_CONV + Q_DIM:3 * D_CONV + Q_DIM + KV_DIM]
    v = z[:, 3 * D_CONV + Q_DIM + KV_DIM:]
    qn = _head_rms(q, segq_ref[...], qg) * (HEAD_DIM ** -0.5)
    kn = _head_rms(k, segk_ref[...], kg)
    return gc * xa, gb, qn, kn, v


def _l0_tile(x, rows, sinks_ref, gmix, win_ref, segq_ref, segk_ref, qg, kg, cw_ref, wout_ref, gmlp,
             wup_ref, wdn_ref, u_scr, k_scr, v_scr, first_scr, band_scr):
    u, gb, qn, kn, v = _in_proj(x, gmix, win_ref, segq_ref, segk_ref, qg, kg)

    u_scr[CONV_HDR:CONV_HDR + rows, :] = u
    u1 = u_scr[CONV_HDR - 1:CONV_HDR - 1 + rows, :]
    u2 = u_scr[CONV_HDR - 2:CONV_HDR - 2 + rows, :]
    ya = gb * (u2 * cw_ref[0:1, :] + u1 * cw_ref[1:2, :] + u * cw_ref[2:3, :])
    u_scr[CONV_HDR - 2:CONV_HDR, :] = u_scr[CONV_HDR - 2 + rows:CONV_HDR + rows, :]

    k_scr[BLOCK:BLOCK + rows, :] = kn
    v_scr[BLOCK:BLOCK + rows, :] = v
    yb_blocks = []
    for i in range(rows // BLOCK):
        qb = qn[i * BLOCK:(i + 1) * BLOCK, :]
        bias = first_scr[...] if i == 0 else band_scr[...]
        heads_t = []
        for j in range(N_KV):
            lanes = slice(j * HEAD_DIM, (j + 1) * HEAD_DIM)
            k2 = k_scr[i * BLOCK:(i + 2) * BLOCK, lanes].astype(BF16)
            v2 = v_scr[i * BLOCK:(i + 2) * BLOCK, lanes].astype(BF16)
            qs = jnp.concatenate(
                [qb[:, (j * GROUP + g) * HEAD_DIM:(j * GROUP + g + 1) * HEAD_DIM] for g in range(GROUP)],
                axis=0).astype(BF16)
            sk = jnp.concatenate(
                [jnp.full((1, BLOCK), sinks_ref[j * GROUP + g], F32) for g in range(GROUP)], axis=1)
            st = _dot_nt(k2, qs) + bias
            m = jnp.maximum(jnp.max(st, axis=0, keepdims=True), sk)
            p = jnp.exp(st - m)
            den = jnp.sum(p, axis=0, keepdims=True) + jnp.exp(sk - m)
            ot = _dot_tn(v2, p.astype(BF16)) / den
            heads_t += [ot[:, g * BLOCK:(g + 1) * BLOCK] for g in range(GROUP)]
        yb_blocks.append(jnp.concatenate(heads_t, axis=0).T)
    yb = yb_blocks[0] if len(yb_blocks) == 1 else jnp.concatenate(yb_blocks, axis=0)
    k_scr[0:BLOCK, :] = k_scr[rows:rows + BLOCK, :]
    v_scr[0:BLOCK, :] = v_scr[rows:rows + BLOCK, :]

    mix = jnp.concatenate([ya, yb], axis=-1).astype(BF16)
    x1 = x + _dot(mix, wout_ref[...])
    return _mlp(x1, gmlp, wup_ref, wdn_ref)


def _l0_prompt_kernel(sinks_ref, x_ref, meta_ref, gmix_ref, win_ref, segq_ref, segk_ref, qg_ref, kg_ref,
                      cw_ref, wout_ref, gmlp_ref, wup_ref, wdn_ref,
                      y_ref, ymeta_ref, conv_ref, kout_ref, vout_ref,
                      u_scr, k_scr, v_scr, first_scr, band_scr):
    s = pl.program_id(1)
    tile = functools.partial(
        _l0_tile, sinks_ref=sinks_ref, gmix=gmix_ref[...], win_ref=win_ref, segq_ref=segq_ref,
        segk_ref=segk_ref, qg=qg_ref[...], kg=kg_ref[...], cw_ref=cw_ref, wout_ref=wout_ref,
        gmlp=gmlp_ref[...], wup_ref=wup_ref, wdn_ref=wdn_ref, u_scr=u_scr, k_scr=k_scr, v_scr=v_scr,
        first_scr=first_scr, band_scr=band_scr)

    def band_bias(first_valid_key):
        c = lax.broadcasted_iota(jnp.int32, (2 * BLOCK, GROUP * BLOCK), 0)
        q = lax.broadcasted_iota(jnp.int32, (2 * BLOCK, GROUP * BLOCK), 1) & (BLOCK - 1)
        return jnp.where((c >= q) & (c <= q + WINDOW) & (c >= first_valid_key), 0.0, NEG)

    @pl.when(s == 0)
    def _meta():
        band_scr[...] = band_bias(0)
        u_scr[0:CONV_HDR, :] = jnp.zeros((CONV_HDR, D_CONV), F32)
        k_scr[0:BLOCK, :] = jnp.zeros((BLOCK, KV_DIM), F32)
        v_scr[0:BLOCK, :] = jnp.zeros((BLOCK, KV_DIM), F32)
        first_scr[...] = band_bias(BLOCK + META_PAD)
        ymeta_ref[...] = tile(meta_ref[...], BLOCK)
        first_scr[...] = band_bias(META_PAD)

    @pl.when(s > 0)
    def _main():
        y_ref[...] = tile(x_ref[...], SEQ_TILE)

    @pl.when(s == 1)
    def _past_meta():
        first_scr[...] = band_scr[...]

    @pl.when(s == pl.num_programs(1) - 1)
    def _state():
        conv_ref[...] = u_scr[CONV_HDR - 2:CONV_HDR, :]
        kout_ref[...] = k_scr[0:BLOCK, :]
        vout_ref[...] = v_scr[0:BLOCK, :]


def _l1_tile(x, rows, is_meta, gmix, wpool_ref, pscale, gmlp, wup_ref, wdn_ref, h_scr):
    h = _rms(x, gmix)
    h_scr[POOL_MAX:POOL_MAX + rows, :] = h
    ys = []
    for gi, w in enumerate(POOL_WINDOWS):
        cols = slice(gi * POOL_GROUP_DIM, (gi + 1) * POOL_GROUP_DIM)
        hg = h[:, cols]
        acc = hg
        for j in range(1, w):
            acc = acc + h_scr[POOL_MAX - j:POOL_MAX - j + rows, cols]
        if is_meta:
            r = lax.broadcasted_iota(jnp.int32, (rows, 1), 0)
            pooled = acc / jnp.clip(r - (META_PAD - 1), 1, w).astype(F32)
        else:
            pooled = acc * (1.0 / w)
        ys.append(_dot((pooled - hg).astype(BF16), wpool_ref[gi]))
    h_scr[0:POOL_MAX, :] = h_scr[rows:rows + POOL_MAX, :]
    x1 = x + jnp.concatenate(ys, axis=-1) * pscale
    return _mlp(x1, gmlp, wup_ref, wdn_ref)


def _l1_prompt_kernel(x_ref, meta_ref, gmix_ref, wpool_ref, pscale_ref, gmlp_ref, wup_ref, wdn_ref,
                      y_ref, pool_ref, h_scr):
    s = pl.program_id(1)
    tile = functools.partial(_l1_tile, gmix=gmix_ref[...], wpool_ref=wpool_ref, pscale=pscale_ref[...],
                             gmlp=gmlp_ref[...], wup_ref=wup_ref, wdn_ref=wdn_ref, h_scr=h_scr)

    @pl.when(s == 0)
    def _meta():
        h_scr[0:POOL_MAX, :] = jnp.zeros((POOL_MAX, D_MODEL), F32)
        tile(meta_ref[...], BLOCK, True)

    @pl.when(s > 0)
    def _main():
        y_ref[...] = tile(x_ref[...], SEQ_TILE, False)

    @pl.when(s == pl.num_programs(1) - 1)
    def _state():
        pool_ref[...] = h_scr[1:POOL_MAX, :]


def _s0_pre_kernel(x_ref, st_ref, gmix_ref, win_ref, segq_ref, segk_ref, qg_ref, kg_ref, cw_ref,
                   ya_ref, q_ref, k_ref, v_ref, conv_ref):
    n_seq = st_ref.shape[1]
    n_t = x_ref.shape[0] // n_seq
    u, gb, qn, kn, v = _in_proj(x_ref[...], gmix_ref[...], win_ref, segq_ref, segk_ref, qg_ref[...],
                                kg_ref[...])
    ue = [st_ref[0], st_ref[1]] + [u[t * n_seq:(t + 1) * n_seq, :] for t in range(n_t)]
    for t in range(n_t):
        y = ue[t] * cw_ref[0:1, :] + ue[t + 1] * cw_ref[1:2, :] + ue[t + 2] * cw_ref[2:3, :]
        ya_ref[t * n_seq:(t + 1) * n_seq, :] = gb[t * n_seq:(t + 1) * n_seq, :] * y
    q_ref[...] = qn
    k_ref[...] = kn
    v_ref[...] = v
    conv_ref[0] = ue[-2]
    conv_ref[1] = ue[-1]


def _s0_attn_kernel(q_ref, ck_ref, cv_ref, kn_ref, vn_ref, sk_ref, o_ref, kout_ref, vout_ref, ke_scr, ve_scr):
    bb, n_t = kn_ref.shape[0], kn_ref.shape[1]
    wb = ck_ref.shape[1]
    pad = SAMPLE_KEYS - wb - n_t
    for src, new, scr, out in ((ck_ref, kn_ref, ke_scr, kout_ref), (cv_ref, vn_ref, ve_scr, vout_ref)):
        scr[:, 0:wb, :] = src[...]
        scr[:, wb:wb + n_t, :] = new[...]
        scr[:, wb + n_t:, :] = jnp.zeros((bb, pad, KV_DIM), F32)
        out[...] = scr[:, n_t:n_t + wb, :]
    rows = n_t * GROUP
    t = lax.broadcasted_iota(jnp.int32, (rows, SAMPLE_KEYS), 0) // GROUP
    c = lax.broadcasted_iota(jnp.int32, (rows, SAMPLE_KEYS), 1)
    dist = wb + t - c
    bias = jnp.where((dist >= 0) & (dist <= WINDOW), 0.0, NEG)[None]
    for j in range(N_KV):
        lanes = slice(j * HEAD_DIM, (j + 1) * HEAD_DIM)
        q = q_ref[:, j].astype(BF16)
        k = ke_scr[:, :, lanes].astype(BF16)
        v = ve_scr[:, :, lanes].astype(BF16)
        sk = sk_ref[j][None]
        s = jnp.einsum('bqd,bkd->bqk', q, k, preferred_element_type=F32) + bias
        m = jnp.maximum(jnp.max(s, axis=-1, keepdims=True), sk)
        p = jnp.exp(s - m)
        den = jnp.sum(p, axis=-1, keepdims=True) + jnp.exp(sk - m)
        o = jnp.einsum('bqk,bkd->bqd', p.astype(BF16), v, preferred_element_type=F32)
        o_ref[:, j] = o / den


def _s0_post_kernel(x_ref, ya_ref, yb_ref, wout_ref, gmlp_ref, wup_ref, wdn_ref, y_ref):
    mix = jnp.concatenate([ya_ref[...], yb_ref[...]], axis=-1).astype(BF16)
    x1 = x_ref[...] + _dot(mix, wout_ref[...])
    y_ref[...] = _mlp(x1, gmlp_ref[...], wup_ref, wdn_ref)


def _s1_kernel(x_ref, buf_ref, gmix_ref, wpool_ref, pscale_ref, gmlp_ref, wup_ref, wdn_ref, y_ref, pool_ref):
    n_buf, n_seq = buf_ref.shape[0], buf_ref.shape[1]
    n_t = x_ref.shape[0] // n_seq
    x = x_ref[...]
    h = _rms(x, gmix_ref[...])
    he = [buf_ref[i] for i in range(n_buf)] + [h[t * n_seq:(t + 1) * n_seq, :] for t in range(n_t)]
    for i in range(n_buf):
        pool_ref[i] = he[n_t + i]
    ys = []
    for gi, w in enumerate(POOL_WINDOWS):
        cols = slice(gi * POOL_GROUP_DIM, (gi + 1) * POOL_GROUP_DIM)
        dl = []
        for t in range(n_t):
            acc = he[n_buf + t][:, cols]
            for j in range(1, w):
                acc = acc + he[n_buf + t - j][:, cols]
            dl.append(acc * (1.0 / w) - he[n_buf + t][:, cols])
        ys.append(_dot(jnp.concatenate(dl, axis=0).astype(BF16), wpool_ref[gi]))
    x1 = x + jnp.concatenate(ys, axis=-1) * pscale_ref[...]
    y_ref[...] = _mlp(x1, gmlp_ref[...], wup_ref, wdn_ref)


def _resident(shape, n_grid):
    zeros = (0,) * len(shape)
    index_map = {1: lambda i: zeros, 2: lambda i, j: zeros}[n_grid]
    return pl.BlockSpec(shape, index_map, pipeline_mode=pl.Buffered(1))


def _params(n_grid):
    return pltpu.CompilerParams(dimension_semantics=("arbitrary",) * n_grid, vmem_limit_bytes=VMEM_LIMIT)


def _l0_prompt(x, meta_pad, sinks, gmix, win, segq, segk, qg, kg, cw, wout, gmlp, wup, wdn):
    n_b, seq, _ = x.shape
    n_tiles = seq // SEQ_TILE
    res = functools.partial(_resident, n_grid=2)
    tile_spec = pl.BlockSpec((None, SEQ_TILE, D_MODEL), lambda b, s: (b, jnp.maximum(s - 1, 0), 0))
    per_b = lambda r, c: pl.BlockSpec((None, r, c), lambda b, s: (b, 0, 0))
    return pl.pallas_call(
        _l0_prompt_kernel,
        grid=(n_b, n_tiles + 1),
        in_specs=[pl.BlockSpec(memory_space=pltpu.SMEM), tile_spec, res((BLOCK, D_MODEL)), res((1, D_MODEL)),
                  res((D_MODEL, D_IN_EVEN)), res((Q_DIM, Q_DIM)), res((KV_DIM, KV_DIM)), res((1, Q_DIM)),
                  res((1, KV_DIM)), res((CONV_W, D_CONV)), res((D_MODEL, D_MODEL)), res((1, D_MODEL)),
                  res((D_MODEL, D_FF)), res((D_FF, D_MODEL))],
        out_specs=[tile_spec, per_b(BLOCK, D_MODEL), per_b(CONV_W - 1, D_CONV), per_b(WINDOW, KV_DIM),
                   per_b(WINDOW, KV_DIM)],
        out_shape=[jax.ShapeDtypeStruct((n_b, seq, D_MODEL), F32),
                   jax.ShapeDtypeStruct((n_b, BLOCK, D_MODEL), F32),
                   jax.ShapeDtypeStruct((n_b, CONV_W - 1, D_CONV), F32),
                   jax.ShapeDtypeStruct((n_b, WINDOW, KV_DIM), F32),
                   jax.ShapeDtypeStruct((n_b, WINDOW, KV_DIM), F32)],
        scratch_shapes=[pltpu.VMEM((CONV_HDR + SEQ_TILE, D_CONV), F32),
                        pltpu.VMEM((BLOCK + SEQ_TILE, KV_DIM), F32),
                        pltpu.VMEM((BLOCK + SEQ_TILE, KV_DIM), F32),
                        pltpu.VMEM((2 * BLOCK, GROUP * BLOCK), F32),
                        pltpu.VMEM((2 * BLOCK, GROUP * BLOCK), F32)],
        compiler_params=_params(2),
        name="l0_prompt",
    )(sinks, x, meta_pad, gmix, win, segq, segk, qg, kg, cw, wout, gmlp, wup, wdn)


def _l1_prompt(x, xmeta, gmix, wpool, pscale, gmlp, wup, wdn):
    n_b, seq, _ = x.shape
    n_tiles = seq // SEQ_TILE
    res = functools.partial(_resident, n_grid=2)
    tile_spec = pl.BlockSpec((None, SEQ_TILE, D_MODEL), lambda b, s: (b, jnp.maximum(s - 1, 0), 0))
    per_b = lambda r, c: pl.BlockSpec((None, r, c), lambda b, s: (b, 0, 0))
    return pl.pallas_call(
        _l1_prompt_kernel,
        grid=(n_b, n_tiles + 1),
        in_specs=[tile_spec, per_b(BLOCK, D_MODEL), res((1, D_MODEL)),
                  res((len(POOL_WINDOWS), POOL_GROUP_DIM, POOL_GROUP_DIM)), res((1, D_MODEL)),
                  res((1, D_MODEL)), res((D_MODEL, D_FF)), res((D_FF, D_MODEL))],
        out_specs=[tile_spec, per_b(POOL_MAX - 1, D_MODEL)],
        out_shape=[jax.ShapeDtypeStruct((n_b, seq, D_MODEL), F32),
                   jax.ShapeDtypeStruct((n_b, POOL_MAX - 1, D_MODEL), F32)],
        scratch_shapes=[pltpu.VMEM((POOL_MAX + SEQ_TILE, D_MODEL), F32)],
        compiler_params=_params(2),
        name="l1_prompt",
    )(x, xmeta, gmix, wpool, pscale, gmlp, wup, wdn)


def _single_step(kernel, name, out_shape, *args):
    res = functools.partial(_resident, n_grid=1)
    return pl.pallas_call(
        kernel,
        grid=(1,),
        in_specs=[res(a.shape) for a in args],
        out_specs=[pl.BlockSpec(o.shape, lambda i, n=len(o.shape): (0,) * n) for o in out_shape],
        out_shape=out_shape,
        compiler_params=_params(1),
        name=name,
    )(*args)


def _s0_attn(q4, ck, cv, kn, vn, sk_rows):
    n_seq, wb, _ = ck.shape
    n_t = kn.shape[1]
    rows = n_t * GROUP
    blk = lambda *tail: pl.BlockSpec((SAMPLE_BB,) + tail, lambda i: (i,) + (0,) * len(tail))
    return pl.pallas_call(
        _s0_attn_kernel,
        grid=(n_seq // SAMPLE_BB,),
        in_specs=[blk(N_KV, rows, HEAD_DIM), blk(wb, KV_DIM), blk(wb, KV_DIM), blk(n_t, KV_DIM),
                  blk(n_t, KV_DIM), pl.BlockSpec((N_KV, rows, 1), lambda i: (0, 0, 0))],
        out_specs=[blk(N_KV, rows, HEAD_DIM), blk(wb, KV_DIM), blk(wb, KV_DIM)],
        out_shape=[jax.ShapeDtypeStruct((n_seq, N_KV, rows, HEAD_DIM), F32),
                   jax.ShapeDtypeStruct((n_seq, wb, KV_DIM), F32),
                   jax.ShapeDtypeStruct((n_seq, wb, KV_DIM), F32)],
        scratch_shapes=[pltpu.VMEM((SAMPLE_BB, SAMPLE_KEYS, KV_DIM), F32),
                        pltpu.VMEM((SAMPLE_BB, SAMPLE_KEYS, KV_DIM), F32)],
        compiler_params=_params(1),
        name="s0_attn",
    )(q4, ck, cv, kn, vn, sk_rows)


def kernel(x_prompt, x_sample, state_conv, cache_k_win, cache_v_win, state_pool, meta_tokens, norm_mix, norm_mlp, w_in_even, conv_w, q_norm, k_norm, attn_sinks, w_out_even, w_pool, pool_scale, w_up, w_down):
    n_b = x_prompt.shape[0]
    n_seq, n_t, _ = x_sample.shape
    wb = cache_k_win.shape[2]
    assert x_prompt.shape[1] % SEQ_TILE == 0 and n_seq % SAMPLE_BB == 0
    assert wb == WINDOW and wb + n_t <= SAMPLE_KEYS

    win = w_in_even[0].astype(BF16)
    wout = w_out_even[0].astype(BF16)
    wpool = w_pool[0].astype(BF16)
    wup = [w_up[l].astype(BF16) for l in range(2)]
    wdn = [w_down[l].astype(BF16) for l in range(2)]
    gmix = [norm_mix[l][None, :] for l in range(2)]
    gmlp = [norm_mlp[l][None, :] for l in range(2)]
    qg = jnp.tile(q_norm[0], N_HEADS)[None, :]
    kg = jnp.tile(k_norm[0], N_KV)[None, :]
    segq = jnp.kron(jnp.eye(N_HEADS, dtype=F32), jnp.full((HEAD_DIM, HEAD_DIM), 1.0 / HEAD_DIM, F32)).astype(BF16)
    segk = segq[:KV_DIM, :KV_DIM]
    cw = conv_w[0]
    sinks = attn_sinks[0]
    pscale = pool_scale[0][None, :]
    meta_pad = jnp.pad(meta_tokens, ((META_PAD, 0), (0, 0)))

    x2, x2_meta, conv_p, k_p, v_p = _l0_prompt(x_prompt, meta_pad, sinks, gmix[0], win, segq, segk, qg, kg, cw,
                                               wout, gmlp[0], wup[0], wdn[0])
    y_prompt, pool_p = _l1_prompt(x2, x2_meta, gmix[1], wpool, pscale, gmlp[1], wup[1], wdn[1])

    n_rows = n_t * n_seq
    xs = jnp.transpose(x_sample, (1, 0, 2)).reshape(n_rows, D_MODEL)
    st_conv = jnp.transpose(state_conv[0], (1, 0, 2))
    sds = lambda *shape: jax.ShapeDtypeStruct(shape, F32)
    ya, qn, kn, vn, conv_s = _single_step(
        _s0_pre_kernel, "s0_pre",
        [sds(n_rows, D_CONV), sds(n_rows, Q_DIM), sds(n_rows, KV_DIM), sds(n_rows, KV_DIM),
         sds(CONV_W - 1, n_seq, D_CONV)],
        xs, st_conv, gmix[0], win, segq, segk, qg, kg, cw)
    q4 = qn.reshape(n_t, n_seq, N_KV, GROUP, HEAD_DIM).transpose(1, 2, 0, 3, 4)
    q4 = q4.reshape(n_seq, N_KV, n_t * GROUP, HEAD_DIM)
    kn_b = kn.reshape(n_t, n_seq, KV_DIM).transpose(1, 0, 2)
    vn_b = vn.reshape(n_t, n_seq, KV_DIM).transpose(1, 0, 2)
    sk_rows = jnp.tile(sinks.reshape(N_KV, 1, GROUP), (1, n_t, 1)).reshape(N_KV, n_t * GROUP, 1)
    o4, k_s, v_s = _s0_attn(q4, cache_k_win[0].reshape(n_seq, wb, KV_DIM),
                            cache_v_win[0].reshape(n_seq, wb, KV_DIM), kn_b, vn_b, sk_rows)
    yb = o4.reshape(n_seq, N_KV, n_t, GROUP, HEAD_DIM).transpose(2, 0, 1, 3, 4).reshape(n_rows, Q_DIM)
    xs2, = _single_step(_s0_post_kernel, "s0_post", [sds(n_rows, D_MODEL)],
                        xs, ya, yb, wout, gmlp[0], wup[0], wdn[0])
    buf_t = jnp.transpose(state_pool[0], (1, 0, 2))
    ys, pool_s = _single_step(_s1_kernel, "s1", [sds(n_rows, D_MODEL), sds(POOL_MAX - 1, n_seq, D_MODEL)],
                              xs2, buf_t, gmix[1], wpool, pscale, gmlp[1], wup[1], wdn[1])
    y_sample = ys.reshape(n_t, n_seq, D_MODEL).transpose(1, 0, 2)

    kv5 = lambda a: a.reshape(1, a.shape[0], WINDOW, N_KV, HEAD_DIM)
    return (y_prompt, y_sample,
            conv_p[None], jnp.transpose(conv_s, (1, 0, 2))[None],
            kv5(k_p), kv5(k_s), kv5(v_p), kv5(v_s),
            pool_p[None], jnp.transpose(pool_s, (1, 0, 2))[None])
```

```python
import functools
from typing import NamedTuple

import jax
import jax.numpy as jnp
from jax import lax
from jax.experimental import pallas as pl
from jax.experimental.pallas import tpu as pltpu

F32 = jnp.float32
BF16 = jnp.bfloat16

D_MODEL = 1024
D_CONV = 512
CONV_W = 3
HEAD_DIM = 64
N_HEADS = 8
N_KV = 2
GROUP = N_HEADS // N_KV
WINDOW = 128
BLOCK = 128
Q_DIM = N_HEADS * HEAD_DIM
KV_DIM = N_KV * HEAD_DIM
POOL_WINDOWS = (2, 4, 8, 16)
POOL_GROUP_DIM = D_MODEL // len(POOL_WINDOWS)
POOL_MAX = 16
D_FF = 4 * D_MODEL
D_IN_EVEN = 3 * D_CONV + Q_DIM + 2 * KV_DIM
N_META = 16
EPS = 1e-6
NEG = -1e30

META_PAD = BLOCK - N_META
SEQ_TILE = 512
SUB_TILE = 256
FF_CHUNK = 1024
CONV_HDR = 8
SAMPLE_BB = 16
SAMPLE_KEYS = 144
VMEM_LIMIT = 56 * 1024 * 1024


def _dot(a, b):
    return jnp.dot(a, b, preferred_element_type=F32)


def _dot_nt(a, b):
    return lax.dot_general(a, b, (((1,), (1,)), ((), ())), preferred_element_type=F32)


def _dot_tn(a, b):
    return lax.dot_general(a, b, (((0,), (0,)), ((), ())), preferred_element_type=F32)


def _rms(x, g):
    ms = jnp.mean(x * x, axis=-1, keepdims=True)
    return x * lax.rsqrt(ms + EPS) * g


def _head_rms(x, seg, g):
    x2 = x * x
    hi = x2.astype(BF16)
    lo = (x2 - hi.astype(F32)).astype(BF16)
    ms = _dot(hi, seg) + _dot(lo, seg)
    return x * lax.rsqrt(ms + EPS) * g


def _interleave(*steppers):
    results = [None] * len(steppers)
    live = list(enumerate(steppers))
    while live:
        still = []
        for idx, stepper in live:
            try:
                next(stepper)
                still.append((idx, stepper))
            except StopIteration as done:
                results[idx] = done.value
        live = still
    return results


def _mlp_steps(x, g, wup_ref, wdn_ref):
    xn = _rms(x, g).astype(BF16)
    acc = x
    for c in range(D_FF // FF_CHUNK):
        cols = slice(c * FF_CHUNK, (c + 1) * FF_CHUNK)
        h = _dot(xn, wup_ref[:, cols])
        a = jnp.square(jnp.maximum(h, 0.0)).astype(BF16)
        acc = acc + _dot(a, wdn_ref[cols, :])
        yield
    return acc


def _mlp(x, g, wup_ref, wdn_ref):
    return _interleave(_mlp_steps(x, g, wup_ref, wdn_ref))[0]


def _in_proj(x, g, win_ref, segq_ref, segk_ref, qg, kg):
    hn = _rms(x, g).astype(BF16)
    z = _dot(hn, win_ref[...])
    xa = z[:, 0:D_CONV]
    gc = z[:, D_CONV:2 * D_CONV]
    gb = z[:, 2 * D_CONV:3 * D_CONV]
    q = z[:, 3 * D_CONV:3 * D_CONV + Q_DIM]
    k = z[:, 3 * D_CONV + Q_DIM:3 * D_CONV + Q_DIM + KV_DIM]
    v = z[:, 3 * D_CONV + Q_DIM + KV_DIM:]
    qn = _head_rms(q, segq_ref[...], qg) * (HEAD_DIM ** -0.5)
    kn = _head_rms(k, segk_ref[...], kg)
    return gc * xa, gb, qn, kn, v


def _l0_tile(x, rows, off, sinks_ref, gmix, win_ref, segq_ref, segk_ref, qg, kg, cw_ref, wout_ref, gmlp,
             wup_ref, wdn_ref, u_scr, k_scr, v_scr, first_scr, band_scr):
    u, gb, qn, kn, v = _in_proj(x, gmix, win_ref, segq_ref, segk_ref, qg, kg)

    base = CONV_HDR + off
    u_scr[base:base + rows, :] = u
    u1 = u_scr[base - 1:base - 1 + rows, :]
    u2 = u_scr[base - 2:base - 2 + rows, :]
    ya = gb * (u2 * cw_ref[0:1, :] + u1 * cw_ref[1:2, :] + u * cw_ref[2:3, :])

    k_scr[BLOCK + off:BLOCK + off + rows, :] = kn
    v_scr[BLOCK + off:BLOCK + off + rows, :] = v
    yb_blocks = []
    for i in range(rows // BLOCK):
        kb = off + i * BLOCK
        qb = qn[i * BLOCK:(i + 1) * BLOCK, :]
        bias = first_scr[...] if kb == 0 else band_scr[...]
        heads_t = []
        for j in range(N_KV):
            lanes = slice(j * HEAD_DIM, (j + 1) * HEAD_DIM)
            k2 = k_scr[kb:kb + 2 * BLOCK, lanes].astype(BF16)
            v2 = v_scr[kb:kb + 2 * BLOCK, lanes].astype(BF16)
            qs = jnp.concatenate(
                [qb[:, (j * GROUP + g) * HEAD_DIM:(j * GROUP + g + 1) * HEAD_DIM] for g in range(GROUP)],
                axis=0).astype(BF16)
            sk = jnp.concatenate(
                [jnp.full((1, BLOCK), sinks_ref[j * GROUP + g], F32) for g in range(GROUP)], axis=1)
            st = _dot_nt(k2, qs) + bias
            m = jnp.maximum(jnp.max(st, axis=0, keepdims=True), sk)
            p = jnp.exp(st - m)
            den = jnp.sum(p, axis=0, keepdims=True) + jnp.exp(sk - m)
            ot = _dot_tn(v2, p.astype(BF16)) / den
            heads_t += [ot[:, g * BLOCK:(g + 1) * BLOCK] for g in range(GROUP)]
        yb_blocks.append(jnp.concatenate(heads_t, axis=0).T)
    yb = yb_blocks[0] if len(yb_blocks) == 1 else jnp.concatenate(yb_blocks, axis=0)

    mix = jnp.concatenate([ya, yb], axis=-1).astype(BF16)
    x1 = x + _dot(mix, wout_ref[...])
    return _mlp(x1, gmlp, wup_ref, wdn_ref)


def _l0_carry(rows, u_scr, k_scr, v_scr):
    u_scr[CONV_HDR - 2:CONV_HDR, :] = u_scr[CONV_HDR - 2 + rows:CONV_HDR + rows, :]
    k_scr[0:BLOCK, :] = k_scr[rows:rows + BLOCK, :]
    v_scr[0:BLOCK, :] = v_scr[rows:rows + BLOCK, :]


def _l0_prompt_kernel(sinks_ref, x_ref, meta_ref, gmix_ref, win_ref, segq_ref, segk_ref, qg_ref, kg_ref,
                      cw_ref, wout_ref, gmlp_ref, wup_ref, wdn_ref,
                      y_ref, ymeta_ref, conv_ref, kout_ref, vout_ref,
                      u_scr, k_scr, v_scr, first_scr, band_scr):
    s = pl.program_id(1)
    tile = functools.partial(
        _l0_tile, sinks_ref=sinks_ref, gmix=gmix_ref[...], win_ref=win_ref, segq_ref=segq_ref,
        segk_ref=segk_ref, qg=qg_ref[...], kg=kg_ref[...], cw_ref=cw_ref, wout_ref=wout_ref,
        gmlp=gmlp_ref[...], wup_ref=wup_ref, wdn_ref=wdn_ref, u_scr=u_scr, k_scr=k_scr, v_scr=v_scr,
        first_scr=first_scr, band_scr=band_scr)

    def band_bias(first_valid_key):
        c = lax.broadcasted_iota(jnp.int32, (2 * BLOCK, GROUP * BLOCK), 0)
        q = lax.broadcasted_iota(jnp.int32, (2 * BLOCK, GROUP * BLOCK), 1) & (BLOCK - 1)
        return jnp.where((c >= q) & (c <= q + WINDOW) & (c >= first_valid_key), 0.0, NEG)

    @pl.when(s == 0)
    def _meta():
        band_scr[...] = band_bias(0)
        u_scr[0:CONV_HDR, :] = jnp.zeros((CONV_HDR, D_CONV), F32)
        k_scr[0:BLOCK, :] = jnp.zeros((BLOCK, KV_DIM), F32)
        v_scr[0:BLOCK, :] = jnp.zeros((BLOCK, KV_DIM), F32)
        first_scr[...] = band_bias(BLOCK + META_PAD)
        ymeta_ref[...] = tile(meta_ref[...], BLOCK, 0)
        _l0_carry(BLOCK, u_scr, k_scr, v_scr)
        first_scr[...] = band_bias(META_PAD)

    @pl.when(s > 0)
    def _main():
        for off in range(0, SEQ_TILE, SUB_TILE):
            y_ref[off:off + SUB_TILE, :] = tile(x_ref[off:off + SUB_TILE, :], SUB_TILE, off)
        _l0_carry(SEQ_TILE, u_scr, k_scr, v_scr)

    @pl.when(s == 1)
    def _past_meta():
        first_scr[...] = band_scr[...]

    @pl.when(s == pl.num_programs(1) - 1)
    def _state():
        conv_ref[...] = u_scr[CONV_HDR - 2:CONV_HDR, :]
        kout_ref[...] = k_scr[0:BLOCK, :]
        vout_ref[...] = v_scr[0:BLOCK, :]


def _l1_mixer(x, rows, off, is_meta, gmix, wpool_ref, pscale, h_scr):
    h = _rms(x, gmix)
    base = POOL_MAX + off
    h_scr[base:base + rows, :] = h
    ys = []
    for gi, w in enumerate(POOL_WINDOWS):
        cols = slice(gi * POOL_GROUP_DIM, (gi + 1) * POOL_GROUP_DIM)
        hg = h[:, cols]
        acc = hg
        for j in range(1, w):
            acc = acc + h_scr[base - j:base - j + rows, cols]
        if is_meta:
            r = lax.broadcasted_iota(jnp.int32, (rows, 1), 0)
            pooled = acc / jnp.clip(r - (META_PAD - 1), 1, w).astype(F32)
        else:
            pooled = acc * (1.0 / w)
        ys.append(_dot((pooled - hg).astype(BF16), wpool_ref[gi]))
        yield
    return x + jnp.concatenate(ys, axis=-1) * pscale


def _l1_prompt_kernel(x_ref, meta_ref, gmix_ref, wpool_ref, pscale_ref, gmlp_ref, wup_ref, wdn_ref,
                      y_ref, pool_ref, h_scr):
    s = pl.program_id(1)
    mixer = functools.partial(_l1_mixer, gmix=gmix_ref[...], wpool_ref=wpool_ref, pscale=pscale_ref[...],
                              h_scr=h_scr)
    mlp = functools.partial(_mlp_steps, g=gmlp_ref[...], wup_ref=wup_ref, wdn_ref=wdn_ref)

    @pl.when(s == 0)
    def _meta():
        h_scr[0:POOL_MAX, :] = jnp.zeros((POOL_MAX, D_MODEL), F32)
        _interleave(mixer(meta_ref[...], BLOCK, 0, True))
        h_scr[0:POOL_MAX, :] = h_scr[BLOCK:BLOCK + POOL_MAX, :]

    @pl.when(s > 0)
    def _main():
        offs = list(range(0, SEQ_TILE, SUB_TILE))
        x1, = _interleave(mixer(x_ref[0:SUB_TILE, :], SUB_TILE, 0, False))
        for off, nxt in zip(offs, offs[1:] + [None]):
            work = [mlp(x1)]
            if nxt is not None:
                work.append(mixer(x_ref[nxt:nxt + SUB_TILE, :], SUB_TILE, nxt, False))
            y, *rest = _interleave(*work)
            y_ref[off:off + SUB_TILE, :] = y
            x1 = rest[0] if rest else None
        h_scr[0:POOL_MAX, :] = h_scr[SEQ_TILE:SEQ_TILE + POOL_MAX, :]

    @pl.when(s == pl.num_programs(1) - 1)
    def _state():
        pool_ref[...] = h_scr[1:POOL_MAX, :]


def _s0_pre_kernel(x_ref, st_ref, gmix_ref, win_ref, segq_ref, segk_ref, qg_ref, kg_ref, cw_ref,
                   ya_ref, q_ref, k_ref, v_ref, conv_ref):
    n_seq = st_ref.shape[1]
    n_t = x_ref.shape[0] // n_seq
    u, gb, qn, kn, v = _in_proj(x_ref[...], gmix_ref[...], win_ref, segq_ref, segk_ref, qg_ref[...],
                                kg_ref[...])
    ue = [st_ref[0], st_ref[1]] + [u[t * n_seq:(t + 1) * n_seq, :] for t in range(n_t)]
    for t in range(n_t):
        y = ue[t] * cw_ref[0:1, :] + ue[t + 1] * cw_ref[1:2, :] + ue[t + 2] * cw_ref[2:3, :]
        ya_ref[t * n_seq:(t + 1) * n_seq, :] = gb[t * n_seq:(t + 1) * n_seq, :] * y
    q_ref[...] = qn
    k_ref[...] = kn
    v_ref[...] = v
    conv_ref[0] = ue[-2]
    conv_ref[1] = ue[-1]


def _s0_attn_kernel(q_ref, ck_ref, cv_ref, kn_ref, vn_ref, sk_ref, o_ref, kout_ref, vout_ref, ke_scr, ve_scr):
    bb, n_t = kn_ref.shape[0], kn_ref.shape[1]
    wb = ck_ref.shape[1]
    pad = SAMPLE_KEYS - wb - n_t
    for src, new, scr, out in ((ck_ref, kn_ref, ke_scr, kout_ref), (cv_ref, vn_ref, ve_scr, vout_ref)):
        scr[:, 0:wb, :] = src[...]
        scr[:, wb:wb + n_t, :] = new[...]
        scr[:, wb + n_t:, :] = jnp.zeros((bb, pad, KV_DIM), F32)
        out[...] = scr[:, n_t:n_t + wb, :]
    rows = n_t * GROUP
    t = lax.broadcasted_iota(jnp.int32, (rows, SAMPLE_KEYS), 0) // GROUP
    c = lax.broadcasted_iota(jnp.int32, (rows, SAMPLE_KEYS), 1)
    dist = wb + t - c
    bias = jnp.where((dist >= 0) & (dist <= WINDOW), 0.0, NEG)[None]
    for j in range(N_KV):
        lanes = slice(j * HEAD_DIM, (j + 1) * HEAD_DIM)
        q = q_ref[:, j].astype(BF16)
        k = ke_scr[:, :, lanes].astype(BF16)
        v = ve_scr[:, :, lanes].astype(BF16)
        sk = sk_ref[j][None]
        s = jnp.einsum('bqd,bkd->bqk', q, k, preferred_element_type=F32) + bias
        m = jnp.maximum(jnp.max(s, axis=-1, keepdims=True), sk)
        p = jnp.exp(s - m)
        den = jnp.sum(p, axis=-1, keepdims=True) + jnp.exp(sk - m)
        o = jnp.einsum('bqk,bkd->bqd', p.astype(BF16), v, preferred_element_type=F32)
        o_ref[:, j] = o / den


def _s0_post_kernel(x_ref, ya_ref, yb_ref, wout_ref, gmlp_ref, wup_ref, wdn_ref, y_ref):
    mix = jnp.concatenate([ya_ref[...], yb_ref[...]], axis=-1).astype(BF16)
    x1 = x_ref[...] + _dot(mix, wout_ref[...])
    y_ref[...] = _mlp(x1, gmlp_ref[...], wup_ref, wdn_ref)


def _s1_kernel(x_ref, buf_ref, gmix_ref, wpool_ref, pscale_ref, gmlp_ref, wup_ref, wdn_ref, y_ref, pool_ref):
    n_buf, n_seq = buf_ref.shape[0], buf_ref.shape[1]
    n_t = x_ref.shape[0] // n_seq
    x = x_ref[...]
    h = _rms(x, gmix_ref[...])
    he = [buf_ref[i] for i in range(n_buf)] + [h[t * n_seq:(t + 1) * n_seq, :] for t in range(n_t)]
    for i in range(n_buf):
        pool_ref[i] = he[n_t + i]
    ys = []
    for gi, w in enumerate(POOL_WINDOWS):
        cols = slice(gi * POOL_GROUP_DIM, (gi + 1) * POOL_GROUP_DIM)
        dl = []
        for t in range(n_t):
            acc = he[n_buf + t][:, cols]
            for j in range(1, w):
                acc = acc + he[n_buf + t - j][:, cols]
            dl.append(acc * (1.0 / w) - he[n_buf + t][:, cols])
        ys.append(_dot(jnp.concatenate(dl, axis=0).astype(BF16), wpool_ref[gi]))
    x1 = x + jnp.concatenate(ys, axis=-1) * pscale_ref[...]
    y_ref[...] = _mlp(x1, gmlp_ref[...], wup_ref, wdn_ref)


class _Stacked(NamedTuple):
    array: jax.Array
    layer: int


def _operand(arg):
    return arg.array if isinstance(arg, _Stacked) else arg


def _resident(arg, n_grid):
    if isinstance(arg, _Stacked):
        shape, idx = (None,) + arg.array.shape[1:], (arg.layer,) + (0,) * (arg.array.ndim - 1)
    else:
        shape, idx = arg.shape, (0,) * arg.ndim
    index_map = {1: lambda i: idx, 2: lambda i, j: idx}[n_grid]
    return pl.BlockSpec(shape, index_map, pipeline_mode=pl.Buffered(1))


def _params(n_grid):
    return pltpu.CompilerParams(dimension_semantics=("arbitrary",) * n_grid, vmem_limit_bytes=VMEM_LIMIT)


def _prompt_specs():
    tile_spec = pl.BlockSpec((None, SEQ_TILE, D_MODEL), lambda b, s: (b, jnp.maximum(s - 1, 0), 0))
    per_b = lambda r, c: pl.BlockSpec((None, r, c), lambda b, s: (b, 0, 0))
    return tile_spec, per_b


def _l0_prompt(x, sinks, *resident):
    n_b, seq, _ = x.shape
    tile_spec, per_b = _prompt_specs()
    return pl.pallas_call(
        _l0_prompt_kernel,
        grid=(n_b, seq // SEQ_TILE + 1),
        in_specs=[pl.BlockSpec(memory_space=pltpu.SMEM), tile_spec] + [_resident(a, 2) for a in resident],
        out_specs=[tile_spec, per_b(BLOCK, D_MODEL), per_b(CONV_W - 1, D_CONV), per_b(WINDOW, KV_DIM),
                   per_b(WINDOW, KV_DIM)],
        out_shape=[jax.ShapeDtypeStruct((n_b, seq, D_MODEL), F32),
                   jax.ShapeDtypeStruct((n_b, BLOCK, D_MODEL), F32),
                   jax.ShapeDtypeStruct((n_b, CONV_W - 1, D_CONV), F32),
                   jax.ShapeDtypeStruct((n_b, WINDOW, KV_DIM), F32),
                   jax.ShapeDtypeStruct((n_b, WINDOW, KV_DIM), F32)],
        scratch_shapes=[pltpu.VMEM((CONV_HDR + SEQ_TILE, D_CONV), F32),
                        pltpu.VMEM((BLOCK + SEQ_TILE, KV_DIM), F32),
                        pltpu.VMEM((BLOCK + SEQ_TILE, KV_DIM), F32),
                        pltpu.VMEM((2 * BLOCK, GROUP * BLOCK), F32),
                        pltpu.VMEM((2 * BLOCK, GROUP * BLOCK), F32)],
        compiler_params=_params(2),
        name="l0_prompt",
    )(sinks, x, *[_operand(a) for a in resident])


def _l1_prompt(x, xmeta, *resident):
    n_b, seq, _ = x.shape
    tile_spec, per_b = _prompt_specs()
    return pl.pallas_call(
        _l1_prompt_kernel,
        grid=(n_b, seq // SEQ_TILE + 1),
        in_specs=[tile_spec, per_b(BLOCK, D_MODEL)] + [_resident(a, 2) for a in resident],
        out_specs=[tile_spec, per_b(POOL_MAX - 1, D_MODEL)],
        out_shape=[jax.ShapeDtypeStruct((n_b, seq, D_MODEL), F32),
                   jax.ShapeDtypeStruct((n_b, POOL_MAX - 1, D_MODEL), F32)],
        scratch_shapes=[pltpu.VMEM((POOL_MAX + SEQ_TILE, D_MODEL), F32)],
        compiler_params=_params(2),
        name="l1_prompt",
    )(x, xmeta, *[_operand(a) for a in resident])


def _single_step(kernel, name, out_shape, *args):
    return pl.pallas_call(
        kernel,
        grid=(1,),
        in_specs=[_resident(a, 1) for a in args],
        out_specs=[pl.BlockSpec(o.shape, lambda i, n=len(o.shape): (0,) * n) for o in out_shape],
        out_shape=out_shape,
        compiler_params=_params(1),
        name=name,
    )(*[_operand(a) for a in args])


def _s0_attn(q4, ck, cv, kn, vn, sk_rows):
    n_seq, wb, _ = ck.shape
    n_t = kn.shape[1]
    rows = n_t * GROUP
    blk = lambda *tail: pl.BlockSpec((SAMPLE_BB,) + tail, lambda i: (i,) + (0,) * len(tail))
    return pl.pallas_call(
        _s0_attn_kernel,
        grid=(n_seq // SAMPLE_BB,),
        in_specs=[blk(N_KV, rows, HEAD_DIM), blk(wb, KV_DIM), blk(wb, KV_DIM), blk(n_t, KV_DIM),
                  blk(n_t, KV_DIM), pl.BlockSpec((N_KV, rows, 1), lambda i: (0, 0, 0))],
        out_specs=[blk(N_KV, rows, HEAD_DIM), blk(wb, KV_DIM), blk(wb, KV_DIM)],
        out_shape=[jax.ShapeDtypeStruct((n_seq, N_KV, rows, HEAD_DIM), F32),
                   jax.ShapeDtypeStruct((n_seq, wb, KV_DIM), F32),
                   jax.ShapeDtypeStruct((n_seq, wb, KV_DIM), F32)],
        scratch_shapes=[pltpu.VMEM((SAMPLE_BB, SAMPLE_KEYS, KV_DIM), F32),
                        pltpu.VMEM((SAMPLE_BB, SAMPLE_KEYS, KV_DIM), F32)],
        compiler_params=_params(1),
        name="s0_attn",
    )(q4, ck, cv, kn, vn, sk_rows)


def kernel(x_prompt, x_sample, state_conv, cache_k_win, cache_v_win, state_pool, meta_tokens, norm_mix, norm_mlp, w_in_even, conv_w, q_norm, k_norm, attn_sinks, w_out_even, w_pool, pool_scale, w_up, w_down):
    n_seq, n_t, _ = x_sample.shape
    wb = cache_k_win.shape[2]
    assert x_prompt.shape[1] % SEQ_TILE == 0 and n_seq % SAMPLE_BB == 0
    assert wb == WINDOW and wb + n_t <= SAMPLE_KEYS

    win = _Stacked(w_in_even.astype(BF16), 0)
    wout = _Stacked(w_out_even.astype(BF16), 0)
    wpool = _Stacked(w_pool.astype(BF16), 0)
    wup_all, wdn_all = w_up.astype(BF16), w_down.astype(BF16)
    wup = [_Stacked(wup_all, l) for l in range(2)]
    wdn = [_Stacked(wdn_all, l) for l in range(2)]
    gmix = [_Stacked(norm_mix[:, None, :], l) for l in range(2)]
    gmlp = [_Stacked(norm_mlp[:, None, :], l) for l in range(2)]
    qg = jnp.tile(q_norm[0], N_HEADS)[None, :]
    kg = jnp.tile(k_norm[0], N_KV)[None, :]
    segq = jnp.kron(jnp.eye(N_HEADS, dtype=F32), jnp.full((HEAD_DIM, HEAD_DIM), 1.0 / HEAD_DIM, F32)).astype(BF16)
    segk = segq[:KV_DIM, :KV_DIM]
    cw = _Stacked(conv_w, 0)
    sinks = attn_sinks[0]
    pscale = pool_scale
    meta_pad = jnp.pad(meta_tokens, ((META_PAD, 0), (0, 0)))

    x2, x2_meta, conv_p, k_p, v_p = _l0_prompt(x_prompt, sinks, meta_pad, gmix[0], win, segq, segk, qg, kg, cw,
                                               wout, gmlp[0], wup[0], wdn[0])
    y_prompt, pool_p = _l1_prompt(x2, x2_meta, gmix[1], wpool, pscale, gmlp[1], wup[1], wdn[1])

    n_rows = n_t * n_seq
    xs = jnp.transpose(x_sample, (1, 0, 2)).reshape(n_rows, D_MODEL)
    st_conv = jnp.transpose(state_conv[0], (1, 0, 2))
    sds = lambda *shape: jax.ShapeDtypeStruct(shape, F32)
    ya, qn, kn, vn, conv_s = _single_step(
        _s0_pre_kernel, "s0_pre",
        [sds(n_rows, D_CONV), sds(n_rows, Q_DIM), sds(n_rows, KV_DIM), sds(n_rows, KV_DIM),
         sds(CONV_W - 1, n_seq, D_CONV)],
        xs, st_conv, gmix[0], win, segq, segk, qg, kg, cw)
    q4 = qn.reshape(n_t, n_seq, N_KV, GROUP, HEAD_DIM).transpose(1, 2, 0, 3, 4)
    q4 = q4.reshape(n_seq, N_KV, n_t * GROUP, HEAD_DIM)
    kn_b = kn.reshape(n_t, n_seq, KV_DIM).transpose(1, 0, 2)
    vn_b = vn.reshape(n_t, n_seq, KV_DIM).transpose(1, 0, 2)
    sk_rows = jnp.tile(sinks.reshape(N_KV, 1, GROUP), (1, n_t, 1)).reshape(N_KV, n_t * GROUP, 1)
    o4, k_s, v_s = _s0_attn(q4, cache_k_win[0].reshape(n_seq, wb, KV_DIM),
                            cache_v_win[0].reshape(n_seq, wb, KV_DIM), kn_b, vn_b, sk_rows)
    yb = o4.reshape(n_seq, N_KV, n_t, GROUP, HEAD_DIM).transpose(2, 0, 1, 3, 4).reshape(n_rows, Q_DIM)
    xs2, = _single_step(_s0_post_kernel, "s0_post", [sds(n_rows, D_MODEL)],
                        xs, ya, yb, wout, gmlp[0], wup[0], wdn[0])
    buf_t = jnp.transpose(state_pool[0], (1, 0, 2))
    ys, pool_s = _single_step(_s1_kernel, "s1", [sds(n_rows, D_MODEL), sds(POOL_MAX - 1, n_seq, D_MODEL)],
                              xs2, buf_t, gmix[1], wpool, pscale, gmlp[1], wup[1], wdn[1])
    y_sample = ys.reshape(n_t, n_seq, D_MODEL).transpose(1, 0, 2)

    kv5 = lambda a: a.reshape(1, a.shape[0], WINDOW, N_KV, HEAD_DIM)
    return (y_prompt, y_sample,
            conv_p[None], jnp.transpose(conv_s, (1, 0, 2))[None],
            kv5(k_p), kv5(k_s), kv5(v_p), kv5(v_s),
            pool_p[None], jnp.transpose(pool_s, (1, 0, 2))[None])
```

```python
import functools
from typing import NamedTuple

import jax
import jax.numpy as jnp
from jax import lax
from jax.experimental import pallas as pl
from jax.experimental.pallas import tpu as pltpu

F32 = jnp.float32
BF16 = jnp.bfloat16

D_MODEL = 1024
D_CONV = 512
CONV_W = 3
HEAD_DIM = 64
N_HEADS = 8
N_KV = 2
GROUP = N_HEADS // N_KV
WINDOW = 128
BLOCK = 128
Q_DIM = N_HEADS * HEAD_DIM
KV_DIM = N_KV * HEAD_DIM
POOL_WINDOWS = (2, 4, 8, 16)
POOL_GROUP_DIM = D_MODEL // len(POOL_WINDOWS)
POOL_MAX = 16
D_FF = 4 * D_MODEL
D_IN_EVEN = 3 * D_CONV + Q_DIM + 2 * KV_DIM
N_META = 16
EPS = 1e-6
NEG = -1e30

META_PAD = BLOCK - N_META
SEQ_TILE = 512
FF_CHUNK = 512
CONV_HDR = 8
SAMPLE_BB = 16
SAMPLE_KEYS = 144
VMEM_LIMIT = 56 * 1024 * 1024


def _dot(a, b):
    return jnp.dot(a, b, preferred_element_type=F32)


def _dot_nt(a, b):
    return lax.dot_general(a, b, (((1,), (1,)), ((), ())), preferred_element_type=F32)


def _dot_tn(a, b):
    return lax.dot_general(a, b, (((0,), (0,)), ((), ())), preferred_element_type=F32)


def _rms(x, g):
    ms = jnp.mean(x * x, axis=-1, keepdims=True)
    return x * lax.rsqrt(ms + EPS) * g


def _head_rms(x, seg, g):
    x2 = x * x
    hi = x2.astype(BF16)
    lo = (x2 - hi.astype(F32)).astype(BF16)
    ms = _dot(hi, seg) + _dot(lo, seg)
    return x * lax.rsqrt(ms + EPS) * g


def _interleave(*steppers):
    results = [None] * len(steppers)
    live = list(enumerate(steppers))
    while live:
        still = []
        for idx, stepper in live:
            try:
                next(stepper)
                still.append((idx, stepper))
            except StopIteration as done:
                results[idx] = done.value
        live = still
    return results


def _mlp_steps(x, g, wup_ref, wdn_ref):
    xn = _rms(x, g).astype(BF16)
    acc = x
    for c in range(D_FF // FF_CHUNK):
        cols = slice(c * FF_CHUNK, (c + 1) * FF_CHUNK)
        h = _dot(xn, wup_ref[:, cols])
        a = jnp.square(jnp.maximum(h, 0.0)).astype(BF16)
        yield
        acc = acc + _dot(a, wdn_ref[cols, :])
        yield
    return acc


def _mlp(x, g, wup_ref, wdn_ref):
    return _interleave(_mlp_steps(x, g, wup_ref, wdn_ref))[0]


def _in_proj(x, g, win_ref, segq_ref, segk_ref, qg, kg):
    hn = _rms(x, g).astype(BF16)
    z = _dot(hn, win_ref[...])
    xa = z[:, 0:D_CONV]
    gc = z[:, D_CONV:2 * D_CONV]
    gb = z[:, 2 * D_CONV:3 * D_CONV]
    q = z[:, 3 * D_CONV:3 * D_CONV + Q_DIM]
    k = z[:, 3 * D_CONV + Q_DIM:3 * D_CONV + Q_DIM + KV_DIM]
    v = z[:, 3 * D_CONV + Q_DIM + KV_DIM:]
    qn = _head_rms(q, segq_ref[...], qg) * (HEAD_DIM ** -0.5)
    kn = _head_rms(k, segk_ref[...], kg)
    return gc * xa, gb, qn, kn, v


def _l0_mixer(x, rows, sinks_ref, gmix, win_ref, segq_ref, segk_ref, qg, kg, cw_ref, wout_ref,
              u_scr, k_scr, v_scr, first_scr, band_scr):
    u, gb, qn, kn, v = _in_proj(x, gmix, win_ref, segq_ref, segk_ref, qg, kg)

    u_scr[CONV_HDR:CONV_HDR + rows, :] = u
    u1 = u_scr[CONV_HDR - 1:CONV_HDR - 1 + rows, :]
    u2 = u_scr[CONV_HDR - 2:CONV_HDR - 2 + rows, :]
    ya = gb * (u2 * cw_ref[0:1, :] + u1 * cw_ref[1:2, :] + u * cw_ref[2:3, :])
    k_scr[BLOCK:BLOCK + rows, :] = kn
    v_scr[BLOCK:BLOCK + rows, :] = v
    yield

    yb_blocks = []
    for i in range(rows // BLOCK):
        qb = qn[i * BLOCK:(i + 1) * BLOCK, :]
        bias = first_scr[...] if i == 0 else band_scr[...]
        heads_t = []
        for j in range(N_KV):
            lanes = slice(j * HEAD_DIM, (j + 1) * HEAD_DIM)
            k2 = k_scr[i * BLOCK:(i + 2) * BLOCK, lanes].astype(BF16)
            v2 = v_scr[i * BLOCK:(i + 2) * BLOCK, lanes].astype(BF16)
            qs = jnp.concatenate(
                [qb[:, (j * GROUP + g) * HEAD_DIM:(j * GROUP + g + 1) * HEAD_DIM] for g in range(GROUP)],
                axis=0).astype(BF16)
            sk = jnp.concatenate(
                [jnp.full((1, BLOCK), sinks_ref[j * GROUP + g], F32) for g in range(GROUP)], axis=1)
            st = _dot_nt(k2, qs) + bias
            m = jnp.maximum(jnp.max(st, axis=0, keepdims=True), sk)
            p = jnp.exp(st - m)
            den = jnp.sum(p, axis=0, keepdims=True) + jnp.exp(sk - m)
            yield
            ot = _dot_tn(v2, p.astype(BF16)) / den
            heads_t += [ot[:, g * BLOCK:(g + 1) * BLOCK] for g in range(GROUP)]
            yield
        yb_blocks.append(jnp.concatenate(heads_t, axis=0).T)
    yb = yb_blocks[0] if len(yb_blocks) == 1 else jnp.concatenate(yb_blocks, axis=0)

    u_scr[CONV_HDR - 2:CONV_HDR, :] = u_scr[CONV_HDR - 2 + rows:CONV_HDR + rows, :]
    k_scr[0:BLOCK, :] = k_scr[rows:rows + BLOCK, :]
    v_scr[0:BLOCK, :] = v_scr[rows:rows + BLOCK, :]

    mix = jnp.concatenate([ya, yb], axis=-1).astype(BF16)
    return x + _dot(mix, wout_ref[...])


def _pipeline_steps(s, x_ref, y_ref, x1_scr, mixer, mlp):
    slot = lax.rem(s, 2)

    def mix_tile():
        x1_scr[slot] = yield from mixer(x_ref[...], SEQ_TILE)

    def mlp_tile():
        y_ref[...] = yield from mlp(x1_scr[1 - slot])

    return mlp_tile, mix_tile


def _l0_prompt_kernel(sinks_ref, x_ref, meta_ref, gmix_ref, win_ref, segq_ref, segk_ref, qg_ref, kg_ref,
                      cw_ref, wout_ref, gmlp_ref, wup_ref, wdn_ref,
                      y_ref, ymeta_ref, conv_ref, kout_ref, vout_ref,
                      u_scr, k_scr, v_scr, first_scr, band_scr, x1_scr):
    s = pl.program_id(1)
    last = pl.num_programs(1) - 1
    mixer = functools.partial(
        _l0_mixer, sinks_ref=sinks_ref, gmix=gmix_ref[...], win_ref=win_ref, segq_ref=segq_ref,
        segk_ref=segk_ref, qg=qg_ref[...], kg=kg_ref[...], cw_ref=cw_ref, wout_ref=wout_ref,
        u_scr=u_scr, k_scr=k_scr, v_scr=v_scr, first_scr=first_scr, band_scr=band_scr)
    mlp = functools.partial(_mlp_steps, g=gmlp_ref[...], wup_ref=wup_ref, wdn_ref=wdn_ref)
    mlp_tile, mix_tile = _pipeline_steps(s, x_ref, y_ref, x1_scr, mixer, mlp)

    def band_bias(first_valid_key):
        c = lax.broadcasted_iota(jnp.int32, (2 * BLOCK, GROUP * BLOCK), 0)
        q = lax.broadcasted_iota(jnp.int32, (2 * BLOCK, GROUP * BLOCK), 1) & (BLOCK - 1)
        return jnp.where((c >= q) & (c <= q + WINDOW) & (c >= first_valid_key), 0.0, NEG)

    @pl.when(s == 0)
    def _start():
        band_scr[...] = band_bias(0)
        u_scr[0:CONV_HDR, :] = jnp.zeros((CONV_HDR, D_CONV), F32)
        k_scr[0:BLOCK, :] = jnp.zeros((BLOCK, KV_DIM), F32)
        v_scr[0:BLOCK, :] = jnp.zeros((BLOCK, KV_DIM), F32)
        first_scr[...] = band_bias(BLOCK + META_PAD)
        x1_meta, = _interleave(mixer(meta_ref[...], BLOCK))
        ymeta_ref[...], = _interleave(mlp(x1_meta))
        first_scr[...] = band_bias(META_PAD)
        _interleave(mix_tile())
        first_scr[...] = band_scr[...]

    @pl.when((s > 0) & (s < last))
    def _steady():
        _interleave(mlp_tile(), mix_tile())

    @pl.when(s == last)
    def _drain():
        _interleave(mlp_tile())

    @pl.when(s == last - 1)
    def _state():
        conv_ref[...] = u_scr[CONV_HDR - 2:CONV_HDR, :]
        kout_ref[...] = k_scr[0:BLOCK, :]
        vout_ref[...] = v_scr[0:BLOCK, :]


def _l1_mixer(x, rows, is_meta, gmix, wpool_ref, pscale, h_scr):
    h = _rms(x, gmix)
    h_scr[POOL_MAX:POOL_MAX + rows, :] = h
    ys = []
    for gi, w in enumerate(POOL_WINDOWS):
        cols = slice(gi * POOL_GROUP_DIM, (gi + 1) * POOL_GROUP_DIM)
        hg = h[:, cols]
        acc = hg
        for j in range(1, w):
            acc = acc + h_scr[POOL_MAX - j:POOL_MAX - j + rows, cols]
        if is_meta:
            r = lax.broadcasted_iota(jnp.int32, (rows, 1), 0)
            pooled = acc / jnp.clip(r - (META_PAD - 1), 1, w).astype(F32)
        else:
            pooled = acc * (1.0 / w)
        ys.append(_dot((pooled - hg).astype(BF16), wpool_ref[gi]))
        yield
    h_scr[0:POOL_MAX, :] = h_scr[rows:rows + POOL_MAX, :]
    return x + jnp.concatenate(ys, axis=-1) * pscale


def _l1_prompt_kernel(x_ref, meta_ref, gmix_ref, wpool_ref, pscale_ref, gmlp_ref, wup_ref, wdn_ref,
                      y_ref, pool_ref, h_scr, x1_scr):
    s = pl.program_id(1)
    last = pl.num_programs(1) - 1
    mixer = functools.partial(_l1_mixer, gmix=gmix_ref[...], wpool_ref=wpool_ref, pscale=pscale_ref[...],
                              h_scr=h_scr)
    mlp = functools.partial(_mlp_steps, g=gmlp_ref[...], wup_ref=wup_ref, wdn_ref=wdn_ref)
    mlp_tile, mix_tile = _pipeline_steps(s, x_ref, y_ref, x1_scr,
                                         functools.partial(mixer, is_meta=False), mlp)

    @pl.when(s == 0)
    def _start():
        h_scr[0:POOL_MAX, :] = jnp.zeros((POOL_MAX, D_MODEL), F32)
        _interleave(mixer(meta_ref[...], BLOCK, True))
        _interleave(mix_tile())

    @pl.when((s > 0) & (s < last))
    def _steady():
        _interleave(mlp_tile(), mix_tile())

    @pl.when(s == last)
    def _drain():
        _interleave(mlp_tile())

    @pl.when(s == last - 1)
    def _state():
        pool_ref[...] = h_scr[1:POOL_MAX, :]


def _s0_pre_kernel(x_ref, st_ref, gmix_ref, win_ref, segq_ref, segk_ref, qg_ref, kg_ref, cw_ref,
                   ya_ref, q_ref, k_ref, v_ref, conv_ref):
    n_seq = st_ref.shape[1]
    n_t = x_ref.shape[0] // n_seq
    u, gb, qn, kn, v = _in_proj(x_ref[...], gmix_ref[...], win_ref, segq_ref, segk_ref, qg_ref[...],
                                kg_ref[...])
    ue = [st_ref[0], st_ref[1]] + [u[t * n_seq:(t + 1) * n_seq, :] for t in range(n_t)]
    for t in range(n_t):
        y = ue[t] * cw_ref[0:1, :] + ue[t + 1] * cw_ref[1:2, :] + ue[t + 2] * cw_ref[2:3, :]
        ya_ref[t * n_seq:(t + 1) * n_seq, :] = gb[t * n_seq:(t + 1) * n_seq, :] * y
    q_ref[...] = qn
    k_ref[...] = kn
    v_ref[...] = v
    conv_ref[0] = ue[-2]
    conv_ref[1] = ue[-1]


def _s0_attn_kernel(q_ref, ck_ref, cv_ref, kn_ref, vn_ref, sk_ref, o_ref, kout_ref, vout_ref, ke_scr, ve_scr):
    bb, n_t = kn_ref.shape[0], kn_ref.shape[1]
    wb = ck_ref.shape[1]
    pad = SAMPLE_KEYS - wb - n_t
    for src, new, scr, out in ((ck_ref, kn_ref, ke_scr, kout_ref), (cv_ref, vn_ref, ve_scr, vout_ref)):
        scr[:, 0:wb, :] = src[...]
        scr[:, wb:wb + n_t, :] = new[...]
        scr[:, wb + n_t:, :] = jnp.zeros((bb, pad, KV_DIM), F32)
        out[...] = scr[:, n_t:n_t + wb, :]
    rows = n_t * GROUP
    t = lax.broadcasted_iota(jnp.int32, (rows, SAMPLE_KEYS), 0) // GROUP
    c = lax.broadcasted_iota(jnp.int32, (rows, SAMPLE_KEYS), 1)
    dist = wb + t - c
    bias = jnp.where((dist >= 0) & (dist <= WINDOW), 0.0, NEG)[None]
    for j in range(N_KV):
        lanes = slice(j * HEAD_DIM, (j + 1) * HEAD_DIM)
        q = q_ref[:, j].astype(BF16)
        k = ke_scr[:, :, lanes].astype(BF16)
        v = ve_scr[:, :, lanes].astype(BF16)
        sk = sk_ref[j][None]
        s = jnp.einsum('bqd,bkd->bqk', q, k, preferred_element_type=F32) + bias
        m = jnp.maximum(jnp.max(s, axis=-1, keepdims=True), sk)
        p = jnp.exp(s - m)
        den = jnp.sum(p, axis=-1, keepdims=True) + jnp.exp(sk - m)
        o = jnp.einsum('bqk,bkd->bqd', p.astype(BF16), v, preferred_element_type=F32)
        o_ref[:, j] = o / den


def _s0_post_kernel(x_ref, ya_ref, yb_ref, wout_ref, gmlp_ref, wup_ref, wdn_ref, y_ref):
    mix = jnp.concatenate([ya_ref[...], yb_ref[...]], axis=-1).astype(BF16)
    x1 = x_ref[...] + _dot(mix, wout_ref[...])
    y_ref[...] = _mlp(x1, gmlp_ref[...], wup_ref, wdn_ref)


def _s1_kernel(x_ref, buf_ref, gmix_ref, wpool_ref, pscale_ref, gmlp_ref, wup_ref, wdn_ref, y_ref, pool_ref):
    n_buf, n_seq = buf_ref.shape[0], buf_ref.shape[1]
    n_t = x_ref.shape[0] // n_seq
    x = x_ref[...]
    h = _rms(x, gmix_ref[...])
    he = [buf_ref[i] for i in range(n_buf)] + [h[t * n_seq:(t + 1) * n_seq, :] for t in range(n_t)]
    for i in range(n_buf):
        pool_ref[i] = he[n_t + i]
    ys = []
    for gi, w in enumerate(POOL_WINDOWS):
        cols = slice(gi * POOL_GROUP_DIM, (gi + 1) * POOL_GROUP_DIM)
        dl = []
        for t in range(n_t):
            acc = he[n_buf + t][:, cols]
            for j in range(1, w):
                acc = acc + he[n_buf + t - j][:, cols]
            dl.append(acc * (1.0 / w) - he[n_buf + t][:, cols])
        ys.append(_dot(jnp.concatenate(dl, axis=0).astype(BF16), wpool_ref[gi]))
    x1 = x + jnp.concatenate(ys, axis=-1) * pscale_ref[...]
    y_ref[...] = _mlp(x1, gmlp_ref[...], wup_ref, wdn_ref)


class _Stacked(NamedTuple):
    array: jax.Array
    layer: int


def _operand(arg):
    return arg.array if isinstance(arg, _Stacked) else arg


def _resident(arg, n_grid):
    if isinstance(arg, _Stacked):
        shape, idx = (None,) + arg.array.shape[1:], (arg.layer,) + (0,) * (arg.array.ndim - 1)
    else:
        shape, idx = arg.shape, (0,) * arg.ndim
    index_map = {1: lambda i: idx, 2: lambda i, j: idx}[n_grid]
    return pl.BlockSpec(shape, index_map, pipeline_mode=pl.Buffered(1))


def _params(n_grid):
    return pltpu.CompilerParams(dimension_semantics=("arbitrary",) * n_grid, vmem_limit_bytes=VMEM_LIMIT)


def _prompt_specs(n_tiles):
    tile_in = pl.BlockSpec((None, SEQ_TILE, D_MODEL), lambda b, s: (b, jnp.minimum(s, n_tiles - 1), 0))
    tile_out = pl.BlockSpec((None, SEQ_TILE, D_MODEL), lambda b, s: (b, jnp.maximum(s - 1, 0), 0))
    per_b = lambda r, c: pl.BlockSpec((None, r, c), lambda b, s: (b, 0, 0))
    return tile_in, tile_out, per_b


def _l0_prompt(x, sinks, *resident):
    n_b, seq, _ = x.shape
    n_tiles = seq // SEQ_TILE
    tile_in, tile_out, per_b = _prompt_specs(n_tiles)
    return pl.pallas_call(
        _l0_prompt_kernel,
        grid=(n_b, n_tiles + 1),
        in_specs=[pl.BlockSpec(memory_space=pltpu.SMEM), tile_in] + [_resident(a, 2) for a in resident],
        out_specs=[tile_out, per_b(BLOCK, D_MODEL), per_b(CONV_W - 1, D_CONV), per_b(WINDOW, KV_DIM),
                   per_b(WINDOW, KV_DIM)],
        out_shape=[jax.ShapeDtypeStruct((n_b, seq, D_MODEL), F32),
                   jax.ShapeDtypeStruct((n_b, BLOCK, D_MODEL), F32),
                   jax.ShapeDtypeStruct((n_b, CONV_W - 1, D_CONV), F32),
                   jax.ShapeDtypeStruct((n_b, WINDOW, KV_DIM), F32),
                   jax.ShapeDtypeStruct((n_b, WINDOW, KV_DIM), F32)],
        scratch_shapes=[pltpu.VMEM((CONV_HDR + SEQ_TILE, D_CONV), F32),
                        pltpu.VMEM((BLOCK + SEQ_TILE, KV_DIM), F32),
                        pltpu.VMEM((BLOCK + SEQ_TILE, KV_DIM), F32),
                        pltpu.VMEM((2 * BLOCK, GROUP * BLOCK), F32),
                        pltpu.VMEM((2 * BLOCK, GROUP * BLOCK), F32),
                        pltpu.VMEM((2, SEQ_TILE, D_MODEL), F32)],
        compiler_params=_params(2),
        name="l0_prompt",
    )(sinks, x, *[_operand(a) for a in resident])


def _l1_prompt(x, xmeta, *resident):
    n_b, seq, _ = x.shape
    n_tiles = seq // SEQ_TILE
    tile_in, tile_out, per_b = _prompt_specs(n_tiles)
    return pl.pallas_call(
        _l1_prompt_kernel,
        grid=(n_b, n_tiles + 1),
        in_specs=[tile_in, per_b(BLOCK, D_MODEL)] + [_resident(a, 2) for a in resident],
        out_specs=[tile_out, per_b(POOL_MAX - 1, D_MODEL)],
        out_shape=[jax.ShapeDtypeStruct((n_b, seq, D_MODEL), F32),
                   jax.ShapeDtypeStruct((n_b, POOL_MAX - 1, D_MODEL), F32)],
        scratch_shapes=[pltpu.VMEM((POOL_MAX + SEQ_TILE, D_MODEL), F32),
                        pltpu.VMEM((2, SEQ_TILE, D_MODEL), F32)],
        compiler_params=_params(2),
        name="l1_prompt",
    )(x, xmeta, *[_operand(a) for a in resident])


def _single_step(kernel, name, out_shape, *args):
    return pl.pallas_call(
        kernel,
        grid=(1,),
        in_specs=[_resident(a, 1) for a in args],
        out_specs=[pl.BlockSpec(o.shape, lambda i, n=len(o.shape): (0,) * n) for o in out_shape],
        out_shape=out_shape,
        compiler_params=_params(1),
        name=name,
    )(*[_operand(a) for a in args])


def _s0_attn(q4, ck, cv, kn, vn, sk_rows):
    n_seq, wb, _ = ck.shape
    n_t = kn.shape[1]
    rows = n_t * GROUP
    blk = lambda *tail: pl.BlockSpec((SAMPLE_BB,) + tail, lambda i: (i,) + (0,) * len(tail))
    return pl.pallas_call(
        _s0_attn_kernel,
        grid=(n_seq // SAMPLE_BB,),
        in_specs=[blk(N_KV, rows, HEAD_DIM), blk(wb, KV_DIM), blk(wb, KV_DIM), blk(n_t, KV_DIM),
                  blk(n_t, KV_DIM), pl.BlockSpec((N_KV, rows, 1), lambda i: (0, 0, 0))],
        out_specs=[blk(N_KV, rows, HEAD_DIM), blk(wb, KV_DIM), blk(wb, KV_DIM)],
        out_shape=[jax.ShapeDtypeStruct((n_seq, N_KV, rows, HEAD_DIM), F32),
                   jax.ShapeDtypeStruct((n_seq, wb, KV_DIM), F32),
                   jax.ShapeDtypeStruct((n_seq, wb, KV_DIM), F32)],
        scratch_shapes=[pltpu.VMEM((SAMPLE_BB, SAMPLE_KEYS, KV_DIM), F32),
                        pltpu.VMEM((SAMPLE_BB, SAMPLE_KEYS, KV_DIM), F32)],
        compiler_params=_params(1),
        name="s0_attn",
    )(q4, ck, cv, kn, vn, sk_rows)


def kernel(x_prompt, x_sample, state_conv, cache_k_win, cache_v_win, state_pool, meta_tokens, norm_mix, norm_mlp, w_in_even, conv_w, q_norm, k_norm, attn_sinks, w_out_even, w_pool, pool_scale, w_up, w_down):
    n_seq, n_t, _ = x_sample.shape
    wb = cache_k_win.shape[2]
    assert x_prompt.shape[1] % SEQ_TILE == 0 and x_prompt.shape[1] >= 2 * SEQ_TILE and n_seq % SAMPLE_BB == 0
    assert wb == WINDOW and wb + n_t <= SAMPLE_KEYS

    win = _Stacked(w_in_even.astype(BF16), 0)
    wout = _Stacked(w_out_even.astype(BF16), 0)
    wpool = _Stacked(w_pool.astype(BF16), 0)
    wup_all, wdn_all = w_up.astype(BF16), w_down.astype(BF16)
    wup = [_Stacked(wup_all, l) for l in range(2)]
    wdn = [_Stacked(wdn_all, l) for l in range(2)]
    gmix = [_Stacked(norm_mix[:, None, :], l) for l in range(2)]
    gmlp = [_Stacked(norm_mlp[:, None, :], l) for l in range(2)]
    qg = jnp.tile(q_norm[0], N_HEADS)[None, :]
    kg = jnp.tile(k_norm[0], N_KV)[None, :]
    segq = jnp.kron(jnp.eye(N_HEADS, dtype=F32), jnp.full((HEAD_DIM, HEAD_DIM), 1.0 / HEAD_DIM, F32)).astype(BF16)
    segk = segq[:KV_DIM, :KV_DIM]
    cw = _Stacked(conv_w, 0)
    sinks = attn_sinks[0]
    pscale = pool_scale
    meta_pad = jnp.pad(meta_tokens, ((META_PAD, 0), (0, 0)))

    x2, x2_meta, conv_p, k_p, v_p = _l0_prompt(x_prompt, sinks, meta_pad, gmix[0], win, segq, segk, qg, kg, cw,
                                               wout, gmlp[0], wup[0], wdn[0])
    y_prompt, pool_p = _l1_prompt(x2, x2_meta, gmix[1], wpool, pscale, gmlp[1], wup[1], wdn[1])

    n_rows = n_t * n_seq
    xs = jnp.transpose(x_sample, (1, 0, 2)).reshape(n_rows, D_MODEL)
    st_conv = jnp.transpose(state_conv[0], (1, 0, 2))
    sds = lambda *shape: jax.ShapeDtypeStruct(shape, F32)
    ya, qn, kn, vn, conv_s = _single_step(
        _s0_pre_kernel, "s0_pre",
        [sds(n_rows, D_CONV), sds(n_rows, Q_DIM), sds(n_rows, KV_DIM), sds(n_rows, KV_DIM),
         sds(CONV_W - 1, n_seq, D_CONV)],
        xs, st_conv, gmix[0], win, segq, segk, qg, kg, cw)
    q4 = qn.reshape(n_t, n_seq, N_KV, GROUP, HEAD_DIM).transpose(1, 2, 0, 3, 4)
    q4 = q4.reshape(n_seq, N_KV, n_t * GROUP, HEAD_DIM)
    kn_b = kn.reshape(n_t, n_seq, KV_DIM).transpose(1, 0, 2)
    vn_b = vn.reshape(n_t, n_seq, KV_DIM).transpose(1, 0, 2)
    sk_rows = jnp.tile(sinks.reshape(N_KV, 1, GROUP), (1, n_t, 1)).reshape(N_KV, n_t * GROUP, 1)
    o4, k_s, v_s = _s0_attn(q4, cache_k_win[0].reshape(n_seq, wb, KV_DIM),
                            cache_v_win[0].reshape(n_seq, wb, KV_DIM), kn_b, vn_b, sk_rows)
    yb = o4.reshape(n_seq, N_KV, n_t, GROUP, HEAD_DIM).transpose(2, 0, 1, 3, 4).reshape(n_rows, Q_DIM)
    xs2, = _single_step(_s0_post_kernel, "s0_post", [sds(n_rows, D_MODEL)],
                        xs, ya, yb, wout, gmlp[0], wup[0], wdn[0])
    buf_t = jnp.transpose(state_pool[0], (1, 0, 2))
    ys, pool_s = _single_step(_s1_kernel, "s1", [sds(n_rows, D_MODEL), sds(POOL_MAX - 1, n_seq, D_MODEL)],
                              xs2, buf_t, gmix[1], wpool, pscale, gmlp[1], wup[1], wdn[1])
    y_sample = ys.reshape(n_t, n_seq, D_MODEL).transpose(1, 0, 2)

    kv5 = lambda a: a.reshape(1, a.shape[0], WINDOW, N_KV, HEAD_DIM)
    return (y_prompt, y_sample,
            conv_p[None], jnp.transpose(conv_s, (1, 0, 2))[None],
            kv5(k_p), kv5(k_s), kv5(v_p), kv5(v_s),
            pool_p[None], jnp.transpose(pool_s, (1, 0, 2))[None])
```

```python
import functools
from typing import NamedTuple

import jax
import jax.numpy as jnp
from jax import lax
from jax.experimental import pallas as pl
from jax.experimental.pallas import tpu as pltpu

F32 = jnp.float32
BF16 = jnp.bfloat16

D_MODEL = 1024
D_CONV = 512
CONV_W = 3
HEAD_DIM = 64
N_HEADS = 8
N_KV = 2
GROUP = N_HEADS // N_KV
WINDOW = 128
BLOCK = 128
Q_DIM = N_HEADS * HEAD_DIM
KV_DIM = N_KV * HEAD_DIM
POOL_WINDOWS = (2, 4, 8, 16)
POOL_GROUP_DIM = D_MODEL // len(POOL_WINDOWS)
POOL_MAX = 16
D_FF = 4 * D_MODEL
D_IN_EVEN = 3 * D_CONV + Q_DIM + 2 * KV_DIM
N_META = 16
EPS = 1e-6
NEG = -1e30

META_PAD = BLOCK - N_META
SEQ_TILE = 512
FF_CHUNK = 1024
CONV_HDR = 8
SAMPLE_BB = 16
SAMPLE_KEYS = 144
VMEM_LIMIT = 56 * 1024 * 1024
MLP_YIELDS = 2 * (D_FF // FF_CHUNK)
L0_MIX_YIELDS = 1 + 2 * N_KV * (SEQ_TILE // BLOCK)


def _dot(a, b):
    return jnp.dot(a, b, preferred_element_type=F32)


def _dot_nt(a, b):
    return lax.dot_general(a, b, (((1,), (1,)), ((), ())), preferred_element_type=F32)


def _dot_tn(a, b):
    return lax.dot_general(a, b, (((0,), (0,)), ((), ())), preferred_element_type=F32)


def _rms(x, g):
    ms = jnp.mean(x * x, axis=-1, keepdims=True)
    return x * lax.rsqrt(ms + EPS) * g


def _head_rms(x, seg, g):
    ms = _dot((x * x).astype(BF16), seg)
    return x * lax.rsqrt(ms + EPS) * g


def _interleave(*steppers, shares=None):
    shares = shares or [1] * len(steppers)
    results = [None] * len(steppers)
    done = [0] * len(steppers)
    live = set(range(len(steppers)))
    while live:
        i = min(live, key=lambda k: ((done[k] + 1) / shares[k], k))
        try:
            next(steppers[i])
            done[i] += 1
        except StopIteration as finished:
            results[i] = finished.value
            live.discard(i)
    return results


def _mlp_steps(x, g, wup_ref, wdn_ref):
    xn = _rms(x, g).astype(BF16)
    acc = x
    for c in range(D_FF // FF_CHUNK):
        cols = slice(c * FF_CHUNK, (c + 1) * FF_CHUNK)
        h = _dot(xn, wup_ref[:, cols])
        a = jnp.square(jnp.maximum(h, 0.0)).astype(BF16)
        yield
        acc = acc + _dot(a, wdn_ref[cols, :])
        yield
    return acc


def _mlp(x, g, wup_ref, wdn_ref):
    return _interleave(_mlp_steps(x, g, wup_ref, wdn_ref))[0]


def _in_proj(x, g, win_ref, segq_ref, segk_ref, qg, kg):
    hn = _rms(x, g).astype(BF16)
    kv_col = 3 * D_CONV + Q_DIM
    half = x.shape[0] // 2
    z = _dot(hn, win_ref[:, 0:kv_col])
    kv = jnp.concatenate([_dot(hn[0:half], win_ref[:, kv_col:]), _dot(hn[half:], win_ref[:, kv_col:])], axis=0)
    xa = z[:, 0:D_CONV]
    gc = z[:, D_CONV:2 * D_CONV]
    gb = z[:, 2 * D_CONV:3 * D_CONV]
    q = z[:, 3 * D_CONV:]
    k = kv[:, 0:KV_DIM]
    v = kv[:, KV_DIM:]
    qn = _head_rms(q, segq_ref[...], qg) * (HEAD_DIM ** -0.5)
    kn = _head_rms(k, segk_ref[...], kg)
    return gc * xa, gb, qn, kn, v


def _l0_mixer(x, rows, sinks_ref, gmix, win_ref, segq_ref, segk_ref, qg, kg, cw_ref, wout_ref,
              u_scr, k_scr, v_scr, first_scr, band_scr):
    u, gb, qn, kn, v = _in_proj(x, gmix, win_ref, segq_ref, segk_ref, qg, kg)

    u_scr[CONV_HDR:CONV_HDR + rows, :] = u
    u1 = u_scr[CONV_HDR - 1:CONV_HDR - 1 + rows, :]
    u2 = u_scr[CONV_HDR - 2:CONV_HDR - 2 + rows, :]
    ya = gb * (u2 * cw_ref[0:1, :] + u1 * cw_ref[1:2, :] + u * cw_ref[2:3, :])
    k_scr[BLOCK:BLOCK + rows, :] = kn
    v_scr[BLOCK:BLOCK + rows, :] = v
    yield

    yb_blocks = []
    for i in range(rows // BLOCK):
        qb = qn[i * BLOCK:(i + 1) * BLOCK, :]
        bias = first_scr[...] if i == 0 else band_scr[...]
        heads_t = []
        for j in range(N_KV):
            lanes = slice(j * HEAD_DIM, (j + 1) * HEAD_DIM)
            k2 = k_scr[i * BLOCK:(i + 2) * BLOCK, lanes].astype(BF16)
            v2 = v_scr[i * BLOCK:(i + 2) * BLOCK, lanes].astype(BF16)
            qs = jnp.concatenate(
                [qb[:, (j * GROUP + g) * HEAD_DIM:(j * GROUP + g + 1) * HEAD_DIM] for g in range(GROUP)],
                axis=0).astype(BF16)
            sk = jnp.concatenate(
                [jnp.full((1, BLOCK), sinks_ref[j * GROUP + g], F32) for g in range(GROUP)], axis=1)
            st = _dot_nt(k2, qs) + bias
            m = jnp.maximum(jnp.max(st, axis=0, keepdims=True), sk)
            p = jnp.exp(st - m)
            den = jnp.sum(p, axis=0, keepdims=True) + jnp.exp(sk - m)
            yield
            ot = _dot_tn(v2, p.astype(BF16)) / den
            heads_t += [ot[:, g * BLOCK:(g + 1) * BLOCK] for g in range(GROUP)]
            yield
        yb_blocks.append(jnp.concatenate(heads_t, axis=0).T)
    yb = yb_blocks[0] if len(yb_blocks) == 1 else jnp.concatenate(yb_blocks, axis=0)

    u_scr[CONV_HDR - 2:CONV_HDR, :] = u_scr[CONV_HDR - 2 + rows:CONV_HDR + rows, :]
    k_scr[0:BLOCK, :] = k_scr[rows:rows + BLOCK, :]
    v_scr[0:BLOCK, :] = v_scr[rows:rows + BLOCK, :]

    mix = jnp.concatenate([ya, yb], axis=-1).astype(BF16)
    return x + _dot(mix, wout_ref[...])


def _pipeline_steps(s, x_ref, y_ref, x1_scr, mixer, mlp):
    slot = lax.rem(s, 2)

    def mix_tile():
        x1_scr[slot] = yield from mixer(x_ref[...], SEQ_TILE)

    def mlp_tile():
        y_ref[...] = yield from mlp(x1_scr[1 - slot])

    return mlp_tile, mix_tile


def _l0_prompt_kernel(sinks_ref, x_ref, meta_ref, gmix_ref, win_ref, segq_ref, segk_ref, qg_ref, kg_ref,
                      cw_ref, wout_ref, gmlp_ref, wup_ref, wdn_ref,
                      y_ref, ymeta_ref, conv_ref, kout_ref, vout_ref,
                      u_scr, k_scr, v_scr, first_scr, band_scr, x1_scr):
    s = pl.program_id(1)
    last = pl.num_programs(1) - 1
    mixer = functools.partial(
        _l0_mixer, sinks_ref=sinks_ref, gmix=gmix_ref[...], win_ref=win_ref, segq_ref=segq_ref,
        segk_ref=segk_ref, qg=qg_ref[...], kg=kg_ref[...], cw_ref=cw_ref, wout_ref=wout_ref,
        u_scr=u_scr, k_scr=k_scr, v_scr=v_scr, first_scr=first_scr, band_scr=band_scr)
    mlp = functools.partial(_mlp_steps, g=gmlp_ref[...], wup_ref=wup_ref, wdn_ref=wdn_ref)
    mlp_tile, mix_tile = _pipeline_steps(s, x_ref, y_ref, x1_scr, mixer, mlp)

    def band_bias(first_valid_key):
        c = lax.broadcasted_iota(jnp.int32, (2 * BLOCK, GROUP * BLOCK), 0)
        q = lax.broadcasted_iota(jnp.int32, (2 * BLOCK, GROUP * BLOCK), 1) & (BLOCK - 1)
        return jnp.where((c >= q) & (c <= q + WINDOW) & (c >= first_valid_key), 0.0, NEG)

    @pl.when(s == 0)
    def _start():
        band_scr[...] = band_bias(0)
        u_scr[0:CONV_HDR, :] = jnp.zeros((CONV_HDR, D_CONV), F32)
        k_scr[0:BLOCK, :] = jnp.zeros((BLOCK, KV_DIM), F32)
        v_scr[0:BLOCK, :] = jnp.zeros((BLOCK, KV_DIM), F32)
        first_scr[...] = band_bias(BLOCK + META_PAD)
        x1_meta, = _interleave(mixer(meta_ref[...], BLOCK))
        ymeta_ref[...], = _interleave(mlp(x1_meta))
        first_scr[...] = band_bias(META_PAD)
        _interleave(mix_tile())
        first_scr[...] = band_scr[...]

    @pl.when((s > 0) & (s < last))
    def _steady():
        _interleave(mlp_tile(), mix_tile(), shares=(MLP_YIELDS, L0_MIX_YIELDS))

    @pl.when(s == last)
    def _drain():
        _interleave(mlp_tile())

    @pl.when(s == last - 1)
    def _state():
        conv_ref[...] = u_scr[CONV_HDR - 2:CONV_HDR, :]
        kout_ref[...] = k_scr[0:BLOCK, :]
        vout_ref[...] = v_scr[0:BLOCK, :]


def _l1_mixer(x, rows, is_meta, gmix, wpool_ref, pscale, h_scr):
    h = _rms(x, gmix)
    h_scr[POOL_MAX:POOL_MAX + rows, :] = h
    ys = []
    for gi, w in enumerate(POOL_WINDOWS):
        cols = slice(gi * POOL_GROUP_DIM, (gi + 1) * POOL_GROUP_DIM)
        hg = h[:, cols]
        acc = hg
        for j in range(1, w):
            acc = acc + h_scr[POOL_MAX - j:POOL_MAX - j + rows, cols]
        if is_meta:
            r = lax.broadcasted_iota(jnp.int32, (rows, 1), 0)
            pooled = acc / jnp.clip(r - (META_PAD - 1), 1, w).astype(F32)
        else:
            pooled = acc * (1.0 / w)
        ys.append(_dot((pooled - hg).astype(BF16), wpool_ref[gi]))
        yield
    h_scr[0:POOL_MAX, :] = h_scr[rows:rows + POOL_MAX, :]
    return x + jnp.concatenate(ys, axis=-1) * pscale


def _l1_prompt_kernel(x_ref, meta_ref, gmix_ref, wpool_ref, pscale_ref, gmlp_ref, wup_ref, wdn_ref,
                      y_ref, pool_ref, h_scr, x1_scr):
    s = pl.program_id(1)
    last = pl.num_programs(1) - 1
    mixer = functools.partial(_l1_mixer, gmix=gmix_ref[...], wpool_ref=wpool_ref, pscale=pscale_ref[...],
                              h_scr=h_scr)
    mlp = functools.partial(_mlp_steps, g=gmlp_ref[...], wup_ref=wup_ref, wdn_ref=wdn_ref)
    mlp_tile, mix_tile = _pipeline_steps(s, x_ref, y_ref, x1_scr,
                                         functools.partial(mixer, is_meta=False), mlp)

    @pl.when(s == 0)
    def _start():
        h_scr[0:POOL_MAX, :] = jnp.zeros((POOL_MAX, D_MODEL), F32)
        _interleave(mixer(meta_ref[...], BLOCK, True))
        _interleave(mix_tile())

    @pl.when((s > 0) & (s < last))
    def _steady():
        _interleave(mlp_tile(), mix_tile(), shares=(MLP_YIELDS, len(POOL_WINDOWS)))

    @pl.when(s == last)
    def _drain():
        _interleave(mlp_tile())

    @pl.when(s == last - 1)
    def _state():
        pool_ref[...] = h_scr[1:POOL_MAX, :]


def _s0_pre_kernel(x_ref, st_ref, gmix_ref, win_ref, segq_ref, segk_ref, qg_ref, kg_ref, cw_ref,
                   ya_ref, q_ref, k_ref, v_ref, conv_ref):
    n_seq = st_ref.shape[1]
    n_t = x_ref.shape[0] // n_seq
    u, gb, qn, kn, v = _in_proj(x_ref[...], gmix_ref[...], win_ref, segq_ref, segk_ref, qg_ref[...],
                                kg_ref[...])
    ue = [st_ref[0], st_ref[1]] + [u[t * n_seq:(t + 1) * n_seq, :] for t in range(n_t)]
    for t in range(n_t):
        y = ue[t] * cw_ref[0:1, :] + ue[t + 1] * cw_ref[1:2, :] + ue[t + 2] * cw_ref[2:3, :]
        ya_ref[t * n_seq:(t + 1) * n_seq, :] = gb[t * n_seq:(t + 1) * n_seq, :] * y
    q_ref[...] = qn
    k_ref[...] = kn
    v_ref[...] = v
    conv_ref[0] = ue[-2]
    conv_ref[1] = ue[-1]


def _s0_attn_kernel(q_ref, ck_ref, cv_ref, kn_ref, vn_ref, sk_ref, o_ref, kout_ref, vout_ref, ke_scr, ve_scr):
    bb, n_t = kn_ref.shape[0], kn_ref.shape[1]
    wb = ck_ref.shape[1]
    pad = SAMPLE_KEYS - wb - n_t
    for src, new, scr, out in ((ck_ref, kn_ref, ke_scr, kout_ref), (cv_ref, vn_ref, ve_scr, vout_ref)):
        scr[:, 0:wb, :] = src[...]
        scr[:, wb:wb + n_t, :] = new[...]
        scr[:, wb + n_t:, :] = jnp.zeros((bb, pad, KV_DIM), F32)
        out[...] = scr[:, n_t:n_t + wb, :]
    rows = n_t * GROUP
    t = lax.broadcasted_iota(jnp.int32, (rows, SAMPLE_KEYS), 0) // GROUP
    c = lax.broadcasted_iota(jnp.int32, (rows, SAMPLE_KEYS), 1)
    dist = wb + t - c
    bias = jnp.where((dist >= 0) & (dist <= WINDOW), 0.0, NEG)[None]
    for j in range(N_KV):
        lanes = slice(j * HEAD_DIM, (j + 1) * HEAD_DIM)
        q = q_ref[:, j].astype(BF16)
        k = ke_scr[:, :, lanes].astype(BF16)
        v = ve_scr[:, :, lanes].astype(BF16)
        sk = sk_ref[j][None]
        s = jnp.einsum('bqd,bkd->bqk', q, k, preferred_element_type=F32) + bias
        m = jnp.maximum(jnp.max(s, axis=-1, keepdims=True), sk)
        p = jnp.exp(s - m)
        den = jnp.sum(p, axis=-1, keepdims=True) + jnp.exp(sk - m)
        o = jnp.einsum('bqk,bkd->bqd', p.astype(BF16), v, preferred_element_type=F32)
        o_ref[:, j] = o / den


def _s0_post_kernel(x_ref, ya_ref, yb_ref, wout_ref, gmlp_ref, wup_ref, wdn_ref, y_ref):
    mix = jnp.concatenate([ya_ref[...], yb_ref[...]], axis=-1).astype(BF16)
    x1 = x_ref[...] + _dot(mix, wout_ref[...])
    y_ref[...] = _mlp(x1, gmlp_ref[...], wup_ref, wdn_ref)


def _s1_kernel(x_ref, buf_ref, gmix_ref, wpool_ref, pscale_ref, gmlp_ref, wup_ref, wdn_ref, y_ref, pool_ref):
    n_buf, n_seq = buf_ref.shape[0], buf_ref.shape[1]
    n_t = x_ref.shape[0] // n_seq
    x = x_ref[...]
    h = _rms(x, gmix_ref[...])
    he = [buf_ref[i] for i in range(n_buf)] + [h[t * n_seq:(t + 1) * n_seq, :] for t in range(n_t)]
    for i in range(n_buf):
        pool_ref[i] = he[n_t + i]
    ys = []
    for gi, w in enumerate(POOL_WINDOWS):
        cols = slice(gi * POOL_GROUP_DIM, (gi + 1) * POOL_GROUP_DIM)
        dl = []
        for t in range(n_t):
            acc = he[n_buf + t][:, cols]
            for j in range(1, w):
                acc = acc + he[n_buf + t - j][:, cols]
            dl.append(acc * (1.0 / w) - he[n_buf + t][:, cols])
        ys.append(_dot(jnp.concatenate(dl, axis=0).astype(BF16), wpool_ref[gi]))
    x1 = x + jnp.concatenate(ys, axis=-1) * pscale_ref[...]
    y_ref[...] = _mlp(x1, gmlp_ref[...], wup_ref, wdn_ref)


class _Stacked(NamedTuple):
    array: jax.Array
    layer: int


def _operand(arg):
    return arg.array if isinstance(arg, _Stacked) else arg


def _resident(arg, n_grid):
    if isinstance(arg, _Stacked):
        shape, idx = (None,) + arg.array.shape[1:], (arg.layer,) + (0,) * (arg.array.ndim - 1)
    else:
        shape, idx = arg.shape, (0,) * arg.ndim
    index_map = {1: lambda i: idx, 2: lambda i, j: idx}[n_grid]
    return pl.BlockSpec(shape, index_map, pipeline_mode=pl.Buffered(1))


def _params(n_grid):
    return pltpu.CompilerParams(dimension_semantics=("arbitrary",) * n_grid, vmem_limit_bytes=VMEM_LIMIT)


def _prompt_specs(n_tiles):
    tile_in = pl.BlockSpec((None, SEQ_TILE, D_MODEL), lambda b, s: (b, jnp.minimum(s, n_tiles - 1), 0))
    tile_out = pl.BlockSpec((None, SEQ_TILE, D_MODEL), lambda b, s: (b, jnp.maximum(s - 1, 0), 0))
    per_b = lambda r, c: pl.BlockSpec((None, r, c), lambda b, s: (b, 0, 0))
    return tile_in, tile_out, per_b


def _l0_prompt(x, sinks, *resident):
    n_b, seq, _ = x.shape
    n_tiles = seq // SEQ_TILE
    tile_in, tile_out, per_b = _prompt_specs(n_tiles)
    return pl.pallas_call(
        _l0_prompt_kernel,
        grid=(n_b, n_tiles + 1),
        in_specs=[pl.BlockSpec(memory_space=pltpu.SMEM), tile_in] + [_resident(a, 2) for a in resident],
        out_specs=[tile_out, per_b(BLOCK, D_MODEL), per_b(CONV_W - 1, D_CONV), per_b(WINDOW, KV_DIM),
                   per_b(WINDOW, KV_DIM)],
        out_shape=[jax.ShapeDtypeStruct((n_b, seq, D_MODEL), F32),
                   jax.ShapeDtypeStruct((n_b, BLOCK, D_MODEL), F32),
                   jax.ShapeDtypeStruct((n_b, CONV_W - 1, D_CONV), F32),
                   jax.ShapeDtypeStruct((n_b, WINDOW, KV_DIM), F32),
                   jax.ShapeDtypeStruct((n_b, WINDOW, KV_DIM), F32)],
        scratch_shapes=[pltpu.VMEM((CONV_HDR + SEQ_TILE, D_CONV), F32),
                        pltpu.VMEM((BLOCK + SEQ_TILE, KV_DIM), F32),
                        pltpu.VMEM((BLOCK + SEQ_TILE, KV_DIM), F32),
                        pltpu.VMEM((2 * BLOCK, GROUP * BLOCK), F32),
                        pltpu.VMEM((2 * BLOCK, GROUP * BLOCK), F32),
                        pltpu.VMEM((2, SEQ_TILE, D_MODEL), F32)],
        compiler_params=_params(2),
        name="l0_prompt",
    )(sinks, x, *[_operand(a) for a in resident])


def _l1_prompt(x, xmeta, *resident):
    n_b, seq, _ = x.shape
    n_tiles = seq // SEQ_TILE
    tile_in, tile_out, per_b = _prompt_specs(n_tiles)
    return pl.pallas_call(
        _l1_prompt_kernel,
        grid=(n_b, n_tiles + 1),
        in_specs=[tile_in, per_b(BLOCK, D_MODEL)] + [_resident(a, 2) for a in resident],
        out_specs=[tile_out, per_b(POOL_MAX - 1, D_MODEL)],
        out_shape=[jax.ShapeDtypeStruct((n_b, seq, D_MODEL), F32),
                   jax.ShapeDtypeStruct((n_b, POOL_MAX - 1, D_MODEL), F32)],
        scratch_shapes=[pltpu.VMEM((POOL_MAX + SEQ_TILE, D_MODEL), F32),
                        pltpu.VMEM((2, SEQ_TILE, D_MODEL), F32)],
        compiler_params=_params(2),
        name="l1_prompt",
    )(x, xmeta, *[_operand(a) for a in resident])


def _single_step(kernel, name, out_shape, *args):
    return pl.pallas_call(
        kernel,
        grid=(1,),
        in_specs=[_resident(a, 1) for a in args],
        out_specs=[pl.BlockSpec(o.shape, lambda i, n=len(o.shape): (0,) * n) for o in out_shape],
        out_shape=out_shape,
        compiler_params=_params(1),
        name=name,
    )(*[_operand(a) for a in args])


def _s0_attn(q4, ck, cv, kn, vn, sk_rows):
    n_seq, wb, _ = ck.shape
    n_t = kn.shape[1]
    rows = n_t * GROUP
    blk = lambda *tail: pl.BlockSpec((SAMPLE_BB,) + tail, lambda i: (i,) + (0,) * len(tail))
    return pl.pallas_call(
        _s0_attn_kernel,
        grid=(n_seq // SAMPLE_BB,),
        in_specs=[blk(N_KV, rows, HEAD_DIM), blk(wb, KV_DIM), blk(wb, KV_DIM), blk(n_t, KV_DIM),
                  blk(n_t, KV_DIM), pl.BlockSpec((N_KV, rows, 1), lambda i: (0, 0, 0))],
        out_specs=[blk(N_KV, rows, HEAD_DIM), blk(wb, KV_DIM), blk(wb, KV_DIM)],
        out_shape=[jax.ShapeDtypeStruct((n_seq, N_KV, rows, HEAD_DIM), F32),
                   jax.ShapeDtypeStruct((n_seq, wb, KV_DIM), F32),
                   jax.ShapeDtypeStruct((n_seq, wb, KV_DIM), F32)],
        scratch_shapes=[pltpu.VMEM((SAMPLE_BB, SAMPLE_KEYS, KV_DIM), F32),
                        pltpu.VMEM((SAMPLE_BB, SAMPLE_KEYS, KV_DIM), F32)],
        compiler_params=_params(1),
        name="s0_attn",
    )(q4, ck, cv, kn, vn, sk_rows)


def kernel(x_prompt, x_sample, state_conv, cache_k_win, cache_v_win, state_pool, meta_tokens, norm_mix, norm_mlp, w_in_even, conv_w, q_norm, k_norm, attn_sinks, w_out_even, w_pool, pool_scale, w_up, w_down):
    n_seq, n_t, _ = x_sample.shape
    wb = cache_k_win.shape[2]
    assert x_prompt.shape[1] % SEQ_TILE == 0 and x_prompt.shape[1] >= 2 * SEQ_TILE and n_seq % SAMPLE_BB == 0
    assert wb == WINDOW and wb + n_t <= SAMPLE_KEYS

    win = _Stacked(w_in_even.astype(BF16), 0)
    wout = _Stacked(w_out_even.astype(BF16), 0)
    wpool = _Stacked(w_pool.astype(BF16), 0)
    wup_all, wdn_all = w_up.astype(BF16), w_down.astype(BF16)
    wup = [_Stacked(wup_all, l) for l in range(2)]
    wdn = [_Stacked(wdn_all, l) for l in range(2)]
    gmix = [_Stacked(norm_mix[:, None, :], l) for l in range(2)]
    gmlp = [_Stacked(norm_mlp[:, None, :], l) for l in range(2)]
    qg = jnp.tile(q_norm[0], N_HEADS)[None, :]
    kg = jnp.tile(k_norm[0], N_KV)[None, :]
    segq = jnp.kron(jnp.eye(N_HEADS, dtype=F32), jnp.full((HEAD_DIM, HEAD_DIM), 1.0 / HEAD_DIM, F32)).astype(BF16)
    segk = segq[:KV_DIM, :KV_DIM]
    cw = _Stacked(conv_w, 0)
    sinks = attn_sinks[0]
    pscale = pool_scale
    meta_pad = jnp.pad(meta_tokens, ((META_PAD, 0), (0, 0)))

    x2, x2_meta, conv_p, k_p, v_p = _l0_prompt(x_prompt, sinks, meta_pad, gmix[0], win, segq, segk, qg, kg, cw,
                                               wout, gmlp[0], wup[0], wdn[0])
    y_prompt, pool_p = _l1_prompt(x2, x2_meta, gmix[1], wpool, pscale, gmlp[1], wup[1], wdn[1])

    n_rows = n_t * n_seq
    xs = jnp.transpose(x_sample, (1, 0, 2)).reshape(n_rows, D_MODEL)
    st_conv = jnp.transpose(state_conv[0], (1, 0, 2))
    sds = lambda *shape: jax.ShapeDtypeStruct(shape, F32)
    ya, qn, kn, vn, conv_s = _single_step(
        _s0_pre_kernel, "s0_pre",
        [sds(n_rows, D_CONV), sds(n_rows, Q_DIM), sds(n_rows, KV_DIM), sds(n_rows, KV_DIM),
         sds(CONV_W - 1, n_seq, D_CONV)],
        xs, st_conv, gmix[0], win, segq, segk, qg, kg, cw)
    q4 = qn.reshape(n_t, n_seq, N_KV, GROUP, HEAD_DIM).transpose(1, 2, 0, 3, 4)
    q4 = q4.reshape(n_seq, N_KV, n_t * GROUP, HEAD_DIM)
    kn_b = kn.reshape(n_t, n_seq, KV_DIM).transpose(1, 0, 2)
    vn_b = vn.reshape(n_t, n_seq, KV_DIM).transpose(1, 0, 2)
    sk_rows = jnp.tile(sinks.reshape(N_KV, 1, GROUP), (1, n_t, 1)).reshape(N_KV, n_t * GROUP, 1)
    o4, k_s, v_s = _s0_attn(q4, cache_k_win[0].reshape(n_seq, wb, KV_DIM),
                            cache_v_win[0].reshape(n_seq, wb, KV_DIM), kn_b, vn_b, sk_rows)
    yb = o4.reshape(n_seq, N_KV, n_t, GROUP, HEAD_DIM).transpose(2, 0, 1, 3, 4).reshape(n_rows, Q_DIM)
    xs2, = _single_step(_s0_post_kernel, "s0_post", [sds(n_rows, D_MODEL)],
                        xs, ya, yb, wout, gmlp[0], wup[0], wdn[0])
    buf_t = jnp.transpose(state_pool[0], (1, 0, 2))
    ys, pool_s = _single_step(_s1_kernel, "s1", [sds(n_rows, D_MODEL), sds(POOL_MAX - 1, n_seq, D_MODEL)],
                              xs2, buf_t, gmix[1], wpool, pscale, gmlp[1], wup[1], wdn[1])
    y_sample = ys.reshape(n_t, n_seq, D_MODEL).transpose(1, 0, 2)

    kv5 = lambda a: a.reshape(1, a.shape[0], WINDOW, N_KV, HEAD_DIM)
    return (y_prompt, y_sample,
            conv_p[None], jnp.transpose(conv_s, (1, 0, 2))[None],
            kv5(k_p), kv5(k_s), kv5(v_p), kv5(v_s),
            pool_p[None], jnp.transpose(pool_s, (1, 0, 2))[None])
```

```python
import functools
from typing import NamedTuple

import jax
import jax.numpy as jnp
from jax import lax
from jax.experimental import pallas as pl
from jax.experimental.pallas import tpu as pltpu

F32 = jnp.float32
BF16 = jnp.bfloat16

D_MODEL = 1024
D_CONV = 512
CONV_W = 3
HEAD_DIM = 64
N_HEADS = 8
N_KV = 2
GROUP = N_HEADS // N_KV
WINDOW = 128
BLOCK = 128
Q_DIM = N_HEADS * HEAD_DIM
KV_DIM = N_KV * HEAD_DIM
POOL_WINDOWS = (2, 4, 8, 16)
POOL_GROUP_DIM = D_MODEL // len(POOL_WINDOWS)
POOL_MAX = 16
D_FF = 4 * D_MODEL
D_IN_EVEN = 3 * D_CONV + Q_DIM + 2 * KV_DIM
N_META = 16
EPS = 1e-6
NEG = -1e30

META_PAD = BLOCK - N_META
SEQ_TILE = 512
FF_CHUNK = 1024
FF_STREAM = 512
CONV_HDR = 8
SAMPLE_BB = 16
SAMPLE_KEYS = 144
VMEM_LIMIT = 56 * 1024 * 1024
MLP_YIELDS = 2 * (D_FF // FF_CHUNK)
L0_MIX_YIELDS = 1 + 2 * N_KV * (SEQ_TILE // BLOCK)


def _dot(a, b):
    return jnp.dot(a, b, preferred_element_type=F32)


def _dot_nt(a, b):
    return lax.dot_general(a, b, (((1,), (1,)), ((), ())), preferred_element_type=F32)


def _dot_tn(a, b):
    return lax.dot_general(a, b, (((0,), (0,)), ((), ())), preferred_element_type=F32)


def _rms(x, g):
    ms = jnp.mean(x * x, axis=-1, keepdims=True)
    return x * lax.rsqrt(ms + EPS) * g


def _head_rms(x, seg, g):
    ms = _dot((x * x).astype(BF16), seg)
    return x * lax.rsqrt(ms + EPS) * g


def _interleave(*steppers, shares=None):
    shares = shares or [1] * len(steppers)
    results = [None] * len(steppers)
    done = [0] * len(steppers)
    live = set(range(len(steppers)))
    while live:
        i = min(live, key=lambda k: ((done[k] + 1) / shares[k], k))
        try:
            next(steppers[i])
            done[i] += 1
        except StopIteration as finished:
            results[i] = finished.value
            live.discard(i)
    return results


def _mlp_steps(x, g, wup_ref, wdn_ref):
    xn = _rms(x, g).astype(BF16)
    acc = x
    for c in range(D_FF // FF_CHUNK):
        cols = slice(c * FF_CHUNK, (c + 1) * FF_CHUNK)
        h = _dot(xn, wup_ref[:, cols])
        a = jnp.square(jnp.maximum(h, 0.0)).astype(BF16)
        yield
        acc = acc + _dot(a, wdn_ref[cols, :])
        yield
    return acc


def _mlp(x, g, wup_ref, wdn_ref):
    return _interleave(_mlp_steps(x, g, wup_ref, wdn_ref))[0]


def _in_proj(x, g, win_ref, segq_ref, segk_ref, qg, kg):
    hn = _rms(x, g).astype(BF16)
    kv_col = 3 * D_CONV + Q_DIM
    half = x.shape[0] // 2
    z = _dot(hn, win_ref[:, 0:kv_col])
    kv = jnp.concatenate([_dot(hn[0:half], win_ref[:, kv_col:]), _dot(hn[half:], win_ref[:, kv_col:])], axis=0)
    xa = z[:, 0:D_CONV]
    gc = z[:, D_CONV:2 * D_CONV]
    gb = z[:, 2 * D_CONV:3 * D_CONV]
    q = z[:, 3 * D_CONV:]
    k = kv[:, 0:KV_DIM]
    v = kv[:, KV_DIM:]
    qn = _head_rms(q, segq_ref[...], qg) * (HEAD_DIM ** -0.5)
    kn = _head_rms(k, segk_ref[...], kg)
    return gc * xa, gb, qn, kn, v


def _l0_mixer(x, rows, sinks_ref, gmix, win_ref, segq_ref, segk_ref, qg, kg, cw_ref, wout_ref,
              u_scr, k_scr, v_scr, first_scr, band_scr):
    u, gb, qn, kn, v = _in_proj(x, gmix, win_ref, segq_ref, segk_ref, qg, kg)

    u_scr[CONV_HDR:CONV_HDR + rows, :] = u
    u1 = u_scr[CONV_HDR - 1:CONV_HDR - 1 + rows, :]
    u2 = u_scr[CONV_HDR - 2:CONV_HDR - 2 + rows, :]
    ya = gb * (u2 * cw_ref[0:1, :] + u1 * cw_ref[1:2, :] + u * cw_ref[2:3, :])
    k_scr[BLOCK:BLOCK + rows, :] = kn
    v_scr[BLOCK:BLOCK + rows, :] = v
    yield

    yb_blocks = []
    for i in range(rows // BLOCK):
        qb = qn[i * BLOCK:(i + 1) * BLOCK, :]
        bias = first_scr[...] if i == 0 else band_scr[...]
        heads_t = []
        for j in range(N_KV):
            lanes = slice(j * HEAD_DIM, (j + 1) * HEAD_DIM)
            k2 = k_scr[i * BLOCK:(i + 2) * BLOCK, lanes].astype(BF16)
            v2 = v_scr[i * BLOCK:(i + 2) * BLOCK, lanes].astype(BF16)
            qs = jnp.concatenate(
                [qb[:, (j * GROUP + g) * HEAD_DIM:(j * GROUP + g + 1) * HEAD_DIM] for g in range(GROUP)],
                axis=0).astype(BF16)
            sk = jnp.concatenate(
                [jnp.full((1, BLOCK), sinks_ref[j * GROUP + g], F32) for g in range(GROUP)], axis=1)
            st = _dot_nt(k2, qs) + bias
            m = jnp.maximum(jnp.max(st, axis=0, keepdims=True), sk)
            p = jnp.exp(st - m)
            den = jnp.sum(p, axis=0, keepdims=True) + jnp.exp(sk - m)
            yield
            ot = _dot_tn(v2, p.astype(BF16)) / den
            heads_t += [ot[:, g * BLOCK:(g + 1) * BLOCK] for g in range(GROUP)]
            yield
        yb_blocks.append(jnp.concatenate(heads_t, axis=0).T)
    yb = yb_blocks[0] if len(yb_blocks) == 1 else jnp.concatenate(yb_blocks, axis=0)

    u_scr[CONV_HDR - 2:CONV_HDR, :] = u_scr[CONV_HDR - 2 + rows:CONV_HDR + rows, :]
    k_scr[0:BLOCK, :] = k_scr[rows:rows + BLOCK, :]
    v_scr[0:BLOCK, :] = v_scr[rows:rows + BLOCK, :]

    mix = jnp.concatenate([ya, yb], axis=-1).astype(BF16)
    return x + _dot(mix, wout_ref[...])


def _pipeline_steps(s, x_ref, y_ref, x1_scr, mixer, mlp):
    slot = lax.rem(s, 2)

    def mix_tile():
        x1_scr[slot] = yield from mixer(x_ref[...], SEQ_TILE)

    def mlp_tile():
        y_ref[...] = yield from mlp(x1_scr[1 - slot])

    return mlp_tile, mix_tile


def _l0_prompt_kernel(sinks_ref, x_ref, meta_ref, gmix_ref, win_ref, segq_ref, segk_ref, qg_ref, kg_ref,
                      cw_ref, wout_ref, gmlp_ref, wup_ref, wdn_ref,
                      y_ref, ymeta_ref, conv_ref, kout_ref, vout_ref,
                      u_scr, k_scr, v_scr, first_scr, band_scr, x1_scr):
    s = pl.program_id(1)
    last = pl.num_programs(1) - 1
    mixer = functools.partial(
        _l0_mixer, sinks_ref=sinks_ref, gmix=gmix_ref[...], win_ref=win_ref, segq_ref=segq_ref,
        segk_ref=segk_ref, qg=qg_ref[...], kg=kg_ref[...], cw_ref=cw_ref, wout_ref=wout_ref,
        u_scr=u_scr, k_scr=k_scr, v_scr=v_scr, first_scr=first_scr, band_scr=band_scr)
    mlp = functools.partial(_mlp_steps, g=gmlp_ref[...], wup_ref=wup_ref, wdn_ref=wdn_ref)
    mlp_tile, mix_tile = _pipeline_steps(s, x_ref, y_ref, x1_scr, mixer, mlp)

    def band_bias(first_valid_key):
        c = lax.broadcasted_iota(jnp.int32, (2 * BLOCK, GROUP * BLOCK), 0)
        q = lax.broadcasted_iota(jnp.int32, (2 * BLOCK, GROUP * BLOCK), 1) & (BLOCK - 1)
        return jnp.where((c >= q) & (c <= q + WINDOW) & (c >= first_valid_key), 0.0, NEG)

    @pl.when(s == 0)
    def _start():
        band_scr[...] = band_bias(0)
        u_scr[0:CONV_HDR, :] = jnp.zeros((CONV_HDR, D_CONV), F32)
        k_scr[0:BLOCK, :] = jnp.zeros((BLOCK, KV_DIM), F32)
        v_scr[0:BLOCK, :] = jnp.zeros((BLOCK, KV_DIM), F32)
        first_scr[...] = band_bias(BLOCK + META_PAD)
        x1_meta, = _interleave(mixer(meta_ref[...], BLOCK))
        ymeta_ref[...], = _interleave(mlp(x1_meta))
        first_scr[...] = band_bias(META_PAD)
        _interleave(mix_tile())
        first_scr[...] = band_scr[...]

    @pl.when((s > 0) & (s < last))
    def _steady():
        _interleave(mlp_tile(), mix_tile(), shares=(MLP_YIELDS, L0_MIX_YIELDS))

    @pl.when(s == last)
    def _drain():
        _interleave(mlp_tile())

    @pl.when(s == last - 1)
    def _state():
        conv_ref[...] = u_scr[CONV_HDR - 2:CONV_HDR, :]
        kout_ref[...] = k_scr[0:BLOCK, :]
        vout_ref[...] = v_scr[0:BLOCK, :]


def _l1_mixer(x, rows, is_meta, gmix, wpool_ref, pscale, h_scr):
    h = _rms(x, gmix)
    h_scr[POOL_MAX:POOL_MAX + rows, :] = h
    ys = []
    for gi, w in enumerate(POOL_WINDOWS):
        cols = slice(gi * POOL_GROUP_DIM, (gi + 1) * POOL_GROUP_DIM)
        hg = h[:, cols]
        acc = hg
        for j in range(1, w):
            acc = acc + h_scr[POOL_MAX - j:POOL_MAX - j + rows, cols]
        if is_meta:
            r = lax.broadcasted_iota(jnp.int32, (rows, 1), 0)
            pooled = acc / jnp.clip(r - (META_PAD - 1), 1, w).astype(F32)
        else:
            pooled = acc * (1.0 / w)
        ys.append(_dot((pooled - hg).astype(BF16), wpool_ref[gi]))
        yield
    h_scr[0:POOL_MAX, :] = h_scr[rows:rows + POOL_MAX, :]
    return x + jnp.concatenate(ys, axis=-1) * pscale


def _l1_prompt_kernel(x_ref, meta_ref, gmix_ref, wpool_ref, pscale_ref, gmlp_ref, wup_ref, wdn_ref,
                      y_ref, pool_ref, h_scr, x1_scr):
    s = pl.program_id(1)
    last = pl.num_programs(1) - 1
    mixer = functools.partial(_l1_mixer, gmix=gmix_ref[...], wpool_ref=wpool_ref, pscale=pscale_ref[...],
                              h_scr=h_scr)
    mlp = functools.partial(_mlp_steps, g=gmlp_ref[...], wup_ref=wup_ref, wdn_ref=wdn_ref)
    mlp_tile, mix_tile = _pipeline_steps(s, x_ref, y_ref, x1_scr,
                                         functools.partial(mixer, is_meta=False), mlp)

    @pl.when(s == 0)
    def _start():
        h_scr[0:POOL_MAX, :] = jnp.zeros((POOL_MAX, D_MODEL), F32)
        _interleave(mixer(meta_ref[...], BLOCK, True))
        _interleave(mix_tile())

    @pl.when((s > 0) & (s < last))
    def _steady():
        _interleave(mlp_tile(), mix_tile(), shares=(MLP_YIELDS, len(POOL_WINDOWS)))

    @pl.when(s == last)
    def _drain():
        _interleave(mlp_tile())

    @pl.when(s == last - 1)
    def _state():
        pool_ref[...] = h_scr[1:POOL_MAX, :]


def _s0_pre_kernel(x_ref, st_ref, gmix_ref, win_ref, segq_ref, segk_ref, qg_ref, kg_ref, cw_ref,
                   ya_ref, q_ref, k_ref, v_ref, conv_ref, win_bf_ref):
    n_seq = st_ref.shape[1]
    n_t = x_ref.shape[0] // n_seq
    win_bf_ref[...] = win_ref[...].astype(BF16)
    u, gb, qn, kn, v = _in_proj(x_ref[...], gmix_ref[...], win_bf_ref, segq_ref, segk_ref, qg_ref[...],
                                kg_ref[...])
    ue = [st_ref[0], st_ref[1]] + [u[t * n_seq:(t + 1) * n_seq, :] for t in range(n_t)]
    for t in range(n_t):
        y = ue[t] * cw_ref[0:1, :] + ue[t + 1] * cw_ref[1:2, :] + ue[t + 2] * cw_ref[2:3, :]
        ya_ref[t * n_seq:(t + 1) * n_seq, :] = gb[t * n_seq:(t + 1) * n_seq, :] * y
    q_ref[...] = qn
    k_ref[...] = kn
    v_ref[...] = v
    conv_ref[0] = ue[-2]
    conv_ref[1] = ue[-1]


def _s0_attn_kernel(q_ref, ck_ref, cv_ref, kn_ref, vn_ref, sk_ref, o_ref, kout_ref, vout_ref, ke_scr, ve_scr):
    bb, n_t = kn_ref.shape[0], kn_ref.shape[1]
    wb = ck_ref.shape[1]
    pad = SAMPLE_KEYS - wb - n_t
    for src, new, scr, out in ((ck_ref, kn_ref, ke_scr, kout_ref), (cv_ref, vn_ref, ve_scr, vout_ref)):
        scr[:, 0:wb, :] = src[...]
        scr[:, wb:wb + n_t, :] = new[...]
        scr[:, wb + n_t:, :] = jnp.zeros((bb, pad, KV_DIM), F32)
        out[...] = scr[:, n_t:n_t + wb, :]
    rows = n_t * GROUP
    t = lax.broadcasted_iota(jnp.int32, (rows, SAMPLE_KEYS), 0) // GROUP
    c = lax.broadcasted_iota(jnp.int32, (rows, SAMPLE_KEYS), 1)
    dist = wb + t - c
    bias = jnp.where((dist >= 0) & (dist <= WINDOW), 0.0, NEG)[None]
    for j in range(N_KV):
        lanes = slice(j * HEAD_DIM, (j + 1) * HEAD_DIM)
        q = q_ref[:, j].astype(BF16)
        k = ke_scr[:, :, lanes].astype(BF16)
        v = ve_scr[:, :, lanes].astype(BF16)
        sk = sk_ref[j][None]
        s = jnp.einsum('bqd,bkd->bqk', q, k, preferred_element_type=F32) + bias
        m = jnp.maximum(jnp.max(s, axis=-1, keepdims=True), sk)
        p = jnp.exp(s - m)
        den = jnp.sum(p, axis=-1, keepdims=True) + jnp.exp(sk - m)
        o = jnp.einsum('bqk,bkd->bqd', p.astype(BF16), v, preferred_element_type=F32)
        o_ref[:, j] = o / den


def _mlp_stream_step(c, wup_ref, wdn_ref, y_ref, wup_bf_ref, wdn_bf_ref, x1_scr, xn_scr):
    wup_bf_ref[...] = wup_ref[...].astype(BF16)
    wdn_bf_ref[...] = wdn_ref[...].astype(BF16)
    a = jnp.square(jnp.maximum(_dot(xn_scr[...], wup_bf_ref[...]), 0.0)).astype(BF16)
    x1_scr[...] += _dot(a, wdn_bf_ref[...])

    @pl.when(c == pl.num_programs(0) - 1)
    def _done():
        y_ref[...] = x1_scr[...]


def _s0_post_kernel(x_ref, ya_ref, yb_ref, wout_ref, gmlp_ref, wup_ref, wdn_ref,
                    y_ref, wout_bf_ref, wup_bf_ref, wdn_bf_ref, x1_scr, xn_scr):
    c = pl.program_id(0)

    @pl.when(c == 0)
    def _mix():
        wout_bf_ref[...] = wout_ref[...].astype(BF16)
        mix = jnp.concatenate([ya_ref[...], yb_ref[...]], axis=-1).astype(BF16)
        x1 = x_ref[...] + _dot(mix, wout_bf_ref[...])
        x1_scr[...] = x1
        xn_scr[...] = _rms(x1, gmlp_ref[...]).astype(BF16)

    _mlp_stream_step(c, wup_ref, wdn_ref, y_ref, wup_bf_ref, wdn_bf_ref, x1_scr, xn_scr)


def _s1_kernel(x_ref, buf_ref, gmix_ref, wpool_ref, pscale_ref, gmlp_ref, wup_ref, wdn_ref,
               y_ref, pool_ref, wpool_bf_ref, wup_bf_ref, wdn_bf_ref, x1_scr, xn_scr):
    c = pl.program_id(0)

    @pl.when(c == 0)
    def _mix():
        n_buf, n_seq = buf_ref.shape[0], buf_ref.shape[1]
        n_t = x_ref.shape[0] // n_seq
        wpool_bf_ref[...] = wpool_ref[...].astype(BF16)
        x = x_ref[...]
        h = _rms(x, gmix_ref[...])
        he = [buf_ref[i] for i in range(n_buf)] + [h[t * n_seq:(t + 1) * n_seq, :] for t in range(n_t)]
        for i in range(n_buf):
            pool_ref[i] = he[n_t + i]
        ys = []
        for gi, w in enumerate(POOL_WINDOWS):
            cols = slice(gi * POOL_GROUP_DIM, (gi + 1) * POOL_GROUP_DIM)
            dl = []
            for t in range(n_t):
                acc = he[n_buf + t][:, cols]
                for j in range(1, w):
                    acc = acc + he[n_buf + t - j][:, cols]
                dl.append(acc * (1.0 / w) - he[n_buf + t][:, cols])
            ys.append(_dot(jnp.concatenate(dl, axis=0).astype(BF16), wpool_bf_ref[gi]))
        x1 = x + jnp.concatenate(ys, axis=-1) * pscale_ref[...]
        x1_scr[...] = x1
        xn_scr[...] = _rms(x1, gmlp_ref[...]).astype(BF16)

    _mlp_stream_step(c, wup_ref, wdn_ref, y_ref, wup_bf_ref, wdn_bf_ref, x1_scr, xn_scr)


class _Stacked(NamedTuple):
    array: jax.Array
    layer: int


def _operand(arg):
    return arg.array if isinstance(arg, _Stacked) else arg


def _resident(arg, n_grid):
    if isinstance(arg, _Stacked):
        shape, idx = (None,) + arg.array.shape[1:], (arg.layer,) + (0,) * (arg.array.ndim - 1)
    else:
        shape, idx = arg.shape, (0,) * arg.ndim
    index_map = {1: lambda i: idx, 2: lambda i, j: idx}[n_grid]
    return pl.BlockSpec(shape, index_map, pipeline_mode=pl.Buffered(1))


def _params(n_grid):
    return pltpu.CompilerParams(dimension_semantics=("arbitrary",) * n_grid, vmem_limit_bytes=VMEM_LIMIT)


def _prompt_specs(n_tiles):
    tile_in = pl.BlockSpec((None, SEQ_TILE, D_MODEL), lambda b, s: (b, jnp.minimum(s, n_tiles - 1), 0))
    tile_out = pl.BlockSpec((None, SEQ_TILE, D_MODEL), lambda b, s: (b, jnp.maximum(s - 1, 0), 0))
    per_b = lambda r, c: pl.BlockSpec((None, r, c), lambda b, s: (b, 0, 0))
    return tile_in, tile_out, per_b


def _l0_prompt(x, sinks, *resident):
    n_b, seq, _ = x.shape
    n_tiles = seq // SEQ_TILE
    tile_in, tile_out, per_b = _prompt_specs(n_tiles)
    return pl.pallas_call(
        _l0_prompt_kernel,
        grid=(n_b, n_tiles + 1),
        in_specs=[pl.BlockSpec(memory_space=pltpu.SMEM), tile_in] + [_resident(a, 2) for a in resident],
        out_specs=[tile_out, per_b(BLOCK, D_MODEL), per_b(CONV_W - 1, D_CONV), per_b(WINDOW, KV_DIM),
                   per_b(WINDOW, KV_DIM)],
        out_shape=[jax.ShapeDtypeStruct((n_b, seq, D_MODEL), F32),
                   jax.ShapeDtypeStruct((n_b, BLOCK, D_MODEL), F32),
                   jax.ShapeDtypeStruct((n_b, CONV_W - 1, D_CONV), F32),
                   jax.ShapeDtypeStruct((n_b, WINDOW, KV_DIM), F32),
                   jax.ShapeDtypeStruct((n_b, WINDOW, KV_DIM), F32)],
        scratch_shapes=[pltpu.VMEM((CONV_HDR + SEQ_TILE, D_CONV), F32),
                        pltpu.VMEM((BLOCK + SEQ_TILE, KV_DIM), F32),
                        pltpu.VMEM((BLOCK + SEQ_TILE, KV_DIM), F32),
                        pltpu.VMEM((2 * BLOCK, GROUP * BLOCK), F32),
                        pltpu.VMEM((2 * BLOCK, GROUP * BLOCK), F32),
                        pltpu.VMEM((2, SEQ_TILE, D_MODEL), F32)],
        compiler_params=_params(2),
        name="l0_prompt",
    )(sinks, x, *[_operand(a) for a in resident])


def _l1_prompt(x, xmeta, *resident):
    n_b, seq, _ = x.shape
    n_tiles = seq // SEQ_TILE
    tile_in, tile_out, per_b = _prompt_specs(n_tiles)
    return pl.pallas_call(
        _l1_prompt_kernel,
        grid=(n_b, n_tiles + 1),
        in_specs=[tile_in, per_b(BLOCK, D_MODEL)] + [_resident(a, 2) for a in resident],
        out_specs=[tile_out, per_b(POOL_MAX - 1, D_MODEL)],
        out_shape=[jax.ShapeDtypeStruct((n_b, seq, D_MODEL), F32),
                   jax.ShapeDtypeStruct((n_b, POOL_MAX - 1, D_MODEL), F32)],
        scratch_shapes=[pltpu.VMEM((POOL_MAX + SEQ_TILE, D_MODEL), F32),
                        pltpu.VMEM((2, SEQ_TILE, D_MODEL), F32)],
        compiler_params=_params(2),
        name="l1_prompt",
    )(x, xmeta, *[_operand(a) for a in resident])


def _single_step(kernel, name, out_shape, *args):
    return pl.pallas_call(
        kernel,
        grid=(1,),
        in_specs=[_resident(a, 1) for a in args],
        out_specs=[pl.BlockSpec(o.shape, lambda i, n=len(o.shape): (0,) * n) for o in out_shape],
        out_shape=out_shape,
        compiler_params=_params(1),
        name=name,
    )(*[_operand(a) for a in args])


def _streamed_mlp(kernel, name, outs, consts, w_up, w_down, layer):
    n_rows = outs[0].shape[0]
    const = lambda shape: pl.BlockSpec(shape, lambda c, n=len(shape): (0,) * n)
    return pl.pallas_call(
        kernel,
        grid=(D_FF // FF_STREAM,),
        in_specs=[_resident(a, 1) for a in consts]
        + [pl.BlockSpec((None, D_MODEL, FF_STREAM), lambda c: (layer, 0, c)),
           pl.BlockSpec((None, FF_STREAM, D_MODEL), lambda c: (layer, c, 0))],
        out_specs=[const(o.shape) for o in outs]
        + [pl.BlockSpec((D_MODEL, FF_STREAM), lambda c: (0, c)),
           pl.BlockSpec((FF_STREAM, D_MODEL), lambda c: (c, 0))],
        out_shape=list(outs) + [jax.ShapeDtypeStruct((D_MODEL, D_FF), BF16),
                                jax.ShapeDtypeStruct((D_FF, D_MODEL), BF16)],
        scratch_shapes=[pltpu.VMEM((n_rows, D_MODEL), F32), pltpu.VMEM((n_rows, D_MODEL), BF16)],
        compiler_params=_params(1),
        name=name,
    )(*[_operand(a) for a in consts], w_up, w_down)


def _s0_attn(q4, ck, cv, kn, vn, sk_rows):
    n_seq, wb, _ = ck.shape
    n_t = kn.shape[1]
    rows = n_t * GROUP
    blk = lambda *tail: pl.BlockSpec((SAMPLE_BB,) + tail, lambda i: (i,) + (0,) * len(tail))
    return pl.pallas_call(
        _s0_attn_kernel,
        grid=(n_seq // SAMPLE_BB,),
        in_specs=[blk(N_KV, rows, HEAD_DIM), blk(wb, KV_DIM), blk(wb, KV_DIM), blk(n_t, KV_DIM),
                  blk(n_t, KV_DIM), pl.BlockSpec((N_KV, rows, 1), lambda i: (0, 0, 0))],
        out_specs=[blk(N_KV, rows, HEAD_DIM), blk(wb, KV_DIM), blk(wb, KV_DIM)],
        out_shape=[jax.ShapeDtypeStruct((n_seq, N_KV, rows, HEAD_DIM), F32),
                   jax.ShapeDtypeStruct((n_seq, wb, KV_DIM), F32),
                   jax.ShapeDtypeStruct((n_seq, wb, KV_DIM), F32)],
        scratch_shapes=[pltpu.VMEM((SAMPLE_BB, SAMPLE_KEYS, KV_DIM), F32),
                        pltpu.VMEM((SAMPLE_BB, SAMPLE_KEYS, KV_DIM), F32)],
        compiler_params=_params(1),
        name="s0_attn",
    )(q4, ck, cv, kn, vn, sk_rows)


def kernel(x_prompt, x_sample, state_conv, cache_k_win, cache_v_win, state_pool, meta_tokens, norm_mix, norm_mlp, w_in_even, conv_w, q_norm, k_norm, attn_sinks, w_out_even, w_pool, pool_scale, w_up, w_down):
    n_seq, n_t, _ = x_sample.shape
    wb = cache_k_win.shape[2]
    assert x_prompt.shape[1] % SEQ_TILE == 0 and x_prompt.shape[1] >= 2 * SEQ_TILE and n_seq % SAMPLE_BB == 0
    assert wb == WINDOW and wb + n_t <= SAMPLE_KEYS

    gmix = [_Stacked(norm_mix[:, None, :], l) for l in range(2)]
    gmlp = [_Stacked(norm_mlp[:, None, :], l) for l in range(2)]
    qg = jnp.tile(q_norm[0], N_HEADS)[None, :]
    kg = jnp.tile(k_norm[0], N_KV)[None, :]
    segq = jnp.kron(jnp.eye(N_HEADS, dtype=F32), jnp.full((HEAD_DIM, HEAD_DIM), 1.0 / HEAD_DIM, F32)).astype(BF16)
    segk = segq[:KV_DIM, :KV_DIM]
    cw = _Stacked(conv_w, 0)
    sinks = attn_sinks[0]
    meta_pad = jnp.pad(meta_tokens, ((META_PAD, 0), (0, 0)))
    sds = lambda *shape, dtype=F32: jax.ShapeDtypeStruct(shape, dtype)

    n_rows = n_t * n_seq
    xs = jnp.transpose(x_sample, (1, 0, 2)).reshape(n_rows, D_MODEL)
    st_conv = jnp.transpose(state_conv[0], (1, 0, 2))
    ya, qn, kn, vn, conv_s, win = _single_step(
        _s0_pre_kernel, "s0_pre",
        [sds(n_rows, D_CONV), sds(n_rows, Q_DIM), sds(n_rows, KV_DIM), sds(n_rows, KV_DIM),
         sds(CONV_W - 1, n_seq, D_CONV), sds(D_MODEL, D_IN_EVEN, dtype=BF16)],
        xs, st_conv, gmix[0], _Stacked(w_in_even, 0), segq, segk, qg, kg, cw)
    q4 = qn.reshape(n_t, n_seq, N_KV, GROUP, HEAD_DIM).transpose(1, 2, 0, 3, 4)
    q4 = q4.reshape(n_seq, N_KV, n_t * GROUP, HEAD_DIM)
    kn_b = kn.reshape(n_t, n_seq, KV_DIM).transpose(1, 0, 2)
    vn_b = vn.reshape(n_t, n_seq, KV_DIM).transpose(1, 0, 2)
    sk_rows = jnp.tile(sinks.reshape(N_KV, 1, GROUP), (1, n_t, 1)).reshape(N_KV, n_t * GROUP, 1)
    o4, k_s, v_s = _s0_attn(q4, cache_k_win[0].reshape(n_seq, wb, KV_DIM),
                            cache_v_win[0].reshape(n_seq, wb, KV_DIM), kn_b, vn_b, sk_rows)
    yb = o4.reshape(n_seq, N_KV, n_t, GROUP, HEAD_DIM).transpose(2, 0, 1, 3, 4).reshape(n_rows, Q_DIM)
    xs2, wout, wup0, wdn0 = _streamed_mlp(
        _s0_post_kernel, "s0_post", [sds(n_rows, D_MODEL), sds(D_MODEL, D_MODEL, dtype=BF16)],
        [xs, ya, yb, _Stacked(w_out_even, 0), gmlp[0]], w_up, w_down, 0)
    buf_t = jnp.transpose(state_pool[0], (1, 0, 2))
    ys, pool_s, wpool, wup1, wdn1 = _streamed_mlp(
        _s1_kernel, "s1",
        [sds(n_rows, D_MODEL), sds(POOL_MAX - 1, n_seq, D_MODEL), sds(*w_pool.shape[1:], dtype=BF16)],
        [xs2, buf_t, gmix[1], _Stacked(w_pool, 0), pool_scale, gmlp[1]], w_up, w_down, 1)
    y_sample = ys.reshape(n_t, n_seq, D_MODEL).transpose(1, 0, 2)

    x2, x2_meta, conv_p, k_p, v_p = _l0_prompt(x_prompt, sinks, meta_pad, gmix[0], win, segq, segk, qg, kg, cw,
                                               wout, gmlp[0], wup0, wdn0)
    y_prompt, pool_p = _l1_prompt(x2, x2_meta, gmix[1], wpool, pool_scale, gmlp[1], wup1, wdn1)

    kv5 = lambda a: a.reshape(1, a.shape[0], WINDOW, N_KV, HEAD_DIM)
    return (y_prompt, y_sample,
            conv_p[None], jnp.transpose(conv_s, (1, 0, 2))[None],
            kv5(k_p), kv5(k_s), kv5(v_p), kv5(v_s),
            pool_p[None], jnp.transpose(pool_s, (1, 0, 2))[None])
```

```python
import functools
from typing import NamedTuple

import jax
import jax.numpy as jnp
from jax import lax
from jax.experimental import pallas as pl
from jax.experimental.pallas import tpu as pltpu

F32 = jnp.float32
BF16 = jnp.bfloat16

D_MODEL = 1024
D_CONV = 512
CONV_W = 3
HEAD_DIM = 64
N_HEADS = 8
N_KV = 2
GROUP = N_HEADS // N_KV
WINDOW = 128
BLOCK = 128
Q_DIM = N_HEADS * HEAD_DIM
KV_DIM = N_KV * HEAD_DIM
POOL_WINDOWS = (2, 4, 8, 16)
POOL_GROUP_DIM = D_MODEL // len(POOL_WINDOWS)
POOL_MAX = 16
D_FF = 4 * D_MODEL
D_IN_EVEN = 3 * D_CONV + Q_DIM + 2 * KV_DIM
N_META = 16
EPS = 1e-6
NEG = -1e30

META_PAD = BLOCK - N_META
SEQ_TILE = 512
FF_CHUNK = 1024
FF_STREAM = 512
CONV_HDR = 8
SAMPLE_BB = 16
SAMPLE_KEYS = 144
VMEM_LIMIT = 56 * 1024 * 1024
MLP_YIELDS = 2 * (D_FF // FF_CHUNK)
L0_MIX_YIELDS = 1 + 2 * N_KV * (SEQ_TILE // BLOCK)


def _dot(a, b):
    return jnp.dot(a, b, preferred_element_type=F32)


def _dot_nt(a, b):
    return lax.dot_general(a, b, (((1,), (1,)), ((), ())), preferred_element_type=F32)


def _dot_tn(a, b):
    return lax.dot_general(a, b, (((0,), (0,)), ((), ())), preferred_element_type=F32)


def _rms(x, g):
    ms = jnp.mean(x * x, axis=-1, keepdims=True)
    return x * lax.rsqrt(ms + EPS) * g


def _head_rms(x, seg, g):
    ms = _dot((x * x).astype(BF16), seg)
    return x * lax.rsqrt(ms + EPS) * g


def _interleave(*steppers, shares=None):
    shares = shares or [1] * len(steppers)
    results = [None] * len(steppers)
    done = [0] * len(steppers)
    live = set(range(len(steppers)))
    while live:
        i = min(live, key=lambda k: ((done[k] + 1) / shares[k], k))
        try:
            next(steppers[i])
            done[i] += 1
        except StopIteration as finished:
            results[i] = finished.value
            live.discard(i)
    return results


def _mlp_steps(x, g, wup_ref, wdn_ref):
    xn = _rms(x, g).astype(BF16)
    acc = x
    for c in range(D_FF // FF_CHUNK):
        cols = slice(c * FF_CHUNK, (c + 1) * FF_CHUNK)
        h = _dot(xn, wup_ref[:, cols])
        a = jnp.square(jnp.maximum(h, 0.0)).astype(BF16)
        yield
        acc = acc + _dot(a, wdn_ref[cols, :])
        yield
    return acc


def _mlp(x, g, wup_ref, wdn_ref):
    return _interleave(_mlp_steps(x, g, wup_ref, wdn_ref))[0]


def _in_proj(x, g, win_ref, segq_ref, segk_ref, qg, kg):
    hn = _rms(x, g).astype(BF16)
    kv_col = 3 * D_CONV + Q_DIM
    half = x.shape[0] // 2
    z = _dot(hn, win_ref[:, 0:kv_col])
    kv = jnp.concatenate([_dot(hn[0:half], win_ref[:, kv_col:]), _dot(hn[half:], win_ref[:, kv_col:])], axis=0)
    xa = z[:, 0:D_CONV]
    gc = z[:, D_CONV:2 * D_CONV]
    gb = z[:, 2 * D_CONV:3 * D_CONV]
    q = z[:, 3 * D_CONV:]
    k = kv[:, 0:KV_DIM]
    v = kv[:, KV_DIM:]
    qn = _head_rms(q, segq_ref[...], qg) * (HEAD_DIM ** -0.5)
    kn = _head_rms(k, segk_ref[...], kg)
    return gc * xa, gb, qn, kn, v


def _l0_mixer(x, rows, sinks_ref, gmix, win_ref, segq_ref, segk_ref, qg, kg, cw_ref, wout_ref,
              u_scr, k_scr, v_scr, first_scr, band_scr):
    u, gb, qn, kn, v = _in_proj(x, gmix, win_ref, segq_ref, segk_ref, qg, kg)

    u_scr[CONV_HDR:CONV_HDR + rows, :] = u
    u1 = u_scr[CONV_HDR - 1:CONV_HDR - 1 + rows, :]
    u2 = u_scr[CONV_HDR - 2:CONV_HDR - 2 + rows, :]
    ya = gb * (u2 * cw_ref[0:1, :] + u1 * cw_ref[1:2, :] + u * cw_ref[2:3, :])
    k_scr[BLOCK:BLOCK + rows, :] = kn
    v_scr[BLOCK:BLOCK + rows, :] = v
    yield

    yb_blocks = []
    for i in range(rows // BLOCK):
        qb = qn[i * BLOCK:(i + 1) * BLOCK, :]
        bias = first_scr[...] if i == 0 else band_scr[...]
        heads_t = []
        for j in range(N_KV):
            lanes = slice(j * HEAD_DIM, (j + 1) * HEAD_DIM)
            k2 = k_scr[i * BLOCK:(i + 2) * BLOCK, lanes].astype(BF16)
            v2 = v_scr[i * BLOCK:(i + 2) * BLOCK, lanes].astype(BF16)
            qs = jnp.concatenate(
                [qb[:, (j * GROUP + g) * HEAD_DIM:(j * GROUP + g + 1) * HEAD_DIM] for g in range(GROUP)],
                axis=0).astype(BF16)
            sk = jnp.concatenate(
                [jnp.full((1, BLOCK), sinks_ref[j * GROUP + g], F32) for g in range(GROUP)], axis=1)
            st = _dot_nt(k2, qs) + bias
            m = jnp.maximum(jnp.max(st, axis=0, keepdims=True), sk)
            p = jnp.exp(st - m)
            den = jnp.sum(p, axis=0, keepdims=True) + jnp.exp(sk - m)
            yield
            ot = _dot_tn(v2, p.astype(BF16)) / den
            heads_t += [ot[:, g * BLOCK:(g + 1) * BLOCK] for g in range(GROUP)]
            yield
        yb_blocks.append(jnp.concatenate(heads_t, axis=0).T)
    yb = yb_blocks[0] if len(yb_blocks) == 1 else jnp.concatenate(yb_blocks, axis=0)

    u_scr[CONV_HDR - 2:CONV_HDR, :] = u_scr[CONV_HDR - 2 + rows:CONV_HDR + rows, :]
    k_scr[0:BLOCK, :] = k_scr[rows:rows + BLOCK, :]
    v_scr[0:BLOCK, :] = v_scr[rows:rows + BLOCK, :]

    mix = jnp.concatenate([ya, yb], axis=-1).astype(BF16)
    return x + _dot(mix, wout_ref[...])


def _pipeline_steps(s, x_ref, y_ref, x1_scr, mixer, mlp):
    slot = lax.rem(s, 2)

    def mix_tile():
        x1_scr[slot] = yield from mixer(x_ref[...], SEQ_TILE)

    def mlp_tile():
        y_ref[...] = yield from mlp(x1_scr[1 - slot])

    return mlp_tile, mix_tile


def _l0_prompt_kernel(sinks_ref, x_ref, meta_ref, gmix_ref, win_ref, segq_ref, segk_ref, qg_ref, kg_ref,
                      cw_ref, wout_ref, gmlp_ref, wup_ref, wdn_ref,
                      y_ref, ymeta_ref, conv_ref, kout_ref, vout_ref,
                      u_scr, k_scr, v_scr, first_scr, band_scr, x1_scr):
    s = pl.program_id(1)
    last = pl.num_programs(1) - 1
    mixer = functools.partial(
        _l0_mixer, sinks_ref=sinks_ref, gmix=gmix_ref[...], win_ref=win_ref, segq_ref=segq_ref,
        segk_ref=segk_ref, qg=qg_ref[...], kg=kg_ref[...], cw_ref=cw_ref, wout_ref=wout_ref,
        u_scr=u_scr, k_scr=k_scr, v_scr=v_scr, first_scr=first_scr, band_scr=band_scr)
    mlp = functools.partial(_mlp_steps, g=gmlp_ref[...], wup_ref=wup_ref, wdn_ref=wdn_ref)
    mlp_tile, mix_tile = _pipeline_steps(s, x_ref, y_ref, x1_scr, mixer, mlp)

    def band_bias(first_valid_key):
        c = lax.broadcasted_iota(jnp.int32, (2 * BLOCK, GROUP * BLOCK), 0)
        q = lax.broadcasted_iota(jnp.int32, (2 * BLOCK, GROUP * BLOCK), 1) & (BLOCK - 1)
        return jnp.where((c >= q) & (c <= q + WINDOW) & (c >= first_valid_key), 0.0, NEG)

    @pl.when(s == 0)
    def _start():
        band_scr[...] = band_bias(0)
        u_scr[0:CONV_HDR, :] = jnp.zeros((CONV_HDR, D_CONV), F32)
        k_scr[0:BLOCK, :] = jnp.zeros((BLOCK, KV_DIM), F32)
        v_scr[0:BLOCK, :] = jnp.zeros((BLOCK, KV_DIM), F32)
        first_scr[...] = band_bias(BLOCK + META_PAD)
        x1_meta, = _interleave(mixer(meta_ref[...], BLOCK))
        ymeta_ref[...], = _interleave(mlp(x1_meta))
        first_scr[...] = band_bias(META_PAD)
        _interleave(mix_tile())
        first_scr[...] = band_scr[...]

    @pl.when((s > 0) & (s < last))
    def _steady():
        _interleave(mlp_tile(), mix_tile(), shares=(MLP_YIELDS, L0_MIX_YIELDS))

    @pl.when(s == last)
    def _drain():
        _interleave(mlp_tile())

    @pl.when(s == last - 1)
    def _state():
        conv_ref[...] = u_scr[CONV_HDR - 2:CONV_HDR, :]
        kout_ref[...] = k_scr[0:BLOCK, :]
        vout_ref[...] = v_scr[0:BLOCK, :]


def _l1_mixer(x, rows, is_meta, gmix, wpool_ref, pscale, h_scr):
    h = _rms(x, gmix)
    h_scr[POOL_MAX:POOL_MAX + rows, :] = h
    ys = []
    for gi, w in enumerate(POOL_WINDOWS):
        cols = slice(gi * POOL_GROUP_DIM, (gi + 1) * POOL_GROUP_DIM)
        hg = h[:, cols]
        acc = hg
        for j in range(1, w):
            acc = acc + h_scr[POOL_MAX - j:POOL_MAX - j + rows, cols]
        if is_meta:
            r = lax.broadcasted_iota(jnp.int32, (rows, 1), 0)
            pooled = acc / jnp.clip(r - (META_PAD - 1), 1, w).astype(F32)
        else:
            pooled = acc * (1.0 / w)
        ys.append(_dot((pooled - hg).astype(BF16), wpool_ref[gi]))
        yield
    h_scr[0:POOL_MAX, :] = h_scr[rows:rows + POOL_MAX, :]
    return x + jnp.concatenate(ys, axis=-1) * pscale


def _l1_prompt_kernel(x_ref, meta_ref, gmix_ref, wpool_ref, pscale_ref, gmlp_ref, wup_ref, wdn_ref,
                      y_ref, pool_ref, h_scr, x1_scr):
    s = pl.program_id(1)
    last = pl.num_programs(1) - 1
    mixer = functools.partial(_l1_mixer, gmix=gmix_ref[...], wpool_ref=wpool_ref, pscale=pscale_ref[...],
                              h_scr=h_scr)
    mlp = functools.partial(_mlp_steps, g=gmlp_ref[...], wup_ref=wup_ref, wdn_ref=wdn_ref)
    mlp_tile, mix_tile = _pipeline_steps(s, x_ref, y_ref, x1_scr,
                                         functools.partial(mixer, is_meta=False), mlp)

    @pl.when(s == 0)
    def _start():
        h_scr[0:POOL_MAX, :] = jnp.zeros((POOL_MAX, D_MODEL), F32)
        _interleave(mixer(meta_ref[...], BLOCK, True))
        _interleave(mix_tile())

    @pl.when((s > 0) & (s < last))
    def _steady():
        _interleave(mlp_tile(), mix_tile(), shares=(MLP_YIELDS, len(POOL_WINDOWS)))

    @pl.when(s == last)
    def _drain():
        _interleave(mlp_tile())

    @pl.when(s == last - 1)
    def _state():
        pool_ref[...] = h_scr[1:POOL_MAX, :]


def _time_major(ref):
    return jnp.concatenate([ref[:, t, :] for t in range(ref.shape[1])], axis=0)


def _s0_pre_kernel(x_ref, st_ref, gmix_ref, win_ref, segq_ref, segk_ref, qg_ref, kg_ref, cw_ref,
                   ya_ref, q_ref, k_ref, v_ref, conv_ref, win_bf_ref):
    n_seq, n_t, _ = x_ref.shape
    win_bf_ref[...] = win_ref[...].astype(BF16)
    u, gb, qn, kn, v = _in_proj(_time_major(x_ref), gmix_ref[...], win_bf_ref, segq_ref, segk_ref, qg_ref[...],
                                kg_ref[...])
    rows = lambda a, t: a[t * n_seq:(t + 1) * n_seq, :]
    ue = [st_ref[:, i, :] for i in range(CONV_W - 1)] + [rows(u, t) for t in range(n_t)]
    for t in range(n_t):
        y = ue[t] * cw_ref[0:1, :] + ue[t + 1] * cw_ref[1:2, :] + ue[t + 2] * cw_ref[2:3, :]
        ya_ref[t * n_seq:(t + 1) * n_seq, :] = rows(gb, t) * y
        k_ref[:, t, :] = rows(kn, t)
        v_ref[:, t, :] = rows(v, t)
        for h in range(N_HEADS):
            q_ref[:, (h // GROUP) * n_t * GROUP + t * GROUP + h % GROUP, :] = (
                rows(qn, t)[:, h * HEAD_DIM:(h + 1) * HEAD_DIM])
    for i in range(CONV_W - 1):
        conv_ref[:, i, :] = ue[n_t + i]


def _s0_attn_kernel(q_ref, ck_ref, cv_ref, kn_ref, vn_ref, sk_ref, o_ref, kout_ref, vout_ref, ke_scr, ve_scr):
    bb, n_t = kn_ref.shape[0], kn_ref.shape[1]
    wb = ck_ref.shape[1]
    pad = SAMPLE_KEYS - wb - n_t
    for src, new, scr, out in ((ck_ref, kn_ref, ke_scr, kout_ref), (cv_ref, vn_ref, ve_scr, vout_ref)):
        scr[:, 0:wb, :] = src[...]
        scr[:, wb:wb + n_t, :] = new[...]
        scr[:, wb + n_t:, :] = jnp.zeros((bb, pad, KV_DIM), F32)
        out[...] = scr[:, n_t:n_t + wb, :]
    rows = n_t * GROUP
    t = lax.broadcasted_iota(jnp.int32, (rows, SAMPLE_KEYS), 0) // GROUP
    c = lax.broadcasted_iota(jnp.int32, (rows, SAMPLE_KEYS), 1)
    dist = wb + t - c
    bias = jnp.where((dist >= 0) & (dist <= WINDOW), 0.0, NEG)[None]
    for j in range(N_KV):
        lanes = slice(j * HEAD_DIM, (j + 1) * HEAD_DIM)
        q = q_ref[:, j * rows:(j + 1) * rows, :].astype(BF16)
        k = ke_scr[:, :, lanes].astype(BF16)
        v = ve_scr[:, :, lanes].astype(BF16)
        sk = sk_ref[j][None]
        s = jnp.einsum('bqd,bkd->bqk', q, k, preferred_element_type=F32) + bias
        m = jnp.maximum(jnp.max(s, axis=-1, keepdims=True), sk)
        p = jnp.exp(s - m)
        den = jnp.sum(p, axis=-1, keepdims=True) + jnp.exp(sk - m)
        o = jnp.einsum('bqk,bkd->bqd', p.astype(BF16), v, preferred_element_type=F32)
        o_ref[:, j * rows:(j + 1) * rows, :] = o / den


def _mlp_stream_step(c, wup_ref, wdn_ref, y_ref, wup_bf_ref, wdn_bf_ref, x1_scr, xn_scr):
    wup_bf_ref[...] = wup_ref[...].astype(BF16)
    wdn_bf_ref[...] = wdn_ref[...].astype(BF16)
    a = jnp.square(jnp.maximum(_dot(xn_scr[...], wup_bf_ref[...]), 0.0)).astype(BF16)
    x1_scr[...] += _dot(a, wdn_bf_ref[...])
    if y_ref is not None:
        @pl.when(c == pl.num_programs(0) - 1)
        def _done():
            y_ref[...] = x1_scr[...]


def _s0_post_kernel(x_ref, ya_ref, o_ref, wout_ref, gmlp_ref, wup_ref, wdn_ref,
                    y_ref, wout_bf_ref, wup_bf_ref, wdn_bf_ref, x1_scr, xn_scr):
    c = pl.program_id(0)

    @pl.when(c == 0)
    def _mix():
        n_t = x_ref.shape[1]
        wout_bf_ref[...] = wout_ref[...].astype(BF16)
        yb = jnp.concatenate(
            [jnp.concatenate([o_ref[:, (h // GROUP) * n_t * GROUP + t * GROUP + h % GROUP, :]
                              for h in range(N_HEADS)], axis=-1) for t in range(n_t)], axis=0)
        mix = jnp.concatenate([ya_ref[...], yb], axis=-1).astype(BF16)
        x1 = _time_major(x_ref) + _dot(mix, wout_bf_ref[...])
        x1_scr[...] = x1
        xn_scr[...] = _rms(x1, gmlp_ref[...]).astype(BF16)

    _mlp_stream_step(c, wup_ref, wdn_ref, y_ref, wup_bf_ref, wdn_bf_ref, x1_scr, xn_scr)


def _s1_kernel(x_ref, buf_ref, gmix_ref, wpool_ref, pscale_ref, gmlp_ref, wup_ref, wdn_ref,
               y_ref, pool_ref, wpool_bf_ref, wup_bf_ref, wdn_bf_ref, x1_scr, xn_scr):
    c = pl.program_id(0)
    n_seq, n_buf, _ = buf_ref.shape
    n_t = x_ref.shape[0] // n_seq

    @pl.when(c == 0)
    def _mix():
        wpool_bf_ref[...] = wpool_ref[...].astype(BF16)
        x = x_ref[...]
        h = _rms(x, gmix_ref[...])
        he = [buf_ref[:, i, :] for i in range(n_buf)] + [h[t * n_seq:(t + 1) * n_seq, :] for t in range(n_t)]
        for i in range(n_buf):
            pool_ref[:, i, :] = he[n_t + i]
        ys = []
        for gi, w in enumerate(POOL_WINDOWS):
            cols = slice(gi * POOL_GROUP_DIM, (gi + 1) * POOL_GROUP_DIM)
            dl = []
            for t in range(n_t):
                acc = he[n_buf + t][:, cols]
                for j in range(1, w):
                    acc = acc + he[n_buf + t - j][:, cols]
                dl.append(acc * (1.0 / w) - he[n_buf + t][:, cols])
            ys.append(_dot(jnp.concatenate(dl, axis=0).astype(BF16), wpool_bf_ref[gi]))
        x1 = x + jnp.concatenate(ys, axis=-1) * pscale_ref[...]
        x1_scr[...] = x1
        xn_scr[...] = _rms(x1, gmlp_ref[...]).astype(BF16)

    _mlp_stream_step(c, wup_ref, wdn_ref, None, wup_bf_ref, wdn_bf_ref, x1_scr, xn_scr)

    @pl.when(c == pl.num_programs(0) - 1)
    def _done():
        for t in range(n_t):
            y_ref[:, t, :] = x1_scr[t * n_seq:(t + 1) * n_seq, :]


class _Stacked(NamedTuple):
    array: jax.Array
    layer: int


def _operand(arg):
    return arg.array if isinstance(arg, _Stacked) else arg


def _resident(arg, n_grid):
    if isinstance(arg, _Stacked):
        shape, idx = (None,) + arg.array.shape[1:], (arg.layer,) + (0,) * (arg.array.ndim - 1)
    else:
        shape, idx = arg.shape, (0,) * arg.ndim
    index_map = {1: lambda i: idx, 2: lambda i, j: idx}[n_grid]
    return pl.BlockSpec(shape, index_map, pipeline_mode=pl.Buffered(1))


def _params(n_grid):
    return pltpu.CompilerParams(dimension_semantics=("arbitrary",) * n_grid, vmem_limit_bytes=VMEM_LIMIT)


def _prompt_specs(n_tiles):
    tile_in = pl.BlockSpec((None, SEQ_TILE, D_MODEL), lambda b, s: (b, jnp.minimum(s, n_tiles - 1), 0))
    tile_out = pl.BlockSpec((None, SEQ_TILE, D_MODEL), lambda b, s: (b, jnp.maximum(s - 1, 0), 0))
    per_b = lambda r, c: pl.BlockSpec((None, r, c), lambda b, s: (b, 0, 0))
    return tile_in, tile_out, per_b


def _l0_prompt(x, sinks, *resident):
    n_b, seq, _ = x.shape
    n_tiles = seq // SEQ_TILE
    tile_in, tile_out, per_b = _prompt_specs(n_tiles)
    return pl.pallas_call(
        _l0_prompt_kernel,
        grid=(n_b, n_tiles + 1),
        in_specs=[pl.BlockSpec(memory_space=pltpu.SMEM), tile_in] + [_resident(a, 2) for a in resident],
        out_specs=[tile_out, per_b(BLOCK, D_MODEL), per_b(CONV_W - 1, D_CONV), per_b(WINDOW, KV_DIM),
                   per_b(WINDOW, KV_DIM)],
        out_shape=[jax.ShapeDtypeStruct((n_b, seq, D_MODEL), F32),
                   jax.ShapeDtypeStruct((n_b, BLOCK, D_MODEL), F32),
                   jax.ShapeDtypeStruct((n_b, CONV_W - 1, D_CONV), F32),
                   jax.ShapeDtypeStruct((n_b, WINDOW, KV_DIM), F32),
                   jax.ShapeDtypeStruct((n_b, WINDOW, KV_DIM), F32)],
        scratch_shapes=[pltpu.VMEM((CONV_HDR + SEQ_TILE, D_CONV), F32),
                        pltpu.VMEM((BLOCK + SEQ_TILE, KV_DIM), F32),
                        pltpu.VMEM((BLOCK + SEQ_TILE, KV_DIM), F32),
                        pltpu.VMEM((2 * BLOCK, GROUP * BLOCK), F32),
                        pltpu.VMEM((2 * BLOCK, GROUP * BLOCK), F32),
                        pltpu.VMEM((2, SEQ_TILE, D_MODEL), F32)],
        compiler_params=_params(2),
        name="l0_prompt",
    )(sinks, x, *[_operand(a) for a in resident])


def _l1_prompt(x, xmeta, *resident):
    n_b, seq, _ = x.shape
    n_tiles = seq // SEQ_TILE
    tile_in, tile_out, per_b = _prompt_specs(n_tiles)
    return pl.pallas_call(
        _l1_prompt_kernel,
        grid=(n_b, n_tiles + 1),
        in_specs=[tile_in, per_b(BLOCK, D_MODEL)] + [_resident(a, 2) for a in resident],
        out_specs=[tile_out, per_b(POOL_MAX - 1, D_MODEL)],
        out_shape=[jax.ShapeDtypeStruct((n_b, seq, D_MODEL), F32),
                   jax.ShapeDtypeStruct((n_b, POOL_MAX - 1, D_MODEL), F32)],
        scratch_shapes=[pltpu.VMEM((POOL_MAX + SEQ_TILE, D_MODEL), F32),
                        pltpu.VMEM((2, SEQ_TILE, D_MODEL), F32)],
        compiler_params=_params(2),
        name="l1_prompt",
    )(x, xmeta, *[_operand(a) for a in resident])


def _single_step(kernel, name, out_shape, *args):
    return pl.pallas_call(
        kernel,
        grid=(1,),
        in_specs=[_resident(a, 1) for a in args],
        out_specs=[pl.BlockSpec(o.shape, lambda i, n=len(o.shape): (0,) * n) for o in out_shape],
        out_shape=out_shape,
        compiler_params=_params(1),
        name=name,
    )(*[_operand(a) for a in args])


def _streamed_mlp(kernel, name, n_rows, outs, consts, w_up, w_down, layer):
    const = lambda shape: pl.BlockSpec(shape, lambda c, n=len(shape): (0,) * n)
    return pl.pallas_call(
        kernel,
        grid=(D_FF // FF_STREAM,),
        in_specs=[_resident(a, 1) for a in consts]
        + [pl.BlockSpec((None, D_MODEL, FF_STREAM), lambda c: (layer, 0, c)),
           pl.BlockSpec((None, FF_STREAM, D_MODEL), lambda c: (layer, c, 0))],
        out_specs=[const(o.shape) for o in outs]
        + [pl.BlockSpec((D_MODEL, FF_STREAM), lambda c: (0, c)),
           pl.BlockSpec((FF_STREAM, D_MODEL), lambda c: (c, 0))],
        out_shape=list(outs) + [jax.ShapeDtypeStruct((D_MODEL, D_FF), BF16),
                                jax.ShapeDtypeStruct((D_FF, D_MODEL), BF16)],
        scratch_shapes=[pltpu.VMEM((n_rows, D_MODEL), F32), pltpu.VMEM((n_rows, D_MODEL), BF16)],
        compiler_params=_params(1),
        name=name,
    )(*[_operand(a) for a in consts], w_up, w_down)


def _s0_attn(q4, ck, cv, kn, vn, sk_rows):
    n_seq, wb, _ = ck.shape
    n_t = kn.shape[1]
    rows = n_t * GROUP
    blk = lambda *tail: pl.BlockSpec((SAMPLE_BB,) + tail, lambda i: (i,) + (0,) * len(tail))
    return pl.pallas_call(
        _s0_attn_kernel,
        grid=(n_seq // SAMPLE_BB,),
        in_specs=[blk(N_KV * rows, HEAD_DIM), blk(wb, KV_DIM), blk(wb, KV_DIM), blk(n_t, KV_DIM),
                  blk(n_t, KV_DIM), pl.BlockSpec((N_KV, rows, 1), lambda i: (0, 0, 0))],
        out_specs=[blk(N_KV * rows, HEAD_DIM), blk(wb, KV_DIM), blk(wb, KV_DIM)],
        out_shape=[jax.ShapeDtypeStruct((n_seq, N_KV * rows, HEAD_DIM), F32),
                   jax.ShapeDtypeStruct((n_seq, wb, KV_DIM), F32),
                   jax.ShapeDtypeStruct((n_seq, wb, KV_DIM), F32)],
        scratch_shapes=[pltpu.VMEM((SAMPLE_BB, SAMPLE_KEYS, KV_DIM), F32),
                        pltpu.VMEM((SAMPLE_BB, SAMPLE_KEYS, KV_DIM), F32)],
        compiler_params=_params(1),
        name="s0_attn",
    )(q4, ck, cv, kn, vn, sk_rows)


def kernel(x_prompt, x_sample, state_conv, cache_k_win, cache_v_win, state_pool, meta_tokens, norm_mix, norm_mlp, w_in_even, conv_w, q_norm, k_norm, attn_sinks, w_out_even, w_pool, pool_scale, w_up, w_down):
    n_seq, n_t, _ = x_sample.shape
    wb = cache_k_win.shape[2]
    assert x_prompt.shape[1] % SEQ_TILE == 0 and x_prompt.shape[1] >= 2 * SEQ_TILE and n_seq % SAMPLE_BB == 0
    assert wb == WINDOW and wb + n_t <= SAMPLE_KEYS

    gmix = [_Stacked(norm_mix[:, None, :], l) for l in range(2)]
    gmlp = [_Stacked(norm_mlp[:, None, :], l) for l in range(2)]
    qg = jnp.tile(q_norm[0], N_HEADS)[None, :]
    kg = jnp.tile(k_norm[0], N_KV)[None, :]
    segq = jnp.kron(jnp.eye(N_HEADS, dtype=F32), jnp.full((HEAD_DIM, HEAD_DIM), 1.0 / HEAD_DIM, F32)).astype(BF16)
    segk = segq[:KV_DIM, :KV_DIM]
    cw = _Stacked(conv_w, 0)
    sinks = attn_sinks[0]
    meta_pad = jnp.pad(meta_tokens, ((META_PAD, 0), (0, 0)))
    sds = lambda *shape, dtype=F32: jax.ShapeDtypeStruct(shape, dtype)

    n_rows = n_t * n_seq
    ya, q4, kn_b, vn_b, conv_s, win = _single_step(
        _s0_pre_kernel, "s0_pre",
        [sds(n_rows, D_CONV), sds(n_seq, N_HEADS * n_t, HEAD_DIM), sds(n_seq, n_t, KV_DIM),
         sds(n_seq, n_t, KV_DIM), sds(n_seq, CONV_W - 1, D_CONV), sds(D_MODEL, D_IN_EVEN, dtype=BF16)],
        x_sample, _Stacked(state_conv, 0), gmix[0], _Stacked(w_in_even, 0), segq, segk, qg, kg, cw)
    sk_rows = jnp.tile(sinks.reshape(N_KV, 1, GROUP), (1, n_t, 1)).reshape(N_KV, n_t * GROUP, 1)
    o4, k_s, v_s = _s0_attn(q4, cache_k_win[0].reshape(n_seq, wb, KV_DIM),
                            cache_v_win[0].reshape(n_seq, wb, KV_DIM), kn_b, vn_b, sk_rows)
    xs2, wout, wup0, wdn0 = _streamed_mlp(
        _s0_post_kernel, "s0_post", n_rows, [sds(n_rows, D_MODEL), sds(D_MODEL, D_MODEL, dtype=BF16)],
        [x_sample, ya, o4, _Stacked(w_out_even, 0), gmlp[0]], w_up, w_down, 0)
    y_sample, pool_s, wpool, wup1, wdn1 = _streamed_mlp(
        _s1_kernel, "s1", n_rows,
        [sds(n_seq, n_t, D_MODEL), sds(n_seq, POOL_MAX - 1, D_MODEL), sds(*w_pool.shape[1:], dtype=BF16)],
        [xs2, _Stacked(state_pool, 0), gmix[1], _Stacked(w_pool, 0), pool_scale, gmlp[1]], w_up, w_down, 1)

    x2, x2_meta, conv_p, k_p, v_p = _l0_prompt(x_prompt, sinks, meta_pad, gmix[0], win, segq, segk, qg, kg, cw,
                                               wout, gmlp[0], wup0, wdn0)
    y_prompt, pool_p = _l1_prompt(x2, x2_meta, gmix[1], wpool, pool_scale, gmlp[1], wup1, wdn1)

    kv5 = lambda a: a.reshape(1, a.shape[0], WINDOW, N_KV, HEAD_DIM)
    return (y_prompt, y_sample, conv_p[None], conv_s[None], kv5(k_p), kv5(k_s), kv5(v_p), kv5(v_s),
            pool_p[None], pool_s[None])
```

```python
import functools
from typing import NamedTuple

import jax
import jax.numpy as jnp
from jax import lax
from jax.experimental import pallas as pl
from jax.experimental.pallas import tpu as pltpu

F32 = jnp.float32
BF16 = jnp.bfloat16

D_MODEL = 1024
D_CONV = 512
CONV_W = 3
HEAD_DIM = 64
N_HEADS = 8
N_KV = 2
GROUP = N_HEADS // N_KV
WINDOW = 128
BLOCK = 128
Q_DIM = N_HEADS * HEAD_DIM
KV_DIM = N_KV * HEAD_DIM
POOL_WINDOWS = (2, 4, 8, 16)
POOL_GROUP_DIM = D_MODEL // len(POOL_WINDOWS)
POOL_MAX = 16
D_FF = 4 * D_MODEL
D_IN_EVEN = 3 * D_CONV + Q_DIM + 2 * KV_DIM
N_META = 16
EPS = 1e-6
NEG = -1e30

META_PAD = BLOCK - N_META
SEQ_TILE = 512
FF_CHUNK = 1024
FF_STREAM = 512
CONV_HDR = 8
SAMPLE_BB = 16
SAMPLE_KEYS = 144
VMEM_LIMIT = 56 * 1024 * 1024
MLP_YIELDS = 2 * (D_FF // FF_CHUNK)
L0_MIX_YIELDS = 1 + 2 * N_KV * (SEQ_TILE // BLOCK)


def _dot(a, b):
    return jnp.dot(a, b, preferred_element_type=F32)


def _dot_nt(a, b):
    return lax.dot_general(a, b, (((1,), (1,)), ((), ())), preferred_element_type=F32)


def _dot_tn(a, b):
    return lax.dot_general(a, b, (((0,), (0,)), ((), ())), preferred_element_type=F32)


def _rms(x, g):
    ms = jnp.mean(x * x, axis=-1, keepdims=True)
    return x * lax.rsqrt(ms + EPS) * g


def _head_rms(x, seg, g):
    ms = _dot((x * x).astype(BF16), seg)
    return x * lax.rsqrt(ms + EPS) * g


def _interleave(*steppers, shares=None):
    shares = shares or [1] * len(steppers)
    results = [None] * len(steppers)
    done = [0] * len(steppers)
    live = set(range(len(steppers)))
    while live:
        i = min(live, key=lambda k: ((done[k] + 1) / shares[k], k))
        try:
            next(steppers[i])
            done[i] += 1
        except StopIteration as finished:
            results[i] = finished.value
            live.discard(i)
    return results


def _mlp_steps(x, g, wup_ref, wdn_ref):
    xn = _rms(x, g).astype(BF16)
    acc = x
    for c in range(D_FF // FF_CHUNK):
        cols = slice(c * FF_CHUNK, (c + 1) * FF_CHUNK)
        h = _dot(xn, wup_ref[:, cols])
        a = jnp.square(jnp.maximum(h, 0.0)).astype(BF16)
        yield
        acc = acc + _dot(a, wdn_ref[cols, :])
        yield
    return acc


def _mlp(x, g, wup_ref, wdn_ref):
    return _interleave(_mlp_steps(x, g, wup_ref, wdn_ref))[0]


def _in_proj(x, g, win_ref, segq_ref, segk_ref, qg, kg):
    hn = _rms(x, g).astype(BF16)
    kv_col = 3 * D_CONV + Q_DIM
    half = x.shape[0] // 2
    z = _dot(hn, win_ref[:, 0:kv_col])
    kv = jnp.concatenate([_dot(hn[0:half], win_ref[:, kv_col:]), _dot(hn[half:], win_ref[:, kv_col:])], axis=0)
    xa = z[:, 0:D_CONV]
    gc = z[:, D_CONV:2 * D_CONV]
    gb = z[:, 2 * D_CONV:3 * D_CONV]
    q = z[:, 3 * D_CONV:]
    k = kv[:, 0:KV_DIM]
    v = kv[:, KV_DIM:]
    qn = _head_rms(q, segq_ref[...], qg) * (HEAD_DIM ** -0.5)
    kn = _head_rms(k, segk_ref[...], kg)
    return gc * xa, gb, qn, kn, v


def _l0_mixer(x, rows, sinks_ref, gmix, win_ref, segq_ref, segk_ref, qg, kg, cw_ref, wout_ref,
              u_scr, k_scr, v_scr, first_scr, band_scr):
    u, gb, qn, kn, v = _in_proj(x, gmix, win_ref, segq_ref, segk_ref, qg, kg)

    u_scr[CONV_HDR:CONV_HDR + rows, :] = u
    u1 = u_scr[CONV_HDR - 1:CONV_HDR - 1 + rows, :]
    u2 = u_scr[CONV_HDR - 2:CONV_HDR - 2 + rows, :]
    ya = gb * (u2 * cw_ref[0:1, :] + u1 * cw_ref[1:2, :] + u * cw_ref[2:3, :])
    k_scr[BLOCK:BLOCK + rows, :] = kn
    v_scr[BLOCK:BLOCK + rows, :] = v
    yield

    yb_blocks = []
    for i in range(rows // BLOCK):
        qb = qn[i * BLOCK:(i + 1) * BLOCK, :]
        bias = first_scr[...] if i == 0 else band_scr[...]
        heads_t = []
        for j in range(N_KV):
            lanes = slice(j * HEAD_DIM, (j + 1) * HEAD_DIM)
            k2 = k_scr[i * BLOCK:(i + 2) * BLOCK, lanes].astype(BF16)
            v2 = v_scr[i * BLOCK:(i + 2) * BLOCK, lanes].astype(BF16)
            qs = jnp.concatenate(
                [qb[:, (j * GROUP + g) * HEAD_DIM:(j * GROUP + g + 1) * HEAD_DIM] for g in range(GROUP)],
                axis=0).astype(BF16)
            sk = jnp.concatenate(
                [jnp.full((1, BLOCK), sinks_ref[j * GROUP + g], F32) for g in range(GROUP)], axis=1)
            st = _dot_nt(k2, qs) + bias
            m = jnp.maximum(jnp.max(st, axis=0, keepdims=True), sk)
            p = jnp.exp(st - m)
            den = jnp.sum(p, axis=0, keepdims=True) + jnp.exp(sk - m)
            yield
            ot = _dot_tn(v2, p.astype(BF16)) / den
            heads_t += [ot[:, g * BLOCK:(g + 1) * BLOCK] for g in range(GROUP)]
            yield
        yb_blocks.append(jnp.concatenate(heads_t, axis=0).T)
    yb = yb_blocks[0] if len(yb_blocks) == 1 else jnp.concatenate(yb_blocks, axis=0)

    u_scr[CONV_HDR - 2:CONV_HDR, :] = u_scr[CONV_HDR - 2 + rows:CONV_HDR + rows, :]
    k_scr[0:BLOCK, :] = k_scr[rows:rows + BLOCK, :]
    v_scr[0:BLOCK, :] = v_scr[rows:rows + BLOCK, :]

    mix = jnp.concatenate([ya, yb], axis=-1).astype(BF16)
    return x + _dot(mix, wout_ref[...])


def _pipeline_steps(s, x_ref, y_ref, x1_scr, mixer, mlp):
    slot = lax.rem(s, 2)

    def mix_tile():
        x1_scr[slot] = yield from mixer(x_ref[...], SEQ_TILE)

    def mlp_tile():
        y_ref[...] = yield from mlp(x1_scr[1 - slot])

    return mlp_tile, mix_tile


def _l0_prompt_kernel(sinks_ref, x_ref, meta_ref, gmix_ref, win_ref, segq_ref, segk_ref, qg_ref, kg_ref,
                      cw_ref, wout_ref, gmlp_ref, wup_ref, wdn_ref,
                      y_ref, ymeta_ref, conv_ref, kout_ref, vout_ref,
                      u_scr, k_scr, v_scr, first_scr, band_scr, x1_scr):
    s = pl.program_id(1)
    last = pl.num_programs(1) - 1
    mixer = functools.partial(
        _l0_mixer, sinks_ref=sinks_ref, gmix=gmix_ref[...], win_ref=win_ref, segq_ref=segq_ref,
        segk_ref=segk_ref, qg=qg_ref[...], kg=kg_ref[...], cw_ref=cw_ref, wout_ref=wout_ref,
        u_scr=u_scr, k_scr=k_scr, v_scr=v_scr, first_scr=first_scr, band_scr=band_scr)
    mlp = functools.partial(_mlp_steps, g=gmlp_ref[...], wup_ref=wup_ref, wdn_ref=wdn_ref)
    mlp_tile, mix_tile = _pipeline_steps(s, x_ref, y_ref, x1_scr, mixer, mlp)

    def band_bias(first_valid_key):
        c = lax.broadcasted_iota(jnp.int32, (2 * BLOCK, GROUP * BLOCK), 0)
        q = lax.broadcasted_iota(jnp.int32, (2 * BLOCK, GROUP * BLOCK), 1) & (BLOCK - 1)
        return jnp.where((c >= q) & (c <= q + WINDOW) & (c >= first_valid_key), 0.0, NEG)

    @pl.when(s == 0)
    def _start():
        band_scr[...] = band_bias(0)
        u_scr[0:CONV_HDR, :] = jnp.zeros((CONV_HDR, D_CONV), F32)
        k_scr[0:BLOCK, :] = jnp.zeros((BLOCK, KV_DIM), F32)
        v_scr[0:BLOCK, :] = jnp.zeros((BLOCK, KV_DIM), F32)
        first_scr[...] = band_bias(BLOCK + META_PAD)
        x1_meta, = _interleave(mixer(meta_ref[...], BLOCK))
        ymeta_ref[...], = _interleave(mlp(x1_meta))
        first_scr[...] = band_bias(META_PAD)
        _interleave(mix_tile())
        first_scr[...] = band_scr[...]

    @pl.when((s > 0) & (s < last))
    def _steady():
        _interleave(mlp_tile(), mix_tile(), shares=(MLP_YIELDS, L0_MIX_YIELDS))

    @pl.when(s == last)
    def _drain():
        _interleave(mlp_tile())

    @pl.when(s == last - 1)
    def _state():
        conv_ref[...] = u_scr[CONV_HDR - 2:CONV_HDR, :]
        kout_ref[...] = k_scr[0:BLOCK, :]
        vout_ref[...] = v_scr[0:BLOCK, :]


def _l1_mixer(x, rows, is_meta, gmix, wpool_ref, pscale, h_scr):
    h = _rms(x, gmix)
    h_scr[POOL_MAX:POOL_MAX + rows, :] = h
    ys = []
    for gi, w in enumerate(POOL_WINDOWS):
        cols = slice(gi * POOL_GROUP_DIM, (gi + 1) * POOL_GROUP_DIM)
        hg = h[:, cols]
        acc = hg
        for j in range(1, w):
            acc = acc + h_scr[POOL_MAX - j:POOL_MAX - j + rows, cols]
        if is_meta:
            r = lax.broadcasted_iota(jnp.int32, (rows, 1), 0)
            pooled = acc / jnp.clip(r - (META_PAD - 1), 1, w).astype(F32)
        else:
            pooled = acc * (1.0 / w)
        ys.append(_dot((pooled - hg).astype(BF16), wpool_ref[gi]))
        yield
    h_scr[0:POOL_MAX, :] = h_scr[rows:rows + POOL_MAX, :]
    return x + jnp.concatenate(ys, axis=-1) * pscale


def _l1_prompt_kernel(x_ref, meta_ref, gmix_ref, wpool_ref, pscale_ref, gmlp_ref, wup_ref, wdn_ref,
                      y_ref, pool_ref, h_scr, x1_scr):
    s = pl.program_id(1)
    last = pl.num_programs(1) - 1
    mixer = functools.partial(_l1_mixer, gmix=gmix_ref[...], wpool_ref=wpool_ref, pscale=pscale_ref[...],
                              h_scr=h_scr)
    mlp = functools.partial(_mlp_steps, g=gmlp_ref[...], wup_ref=wup_ref, wdn_ref=wdn_ref)
    mlp_tile, mix_tile = _pipeline_steps(s, x_ref, y_ref, x1_scr,
                                         functools.partial(mixer, is_meta=False), mlp)

    @pl.when(s == 0)
    def _start():
        h_scr[0:POOL_MAX, :] = jnp.zeros((POOL_MAX, D_MODEL), F32)
        _interleave(mixer(meta_ref[...], BLOCK, True))
        _interleave(mix_tile())

    @pl.when((s > 0) & (s < last))
    def _steady():
        _interleave(mlp_tile(), mix_tile(), shares=(MLP_YIELDS, len(POOL_WINDOWS)))

    @pl.when(s == last)
    def _drain():
        _interleave(mlp_tile())

    @pl.when(s == last - 1)
    def _state():
        pool_ref[...] = h_scr[1:POOL_MAX, :]


def _time_major(ref):
    return jnp.concatenate([ref[:, t, :] for t in range(ref.shape[1])], axis=0)


def _s0_pre_kernel(x_ref, st_ref, gmix_ref, win_ref, segq_ref, segk_ref, qg_ref, kg_ref, cw_ref,
                   ya_ref, q_ref, k_ref, v_ref, conv_ref, win_bf_ref):
    n_seq, n_t, _ = x_ref.shape
    win_bf_ref[...] = win_ref[...].astype(BF16)
    u, gb, qn, kn, v = _in_proj(_time_major(x_ref), gmix_ref[...], win_bf_ref, segq_ref, segk_ref, qg_ref[...],
                                kg_ref[...])
    rows = lambda a, t: a[t * n_seq:(t + 1) * n_seq, :]
    ue = [st_ref[:, i, :] for i in range(CONV_W - 1)] + [rows(u, t) for t in range(n_t)]
    for t in range(n_t):
        y = ue[t] * cw_ref[0:1, :] + ue[t + 1] * cw_ref[1:2, :] + ue[t + 2] * cw_ref[2:3, :]
        ya_ref[t * n_seq:(t + 1) * n_seq, :] = rows(gb, t) * y
        k_ref[:, t, :] = rows(kn, t)
        v_ref[:, t, :] = rows(v, t)
        for h in range(N_HEADS):
            q_ref[:, (h // GROUP) * n_t * GROUP + t * GROUP + h % GROUP, :] = (
                rows(qn, t)[:, h * HEAD_DIM:(h + 1) * HEAD_DIM])
    for i in range(CONV_W - 1):
        conv_ref[:, i, :] = ue[n_t + i]


def _s0_attn_kernel(q_ref, ck_ref, cv_ref, kn_ref, vn_ref, sk_ref, o_ref, kout_ref, vout_ref, ke_scr, ve_scr):
    bb, n_t = kn_ref.shape[0], kn_ref.shape[1]
    wb = ck_ref.shape[1]
    pad = SAMPLE_KEYS - wb - n_t
    for src, new, scr, out in ((ck_ref, kn_ref, ke_scr, kout_ref), (cv_ref, vn_ref, ve_scr, vout_ref)):
        scr[:, 0:wb, :] = src[...]
        scr[:, wb:wb + n_t, :] = new[...]
        scr[:, wb + n_t:, :] = jnp.zeros((bb, pad, KV_DIM), F32)
        out[...] = scr[:, n_t:n_t + wb, :]
    rows = n_t * GROUP
    t = lax.broadcasted_iota(jnp.int32, (rows, SAMPLE_KEYS), 0) // GROUP
    c = lax.broadcasted_iota(jnp.int32, (rows, SAMPLE_KEYS), 1)
    dist = wb + t - c
    bias = jnp.where((dist >= 0) & (dist <= WINDOW), 0.0, NEG)[None]
    for j in range(N_KV):
        lanes = slice(j * HEAD_DIM, (j + 1) * HEAD_DIM)
        q = q_ref[:, j * rows:(j + 1) * rows, :].astype(BF16)
        k = ke_scr[:, :, lanes].astype(BF16)
        v = ve_scr[:, :, lanes].astype(BF16)
        sk = sk_ref[j][None]
        s = jnp.einsum('bqd,bkd->bqk', q, k, preferred_element_type=F32) + bias
        m = jnp.maximum(jnp.max(s, axis=-1, keepdims=True), sk)
        p = jnp.exp(s - m)
        den = jnp.sum(p, axis=-1, keepdims=True) + jnp.exp(sk - m)
        o = jnp.einsum('bqk,bkd->bqd', p.astype(BF16), v, preferred_element_type=F32)
        o_ref[:, j * rows:(j + 1) * rows, :] = o / den


def _mlp_stream_step(c, wup_ref, wdn_ref, y_ref, wup_bf_ref, wdn_bf_ref, x1_scr, xn_scr):
    wup_bf_ref[...] = wup_ref[...].astype(BF16)
    wdn_bf_ref[...] = wdn_ref[...].astype(BF16)
    a = jnp.square(jnp.maximum(_dot(xn_scr[...], wup_bf_ref[...]), 0.0)).astype(BF16)
    x1_scr[...] += _dot(a, wdn_bf_ref[...])
    if y_ref is not None:
        @pl.when(c == pl.num_programs(0) - 1)
        def _done():
            y_ref[...] = x1_scr[...]


def _s0_post_kernel(x_ref, ya_ref, o_ref, wout_ref, gmlp_ref, wup_ref, wdn_ref,
                    y_ref, wout_bf_ref, wup_bf_ref, wdn_bf_ref, x1_scr, xn_scr):
    c = pl.program_id(0)

    @pl.when(c == 0)
    def _mix():
        n_t = x_ref.shape[1]
        wout_bf_ref[...] = wout_ref[...].astype(BF16)
        yb = jnp.concatenate(
            [jnp.concatenate([o_ref[:, (h // GROUP) * n_t * GROUP + t * GROUP + h % GROUP, :]
                              for h in range(N_HEADS)], axis=-1) for t in range(n_t)], axis=0)
        mix = jnp.concatenate([ya_ref[...], yb], axis=-1).astype(BF16)
        x1 = _time_major(x_ref) + _dot(mix, wout_bf_ref[...])
        x1_scr[...] = x1
        xn_scr[...] = _rms(x1, gmlp_ref[...]).astype(BF16)

    _mlp_stream_step(c, wup_ref, wdn_ref, y_ref, wup_bf_ref, wdn_bf_ref, x1_scr, xn_scr)


def _s1_kernel(x_ref, buf_ref, gmix_ref, wpool_ref, pscale_ref, gmlp_ref, wup_ref, wdn_ref,
               y_ref, pool_ref, wpool_bf_ref, wup_bf_ref, wdn_bf_ref, x1_scr, xn_scr):
    c = pl.program_id(0)
    n_buf, n_seq, _ = buf_ref.shape
    n_t = x_ref.shape[0] // n_seq

    @pl.when(c == 0)
    def _mix():
        wpool_bf_ref[...] = wpool_ref[...].astype(BF16)
        x = x_ref[...]
        h = _rms(x, gmix_ref[...])
        he = [buf_ref[i] for i in range(n_buf)] + [h[t * n_seq:(t + 1) * n_seq, :] for t in range(n_t)]
        for i in range(n_buf):
            pool_ref[i] = he[n_t + i]
        ys = []
        for gi, w in enumerate(POOL_WINDOWS):
            cols = slice(gi * POOL_GROUP_DIM, (gi + 1) * POOL_GROUP_DIM)
            dl = []
            for t in range(n_t):
                acc = he[n_buf + t][:, cols]
                for j in range(1, w):
                    acc = acc + he[n_buf + t - j][:, cols]
                dl.append(acc * (1.0 / w) - he[n_buf + t][:, cols])
            ys.append(_dot(jnp.concatenate(dl, axis=0).astype(BF16), wpool_bf_ref[gi]))
        x1 = x + jnp.concatenate(ys, axis=-1) * pscale_ref[...]
        x1_scr[...] = x1
        xn_scr[...] = _rms(x1, gmlp_ref[...]).astype(BF16)

    _mlp_stream_step(c, wup_ref, wdn_ref, None, wup_bf_ref, wdn_bf_ref, x1_scr, xn_scr)

    @pl.when(c == pl.num_programs(0) - 1)
    def _done():
        for t in range(n_t):
            y_ref[:, t, :] = x1_scr[t * n_seq:(t + 1) * n_seq, :]


class _Stacked(NamedTuple):
    array: jax.Array
    layer: int


def _operand(arg):
    return arg.array if isinstance(arg, _Stacked) else arg


def _resident(arg, n_grid):
    if isinstance(arg, _Stacked):
        shape, idx = (None,) + arg.array.shape[1:], (arg.layer,) + (0,) * (arg.array.ndim - 1)
    else:
        shape, idx = arg.shape, (0,) * arg.ndim
    index_map = {1: lambda i: idx, 2: lambda i, j: idx}[n_grid]
    return pl.BlockSpec(shape, index_map, pipeline_mode=pl.Buffered(1))


def _params(n_grid):
    return pltpu.CompilerParams(dimension_semantics=("arbitrary",) * n_grid, vmem_limit_bytes=VMEM_LIMIT)


def _prompt_specs(n_tiles):
    tile_in = pl.BlockSpec((None, SEQ_TILE, D_MODEL), lambda b, s: (b, jnp.minimum(s, n_tiles - 1), 0))
    tile_out = pl.BlockSpec((None, SEQ_TILE, D_MODEL), lambda b, s: (b, jnp.maximum(s - 1, 0), 0))
    per_b = lambda r, c: pl.BlockSpec((None, r, c), lambda b, s: (b, 0, 0))
    return tile_in, tile_out, per_b


def _l0_prompt(x, sinks, *resident):
    n_b, seq, _ = x.shape
    n_tiles = seq // SEQ_TILE
    tile_in, tile_out, per_b = _prompt_specs(n_tiles)
    return pl.pallas_call(
        _l0_prompt_kernel,
        grid=(n_b, n_tiles + 1),
        in_specs=[pl.BlockSpec(memory_space=pltpu.SMEM), tile_in] + [_resident(a, 2) for a in resident],
        out_specs=[tile_out, per_b(BLOCK, D_MODEL), per_b(CONV_W - 1, D_CONV), per_b(WINDOW, KV_DIM),
                   per_b(WINDOW, KV_DIM)],
        out_shape=[jax.ShapeDtypeStruct((n_b, seq, D_MODEL), F32),
                   jax.ShapeDtypeStruct((n_b, BLOCK, D_MODEL), F32),
                   jax.ShapeDtypeStruct((n_b, CONV_W - 1, D_CONV), F32),
                   jax.ShapeDtypeStruct((n_b, WINDOW, KV_DIM), F32),
                   jax.ShapeDtypeStruct((n_b, WINDOW, KV_DIM), F32)],
        scratch_shapes=[pltpu.VMEM((CONV_HDR + SEQ_TILE, D_CONV), F32),
                        pltpu.VMEM((BLOCK + SEQ_TILE, KV_DIM), F32),
                        pltpu.VMEM((BLOCK + SEQ_TILE, KV_DIM), F32),
                        pltpu.VMEM((2 * BLOCK, GROUP * BLOCK), F32),
                        pltpu.VMEM((2 * BLOCK, GROUP * BLOCK), F32),
                        pltpu.VMEM((2, SEQ_TILE, D_MODEL), F32)],
        compiler_params=_params(2),
        name="l0_prompt",
    )(sinks, x, *[_operand(a) for a in resident])


def _l1_prompt(x, xmeta, *resident):
    n_b, seq, _ = x.shape
    n_tiles = seq // SEQ_TILE
    tile_in, tile_out, per_b = _prompt_specs(n_tiles)
    return pl.pallas_call(
        _l1_prompt_kernel,
        grid=(n_b, n_tiles + 1),
        in_specs=[tile_in, per_b(BLOCK, D_MODEL)] + [_resident(a, 2) for a in resident],
        out_specs=[tile_out, per_b(POOL_MAX - 1, D_MODEL)],
        out_shape=[jax.ShapeDtypeStruct((n_b, seq, D_MODEL), F32),
                   jax.ShapeDtypeStruct((n_b, POOL_MAX - 1, D_MODEL), F32)],
        scratch_shapes=[pltpu.VMEM((POOL_MAX + SEQ_TILE, D_MODEL), F32),
                        pltpu.VMEM((2, SEQ_TILE, D_MODEL), F32)],
        compiler_params=_params(2),
        name="l1_prompt",
    )(x, xmeta, *[_operand(a) for a in resident])


def _single_step(kernel, name, out_shape, *args):
    return pl.pallas_call(
        kernel,
        grid=(1,),
        in_specs=[_resident(a, 1) for a in args],
        out_specs=[pl.BlockSpec(o.shape, lambda i, n=len(o.shape): (0,) * n) for o in out_shape],
        out_shape=out_shape,
        compiler_params=_params(1),
        name=name,
    )(*[_operand(a) for a in args])


def _streamed_mlp(kernel, name, n_rows, outs, consts, w_up, w_down, layer):
    const = lambda shape: pl.BlockSpec(shape, lambda c, n=len(shape): (0,) * n)
    return pl.pallas_call(
        kernel,
        grid=(D_FF // FF_STREAM,),
        in_specs=[_resident(a, 1) for a in consts]
        + [pl.BlockSpec((None, D_MODEL, FF_STREAM), lambda c: (layer, 0, c)),
           pl.BlockSpec((None, FF_STREAM, D_MODEL), lambda c: (layer, c, 0))],
        out_specs=[const(o.shape) for o in outs]
        + [pl.BlockSpec((D_MODEL, FF_STREAM), lambda c: (0, c)),
           pl.BlockSpec((FF_STREAM, D_MODEL), lambda c: (c, 0))],
        out_shape=list(outs) + [jax.ShapeDtypeStruct((D_MODEL, D_FF), BF16),
                                jax.ShapeDtypeStruct((D_FF, D_MODEL), BF16)],
        scratch_shapes=[pltpu.VMEM((n_rows, D_MODEL), F32), pltpu.VMEM((n_rows, D_MODEL), BF16)],
        compiler_params=_params(1),
        name=name,
    )(*[_operand(a) for a in consts], w_up, w_down)


def _s0_attn(q4, ck, cv, kn, vn, sk_rows):
    n_seq, wb, _ = ck.shape
    n_t = kn.shape[1]
    rows = n_t * GROUP
    blk = lambda *tail: pl.BlockSpec((SAMPLE_BB,) + tail, lambda i: (i,) + (0,) * len(tail))
    return pl.pallas_call(
        _s0_attn_kernel,
        grid=(n_seq // SAMPLE_BB,),
        in_specs=[blk(N_KV * rows, HEAD_DIM), blk(wb, KV_DIM), blk(wb, KV_DIM), blk(n_t, KV_DIM),
                  blk(n_t, KV_DIM), pl.BlockSpec((N_KV, rows, 1), lambda i: (0, 0, 0))],
        out_specs=[blk(N_KV * rows, HEAD_DIM), blk(wb, KV_DIM), blk(wb, KV_DIM)],
        out_shape=[jax.ShapeDtypeStruct((n_seq, N_KV * rows, HEAD_DIM), F32),
                   jax.ShapeDtypeStruct((n_seq, wb, KV_DIM), F32),
                   jax.ShapeDtypeStruct((n_seq, wb, KV_DIM), F32)],
        scratch_shapes=[pltpu.VMEM((SAMPLE_BB, SAMPLE_KEYS, KV_DIM), F32),
                        pltpu.VMEM((SAMPLE_BB, SAMPLE_KEYS, KV_DIM), F32)],
        compiler_params=_params(1),
        name="s0_attn",
    )(q4, ck, cv, kn, vn, sk_rows)


def kernel(x_prompt, x_sample, state_conv, cache_k_win, cache_v_win, state_pool, meta_tokens, norm_mix, norm_mlp, w_in_even, conv_w, q_norm, k_norm, attn_sinks, w_out_even, w_pool, pool_scale, w_up, w_down):
    n_seq, n_t, _ = x_sample.shape
    wb = cache_k_win.shape[2]
    assert x_prompt.shape[1] % SEQ_TILE == 0 and x_prompt.shape[1] >= 2 * SEQ_TILE and n_seq % SAMPLE_BB == 0
    assert wb == WINDOW and wb + n_t <= SAMPLE_KEYS

    gmix = [_Stacked(norm_mix[:, None, :], l) for l in range(2)]
    gmlp = [_Stacked(norm_mlp[:, None, :], l) for l in range(2)]
    qg = jnp.tile(q_norm[0], N_HEADS)[None, :]
    kg = jnp.tile(k_norm[0], N_KV)[None, :]
    segq = jnp.kron(jnp.eye(N_HEADS, dtype=F32), jnp.full((HEAD_DIM, HEAD_DIM), 1.0 / HEAD_DIM, F32)).astype(BF16)
    segk = segq[:KV_DIM, :KV_DIM]
    cw = _Stacked(conv_w, 0)
    sinks = attn_sinks[0]
    meta_pad = jnp.pad(meta_tokens, ((META_PAD, 0), (0, 0)))
    sds = lambda *shape, dtype=F32: jax.ShapeDtypeStruct(shape, dtype)

    n_rows = n_t * n_seq
    ya, q4, kn_b, vn_b, conv_s, win = _single_step(
        _s0_pre_kernel, "s0_pre",
        [sds(n_rows, D_CONV), sds(n_seq, N_HEADS * n_t, HEAD_DIM), sds(n_seq, n_t, KV_DIM),
         sds(n_seq, n_t, KV_DIM), sds(n_seq, CONV_W - 1, D_CONV), sds(D_MODEL, D_IN_EVEN, dtype=BF16)],
        x_sample, _Stacked(state_conv, 0), gmix[0], _Stacked(w_in_even, 0), segq, segk, qg, kg, cw)
    sk_rows = jnp.tile(sinks.reshape(N_KV, 1, GROUP), (1, n_t, 1)).reshape(N_KV, n_t * GROUP, 1)
    o4, k_s, v_s = _s0_attn(q4, cache_k_win[0].reshape(n_seq, wb, KV_DIM),
                            cache_v_win[0].reshape(n_seq, wb, KV_DIM), kn_b, vn_b, sk_rows)
    xs2, wout, wup0, wdn0 = _streamed_mlp(
        _s0_post_kernel, "s0_post", n_rows, [sds(n_rows, D_MODEL), sds(D_MODEL, D_MODEL, dtype=BF16)],
        [x_sample, ya, o4, _Stacked(w_out_even, 0), gmlp[0]], w_up, w_down, 0)
    buf_t = jnp.transpose(state_pool[0], (1, 0, 2))
    y_sample, pool_s, wpool, wup1, wdn1 = _streamed_mlp(
        _s1_kernel, "s1", n_rows,
        [sds(n_seq, n_t, D_MODEL), sds(POOL_MAX - 1, n_seq, D_MODEL), sds(*w_pool.shape[1:], dtype=BF16)],
        [xs2, buf_t, gmix[1], _Stacked(w_pool, 0), pool_scale, gmlp[1]], w_up, w_down, 1)

    x2, x2_meta, conv_p, k_p, v_p = _l0_prompt(x_prompt, sinks, meta_pad, gmix[0], win, segq, segk, qg, kg, cw,
                                               wout, gmlp[0], wup0, wdn0)
    y_prompt, pool_p = _l1_prompt(x2, x2_meta, gmix[1], wpool, pool_scale, gmlp[1], wup1, wdn1)

    kv5 = lambda a: a.reshape(1, a.shape[0], WINDOW, N_KV, HEAD_DIM)
    return (y_prompt, y_sample, conv_p[None], conv_s[None], kv5(k_p), kv5(k_s), kv5(v_p), kv5(v_s),
            pool_p[None], jnp.transpose(pool_s, (1, 0, 2))[None])
```

```python
import functools
from typing import NamedTuple

import jax
import jax.numpy as jnp
from jax import lax
from jax.experimental import pallas as pl
from jax.experimental.pallas import tpu as pltpu

F32 = jnp.float32
BF16 = jnp.bfloat16

D_MODEL = 1024
D_CONV = 512
CONV_W = 3
HEAD_DIM = 64
N_HEADS = 8
N_KV = 2
GROUP = N_HEADS // N_KV
WINDOW = 128
BLOCK = 128
Q_DIM = N_HEADS * HEAD_DIM
KV_DIM = N_KV * HEAD_DIM
POOL_WINDOWS = (2, 4, 8, 16)
POOL_GROUP_DIM = D_MODEL // len(POOL_WINDOWS)
POOL_MAX = 16
D_FF = 4 * D_MODEL
D_IN_EVEN = 3 * D_CONV + Q_DIM + 2 * KV_DIM
N_META = 16
EPS = 1e-6
NEG = -1e30

META_PAD = BLOCK - N_META
SEQ_TILE = 512
FF_CHUNK = 1024
FF_STREAM = 512
CONV_HDR = 8
SAMPLE_BB = 16
SAMPLE_NEW = 16
VMEM_LIMIT = 56 * 1024 * 1024
MLP_YIELDS = 2 * (D_FF // FF_CHUNK)
L0_MIX_YIELDS = 1 + 2 * N_KV * (SEQ_TILE // BLOCK)


def _dot(a, b):
    return jnp.dot(a, b, preferred_element_type=F32)


def _dot_nt(a, b):
    return lax.dot_general(a, b, (((1,), (1,)), ((), ())), preferred_element_type=F32)


def _dot_tn(a, b):
    return lax.dot_general(a, b, (((0,), (0,)), ((), ())), preferred_element_type=F32)


def _rms(x, g):
    ms = jnp.mean(x * x, axis=-1, keepdims=True)
    return x * lax.rsqrt(ms + EPS) * g


def _head_rms(x, seg, g):
    ms = _dot((x * x).astype(BF16), seg)
    return x * lax.rsqrt(ms + EPS) * g


def _interleave(*steppers, shares=None):
    shares = shares or [1] * len(steppers)
    results = [None] * len(steppers)
    done = [0] * len(steppers)
    live = set(range(len(steppers)))
    while live:
        i = min(live, key=lambda k: ((done[k] + 1) / shares[k], k))
        try:
            next(steppers[i])
            done[i] += 1
        except StopIteration as finished:
            results[i] = finished.value
            live.discard(i)
    return results


def _mlp_steps(x, g, wup_ref, wdn_ref):
    xn = _rms(x, g).astype(BF16)
    acc = x
    for c in range(D_FF // FF_CHUNK):
        cols = slice(c * FF_CHUNK, (c + 1) * FF_CHUNK)
        h = _dot(xn, wup_ref[:, cols])
        a = jnp.square(jnp.maximum(h, 0.0)).astype(BF16)
        yield
        acc = acc + _dot(a, wdn_ref[cols, :])
        yield
    return acc


def _mlp(x, g, wup_ref, wdn_ref):
    return _interleave(_mlp_steps(x, g, wup_ref, wdn_ref))[0]


def _in_proj(x, g, win_ref, segq_ref, segk_ref, qg, kg):
    hn = _rms(x, g).astype(BF16)
    kv_col = 3 * D_CONV + Q_DIM
    half = x.shape[0] // 2
    z = _dot(hn, win_ref[:, 0:kv_col])
    kv = jnp.concatenate([_dot(hn[0:half], win_ref[:, kv_col:]), _dot(hn[half:], win_ref[:, kv_col:])], axis=0)
    xa = z[:, 0:D_CONV]
    gc = z[:, D_CONV:2 * D_CONV]
    gb = z[:, 2 * D_CONV:3 * D_CONV]
    q = z[:, 3 * D_CONV:]
    k = kv[:, 0:KV_DIM]
    v = kv[:, KV_DIM:]
    qn = _head_rms(q, segq_ref[...], qg) * (HEAD_DIM ** -0.5)
    kn = _head_rms(k, segk_ref[...], kg)
    return gc * xa, gb, qn, kn, v


def _l0_mixer(x, rows, sinks_ref, gmix, win_ref, segq_ref, segk_ref, qg, kg, cw_ref, wout_ref,
              u_scr, k_scr, v_scr, first_scr, band_scr):
    u, gb, qn, kn, v = _in_proj(x, gmix, win_ref, segq_ref, segk_ref, qg, kg)

    u_scr[CONV_HDR:CONV_HDR + rows, :] = u
    u1 = u_scr[CONV_HDR - 1:CONV_HDR - 1 + rows, :]
    u2 = u_scr[CONV_HDR - 2:CONV_HDR - 2 + rows, :]
    ya = gb * (u2 * cw_ref[0:1, :] + u1 * cw_ref[1:2, :] + u * cw_ref[2:3, :])
    k_scr[BLOCK:BLOCK + rows, :] = kn
    v_scr[BLOCK:BLOCK + rows, :] = v
    yield

    yb_blocks = []
    for i in range(rows // BLOCK):
        qb = qn[i * BLOCK:(i + 1) * BLOCK, :]
        bias = first_scr[...] if i == 0 else band_scr[...]
        heads_t = []
        for j in range(N_KV):
            lanes = slice(j * HEAD_DIM, (j + 1) * HEAD_DIM)
            k2 = k_scr[i * BLOCK:(i + 2) * BLOCK, lanes].astype(BF16)
            v2 = v_scr[i * BLOCK:(i + 2) * BLOCK, lanes].astype(BF16)
            qs = jnp.concatenate(
                [qb[:, (j * GROUP + g) * HEAD_DIM:(j * GROUP + g + 1) * HEAD_DIM] for g in range(GROUP)],
                axis=0).astype(BF16)
            sk = jnp.concatenate(
                [jnp.full((1, BLOCK), sinks_ref[j * GROUP + g], F32) for g in range(GROUP)], axis=1)
            st = _dot_nt(k2, qs) + bias
            m = jnp.maximum(jnp.max(st, axis=0, keepdims=True), sk)
            p = jnp.exp(st - m)
            den = jnp.sum(p, axis=0, keepdims=True) + jnp.exp(sk - m)
            yield
            ot = _dot_tn(v2, p.astype(BF16)) / den
            heads_t += [ot[:, g * BLOCK:(g + 1) * BLOCK] for g in range(GROUP)]
            yield
        yb_blocks.append(jnp.concatenate(heads_t, axis=0).T)
    yb = yb_blocks[0] if len(yb_blocks) == 1 else jnp.concatenate(yb_blocks, axis=0)

    u_scr[CONV_HDR - 2:CONV_HDR, :] = u_scr[CONV_HDR - 2 + rows:CONV_HDR + rows, :]
    k_scr[0:BLOCK, :] = k_scr[rows:rows + BLOCK, :]
    v_scr[0:BLOCK, :] = v_scr[rows:rows + BLOCK, :]

    mix = jnp.concatenate([ya, yb], axis=-1).astype(BF16)
    return x + _dot(mix, wout_ref[...])


def _pipeline_steps(s, x_ref, y_ref, x1_scr, mixer, mlp):
    slot = lax.rem(s, 2)

    def mix_tile():
        x1_scr[slot] = yield from mixer(x_ref[...], SEQ_TILE)

    def mlp_tile():
        y_ref[...] = yield from mlp(x1_scr[1 - slot])

    return mlp_tile, mix_tile


def _l0_prompt_kernel(sinks_ref, x_ref, meta_ref, gmix_ref, win_ref, segq_ref, segk_ref, qg_ref, kg_ref,
                      cw_ref, wout_ref, gmlp_ref, wup_ref, wdn_ref,
                      y_ref, ymeta_ref, conv_ref, kout_ref, vout_ref,
                      u_scr, k_scr, v_scr, first_scr, band_scr, x1_scr):
    s = pl.program_id(1)
    last = pl.num_programs(1) - 1
    mixer = functools.partial(
        _l0_mixer, sinks_ref=sinks_ref, gmix=gmix_ref[...], win_ref=win_ref, segq_ref=segq_ref,
        segk_ref=segk_ref, qg=qg_ref[...], kg=kg_ref[...], cw_ref=cw_ref, wout_ref=wout_ref,
        u_scr=u_scr, k_scr=k_scr, v_scr=v_scr, first_scr=first_scr, band_scr=band_scr)
    mlp = functools.partial(_mlp_steps, g=gmlp_ref[...], wup_ref=wup_ref, wdn_ref=wdn_ref)
    mlp_tile, mix_tile = _pipeline_steps(s, x_ref, y_ref, x1_scr, mixer, mlp)

    def band_bias(first_valid_key):
        c = lax.broadcasted_iota(jnp.int32, (2 * BLOCK, GROUP * BLOCK), 0)
        q = lax.broadcasted_iota(jnp.int32, (2 * BLOCK, GROUP * BLOCK), 1) & (BLOCK - 1)
        return jnp.where((c >= q) & (c <= q + WINDOW) & (c >= first_valid_key), 0.0, NEG)

    @pl.when(s == 0)
    def _start():
        band_scr[...] = band_bias(0)
        u_scr[0:CONV_HDR, :] = jnp.zeros((CONV_HDR, D_CONV), F32)
        k_scr[0:BLOCK, :] = jnp.zeros((BLOCK, KV_DIM), F32)
        v_scr[0:BLOCK, :] = jnp.zeros((BLOCK, KV_DIM), F32)
        first_scr[...] = band_bias(BLOCK + META_PAD)
        x1_meta, = _interleave(mixer(meta_ref[...], BLOCK))
        ymeta_ref[...], = _interleave(mlp(x1_meta))
        first_scr[...] = band_bias(META_PAD)
        _interleave(mix_tile())
        first_scr[...] = band_scr[...]

    @pl.when((s > 0) & (s < last))
    def _steady():
        _interleave(mlp_tile(), mix_tile(), shares=(MLP_YIELDS, L0_MIX_YIELDS))

    @pl.when(s == last)
    def _drain():
        _interleave(mlp_tile())

    @pl.when(s == last - 1)
    def _state():
        conv_ref[...] = u_scr[CONV_HDR - 2:CONV_HDR, :]
        kout_ref[...] = k_scr[0:BLOCK, :]
        vout_ref[...] = v_scr[0:BLOCK, :]


def _l1_mixer(x, rows, is_meta, gmix, wpool_ref, pscale, h_scr):
    h = _rms(x, gmix)
    h_scr[POOL_MAX:POOL_MAX + rows, :] = h
    ys = []
    for gi, w in enumerate(POOL_WINDOWS):
        cols = slice(gi * POOL_GROUP_DIM, (gi + 1) * POOL_GROUP_DIM)
        hg = h[:, cols]
        acc = hg
        for j in range(1, w):
            acc = acc + h_scr[POOL_MAX - j:POOL_MAX - j + rows, cols]
        if is_meta:
            r = lax.broadcasted_iota(jnp.int32, (rows, 1), 0)
            pooled = acc / jnp.clip(r - (META_PAD - 1), 1, w).astype(F32)
        else:
            pooled = acc * (1.0 / w)
        ys.append(_dot((pooled - hg).astype(BF16), wpool_ref[gi]))
        yield
    h_scr[0:POOL_MAX, :] = h_scr[rows:rows + POOL_MAX, :]
    return x + jnp.concatenate(ys, axis=-1) * pscale


def _l1_prompt_kernel(x_ref, meta_ref, gmix_ref, wpool_ref, pscale_ref, gmlp_ref, wup_ref, wdn_ref,
                      y_ref, pool_ref, h_scr, x1_scr):
    s = pl.program_id(1)
    last = pl.num_programs(1) - 1
    mixer = functools.partial(_l1_mixer, gmix=gmix_ref[...], wpool_ref=wpool_ref, pscale=pscale_ref[...],
                              h_scr=h_scr)
    mlp = functools.partial(_mlp_steps, g=gmlp_ref[...], wup_ref=wup_ref, wdn_ref=wdn_ref)
    mlp_tile, mix_tile = _pipeline_steps(s, x_ref, y_ref, x1_scr,
                                         functools.partial(mixer, is_meta=False), mlp)

    @pl.when(s == 0)
    def _start():
        h_scr[0:POOL_MAX, :] = jnp.zeros((POOL_MAX, D_MODEL), F32)
        _interleave(mixer(meta_ref[...], BLOCK, True))
        _interleave(mix_tile())

    @pl.when((s > 0) & (s < last))
    def _steady():
        _interleave(mlp_tile(), mix_tile(), shares=(MLP_YIELDS, len(POOL_WINDOWS)))

    @pl.when(s == last)
    def _drain():
        _interleave(mlp_tile())

    @pl.when(s == last - 1)
    def _state():
        pool_ref[...] = h_scr[1:POOL_MAX, :]


def _time_major(ref):
    return jnp.concatenate([ref[:, t, :] for t in range(ref.shape[1])], axis=0)


def _s0_pre_kernel(x_ref, st_ref, gmix_ref, win_ref, segq_ref, segk_ref, qg_ref, kg_ref, cw_ref,
                   ya_ref, q_ref, k_ref, v_ref, conv_ref, win_bf_ref):
    n_seq, n_t, _ = x_ref.shape
    win_bf_ref[...] = win_ref[...].astype(BF16)
    u, gb, qn, kn, v = _in_proj(_time_major(x_ref), gmix_ref[...], win_bf_ref, segq_ref, segk_ref, qg_ref[...],
                                kg_ref[...])
    rows = lambda a, t: a[t * n_seq:(t + 1) * n_seq, :]
    ue = [st_ref[:, i, :] for i in range(CONV_W - 1)] + [rows(u, t) for t in range(n_t)]
    for t in range(n_t):
        y = ue[t] * cw_ref[0:1, :] + ue[t + 1] * cw_ref[1:2, :] + ue[t + 2] * cw_ref[2:3, :]
        ya_ref[t * n_seq:(t + 1) * n_seq, :] = rows(gb, t) * y
        k_ref[:, t, :] = rows(kn, t)
        v_ref[:, t, :] = rows(v, t)
        for h in range(N_HEADS):
            q_ref[:, (h // GROUP) * n_t * GROUP + t * GROUP + h % GROUP, :] = (
                rows(qn, t)[:, h * HEAD_DIM:(h + 1) * HEAD_DIM])
    for i in range(CONV_W - 1):
        conv_ref[:, i, :] = ue[n_t + i]


def _s0_attn_kernel(q_ref, ckt_ref, cvt_ref, kn_ref, vn_ref, sk_ref, o_ref, kn_scr, vn_scr):
    bb, n_t = kn_ref.shape[0], kn_ref.shape[1]
    wb = ckt_ref.shape[3]
    rows = n_t * GROUP
    t = lax.broadcasted_iota(jnp.int32, (rows, wb), 0) // GROUP
    c = lax.broadcasted_iota(jnp.int32, (rows, wb), 1)
    bias_c = jnp.where(wb + t - c <= WINDOW, 0.0, NEG)[None]
    t = lax.broadcasted_iota(jnp.int32, (rows, SAMPLE_NEW), 0) // GROUP
    k = lax.broadcasted_iota(jnp.int32, (rows, SAMPLE_NEW), 1)
    bias_n = jnp.where((k <= t) & (k < n_t), 0.0, NEG)[None]
    kn_scr[...] = jnp.zeros(kn_scr.shape, F32)
    vn_scr[...] = jnp.zeros(vn_scr.shape, F32)
    kn_scr[:, 0:n_t, :] = kn_ref[...]
    vn_scr[:, 0:n_t, :] = vn_ref[...]
    for j in range(N_KV):
        lanes = slice(j * HEAD_DIM, (j + 1) * HEAD_DIM)
        q = q_ref[:, j * rows:(j + 1) * rows, :].astype(BF16)
        sk = sk_ref[j][None]
        s_c = jnp.einsum('bqd,bdk->bqk', q, ckt_ref[:, j].astype(BF16), preferred_element_type=F32) + bias_c
        s_n = jnp.einsum('bqd,bkd->bqk', q, kn_scr[:, :, lanes].astype(BF16),
                         preferred_element_type=F32) + bias_n
        m = jnp.maximum(jnp.maximum(jnp.max(s_c, axis=-1, keepdims=True),
                                    jnp.max(s_n, axis=-1, keepdims=True)), sk)
        p_c = jnp.exp(s_c - m)
        p_n = jnp.exp(s_n - m)
        den = jnp.sum(p_c, axis=-1, keepdims=True) + jnp.sum(p_n, axis=-1, keepdims=True) + jnp.exp(sk - m)
        o = (jnp.einsum('bqk,bdk->bqd', p_c.astype(BF16), cvt_ref[:, j].astype(BF16),
                        preferred_element_type=F32)
             + jnp.einsum('bqk,bkd->bqd', p_n.astype(BF16), vn_scr[:, :, lanes].astype(BF16),
                          preferred_element_type=F32))
        o_ref[:, j * rows:(j + 1) * rows, :] = o / den


def _mlp_stream_step(c, wup_ref, wdn_ref, y_ref, wup_bf_ref, wdn_bf_ref, x1_scr, xn_scr):
    wup_bf_ref[...] = wup_ref[...].astype(BF16)
    wdn_bf_ref[...] = wdn_ref[...].astype(BF16)
    a = jnp.square(jnp.maximum(_dot(xn_scr[...], wup_bf_ref[...]), 0.0)).astype(BF16)
    x1_scr[...] += _dot(a, wdn_bf_ref[...])
    if y_ref is not None:
        @pl.when(c == pl.num_programs(0) - 1)
        def _done():
            y_ref[...] = x1_scr[...]


def _s0_post_kernel(x_ref, ya_ref, o_ref, wout_ref, gmlp_ref, wup_ref, wdn_ref,
                    y_ref, wout_bf_ref, wup_bf_ref, wdn_bf_ref, x1_scr, xn_scr):
    c = pl.program_id(0)

    @pl.when(c == 0)
    def _mix():
        n_t = x_ref.shape[1]
        wout_bf_ref[...] = wout_ref[...].astype(BF16)
        yb = jnp.concatenate(
            [jnp.concatenate([o_ref[:, (h // GROUP) * n_t * GROUP + t * GROUP + h % GROUP, :]
                              for h in range(N_HEADS)], axis=-1) for t in range(n_t)], axis=0)
        mix = jnp.concatenate([ya_ref[...], yb], axis=-1).astype(BF16)
        x1 = _time_major(x_ref) + _dot(mix, wout_bf_ref[...])
        x1_scr[...] = x1
        xn_scr[...] = _rms(x1, gmlp_ref[...]).astype(BF16)

    _mlp_stream_step(c, wup_ref, wdn_ref, y_ref, wup_bf_ref, wdn_bf_ref, x1_scr, xn_scr)


def _s1_kernel(x_ref, buf_ref, gmix_ref, wpool_ref, pscale_ref, gmlp_ref, wup_ref, wdn_ref,
               y_ref, pool_ref, wpool_bf_ref, wup_bf_ref, wdn_bf_ref, x1_scr, xn_scr):
    c = pl.program_id(0)
    n_buf, n_seq, _ = buf_ref.shape
    n_t = x_ref.shape[0] // n_seq

    @pl.when(c == 0)
    def _mix():
        wpool_bf_ref[...] = wpool_ref[...].astype(BF16)
        x = x_ref[...]
        h = _rms(x, gmix_ref[...])
        he = [buf_ref[i] for i in range(n_buf)] + [h[t * n_seq:(t + 1) * n_seq, :] for t in range(n_t)]
        for i in range(n_buf):
            pool_ref[i] = he[n_t + i]
        ys = []
        for gi, w in enumerate(POOL_WINDOWS):
            cols = slice(gi * POOL_GROUP_DIM, (gi + 1) * POOL_GROUP_DIM)
            dl = []
            for t in range(n_t):
                acc = he[n_buf + t][:, cols]
                for j in range(1, w):
                    acc = acc + he[n_buf + t - j][:, cols]
                dl.append(acc * (1.0 / w) - he[n_buf + t][:, cols])
            ys.append(_dot(jnp.concatenate(dl, axis=0).astype(BF16), wpool_bf_ref[gi]))
        x1 = x + jnp.concatenate(ys, axis=-1) * pscale_ref[...]
        x1_scr[...] = x1
        xn_scr[...] = _rms(x1, gmlp_ref[...]).astype(BF16)

    _mlp_stream_step(c, wup_ref, wdn_ref, None, wup_bf_ref, wdn_bf_ref, x1_scr, xn_scr)

    @pl.when(c == pl.num_programs(0) - 1)
    def _done():
        for t in range(n_t):
            y_ref[:, t, :] = x1_scr[t * n_seq:(t + 1) * n_seq, :]


class _Stacked(NamedTuple):
    array: jax.Array
    layer: int


def _operand(arg):
    return arg.array if isinstance(arg, _Stacked) else arg


def _resident(arg, n_grid):
    if isinstance(arg, _Stacked):
        shape, idx = (None,) + arg.array.shape[1:], (arg.layer,) + (0,) * (arg.array.ndim - 1)
    else:
        shape, idx = arg.shape, (0,) * arg.ndim
    index_map = {1: lambda i: idx, 2: lambda i, j: idx}[n_grid]
    return pl.BlockSpec(shape, index_map, pipeline_mode=pl.Buffered(1))


def _params(n_grid):
    return pltpu.CompilerParams(dimension_semantics=("arbitrary",) * n_grid, vmem_limit_bytes=VMEM_LIMIT)


def _prompt_specs(n_tiles):
    tile_in = pl.BlockSpec((None, SEQ_TILE, D_MODEL), lambda b, s: (b, jnp.minimum(s, n_tiles - 1), 0))
    tile_out = pl.BlockSpec((None, SEQ_TILE, D_MODEL), lambda b, s: (b, jnp.maximum(s - 1, 0), 0))
    per_b = lambda r, c: pl.BlockSpec((None, r, c), lambda b, s: (b, 0, 0))
    return tile_in, tile_out, per_b


def _l0_prompt(x, sinks, *resident):
    n_b, seq, _ = x.shape
    n_tiles = seq // SEQ_TILE
    tile_in, tile_out, per_b = _prompt_specs(n_tiles)
    return pl.pallas_call(
        _l0_prompt_kernel,
        grid=(n_b, n_tiles + 1),
        in_specs=[pl.BlockSpec(memory_space=pltpu.SMEM), tile_in] + [_resident(a, 2) for a in resident],
        out_specs=[tile_out, per_b(BLOCK, D_MODEL), per_b(CONV_W - 1, D_CONV), per_b(WINDOW, KV_DIM),
                   per_b(WINDOW, KV_DIM)],
        out_shape=[jax.ShapeDtypeStruct((n_b, seq, D_MODEL), F32),
                   jax.ShapeDtypeStruct((n_b, BLOCK, D_MODEL), F32),
                   jax.ShapeDtypeStruct((n_b, CONV_W - 1, D_CONV), F32),
                   jax.ShapeDtypeStruct((n_b, WINDOW, KV_DIM), F32),
                   jax.ShapeDtypeStruct((n_b, WINDOW, KV_DIM), F32)],
        scratch_shapes=[pltpu.VMEM((CONV_HDR + SEQ_TILE, D_CONV), F32),
                        pltpu.VMEM((BLOCK + SEQ_TILE, KV_DIM), F32),
                        pltpu.VMEM((BLOCK + SEQ_TILE, KV_DIM), F32),
                        pltpu.VMEM((2 * BLOCK, GROUP * BLOCK), F32),
                        pltpu.VMEM((2 * BLOCK, GROUP * BLOCK), F32),
                        pltpu.VMEM((2, SEQ_TILE, D_MODEL), F32)],
        compiler_params=_params(2),
        name="l0_prompt",
    )(sinks, x, *[_operand(a) for a in resident])


def _l1_prompt(x, xmeta, *resident):
    n_b, seq, _ = x.shape
    n_tiles = seq // SEQ_TILE
    tile_in, tile_out, per_b = _prompt_specs(n_tiles)
    return pl.pallas_call(
        _l1_prompt_kernel,
        grid=(n_b, n_tiles + 1),
        in_specs=[tile_in, per_b(BLOCK, D_MODEL)] + [_resident(a, 2) for a in resident],
        out_specs=[tile_out, per_b(POOL_MAX - 1, D_MODEL)],
        out_shape=[jax.ShapeDtypeStruct((n_b, seq, D_MODEL), F32),
                   jax.ShapeDtypeStruct((n_b, POOL_MAX - 1, D_MODEL), F32)],
        scratch_shapes=[pltpu.VMEM((POOL_MAX + SEQ_TILE, D_MODEL), F32),
                        pltpu.VMEM((2, SEQ_TILE, D_MODEL), F32)],
        compiler_params=_params(2),
        name="l1_prompt",
    )(x, xmeta, *[_operand(a) for a in resident])


def _single_step(kernel, name, out_shape, *args):
    return pl.pallas_call(
        kernel,
        grid=(1,),
        in_specs=[_resident(a, 1) for a in args],
        out_specs=[pl.BlockSpec(o.shape, lambda i, n=len(o.shape): (0,) * n) for o in out_shape],
        out_shape=out_shape,
        compiler_params=_params(1),
        name=name,
    )(*[_operand(a) for a in args])


def _streamed_mlp(kernel, name, n_rows, outs, consts, w_up, w_down, layer):
    const = lambda shape: pl.BlockSpec(shape, lambda c, n=len(shape): (0,) * n)
    return pl.pallas_call(
        kernel,
        grid=(D_FF // FF_STREAM,),
        in_specs=[_resident(a, 1) for a in consts]
        + [pl.BlockSpec((None, D_MODEL, FF_STREAM), lambda c: (layer, 0, c)),
           pl.BlockSpec((None, FF_STREAM, D_MODEL), lambda c: (layer, c, 0))],
        out_specs=[const(o.shape) for o in outs]
        + [pl.BlockSpec((D_MODEL, FF_STREAM), lambda c: (0, c)),
           pl.BlockSpec((FF_STREAM, D_MODEL), lambda c: (c, 0))],
        out_shape=list(outs) + [jax.ShapeDtypeStruct((D_MODEL, D_FF), BF16),
                                jax.ShapeDtypeStruct((D_FF, D_MODEL), BF16)],
        scratch_shapes=[pltpu.VMEM((n_rows, D_MODEL), F32), pltpu.VMEM((n_rows, D_MODEL), BF16)],
        compiler_params=_params(1),
        name=name,
    )(*[_operand(a) for a in consts], w_up, w_down)


def _s0_attn(q4, ckt, cvt, kn, vn, sk_rows):
    n_seq, _, _, wb = ckt.shape
    n_t = kn.shape[1]
    rows = n_t * GROUP
    blk = lambda *tail: pl.BlockSpec((SAMPLE_BB,) + tail, lambda i: (i,) + (0,) * len(tail))
    return pl.pallas_call(
        _s0_attn_kernel,
        grid=(n_seq // SAMPLE_BB,),
        in_specs=[blk(N_KV * rows, HEAD_DIM), blk(N_KV, HEAD_DIM, wb), blk(N_KV, HEAD_DIM, wb),
                  blk(n_t, KV_DIM), blk(n_t, KV_DIM), pl.BlockSpec((N_KV, rows, 1), lambda i: (0, 0, 0))],
        out_specs=blk(N_KV * rows, HEAD_DIM),
        out_shape=jax.ShapeDtypeStruct((n_seq, N_KV * rows, HEAD_DIM), F32),
        scratch_shapes=[pltpu.VMEM((SAMPLE_BB, SAMPLE_NEW, KV_DIM), F32),
                        pltpu.VMEM((SAMPLE_BB, SAMPLE_NEW, KV_DIM), F32)],
        compiler_params=_params(1),
        name="s0_attn",
    )(q4, ckt, cvt, kn, vn, sk_rows)


def kernel(x_prompt, x_sample, state_conv, cache_k_win, cache_v_win, state_pool, meta_tokens, norm_mix, norm_mlp, w_in_even, conv_w, q_norm, k_norm, attn_sinks, w_out_even, w_pool, pool_scale, w_up, w_down):
    n_seq, n_t, _ = x_sample.shape
    wb = cache_k_win.shape[2]
    assert x_prompt.shape[1] % SEQ_TILE == 0 and x_prompt.shape[1] >= 2 * SEQ_TILE and n_seq % SAMPLE_BB == 0
    assert wb == WINDOW and n_t <= SAMPLE_NEW

    gmix = [_Stacked(norm_mix[:, None, :], l) for l in range(2)]
    gmlp = [_Stacked(norm_mlp[:, None, :], l) for l in range(2)]
    qg = jnp.tile(q_norm[0], N_HEADS)[None, :]
    kg = jnp.tile(k_norm[0], N_KV)[None, :]
    segq = jnp.kron(jnp.eye(N_HEADS, dtype=F32), jnp.full((HEAD_DIM, HEAD_DIM), 1.0 / HEAD_DIM, F32)).astype(BF16)
    segk = segq[:KV_DIM, :KV_DIM]
    cw = _Stacked(conv_w, 0)
    sinks = attn_sinks[0]
    meta_pad = jnp.pad(meta_tokens, ((META_PAD, 0), (0, 0)))
    sds = lambda *shape, dtype=F32: jax.ShapeDtypeStruct(shape, dtype)

    n_rows = n_t * n_seq
    ya, q4, kn_b, vn_b, conv_s, win = _single_step(
        _s0_pre_kernel, "s0_pre",
        [sds(n_rows, D_CONV), sds(n_seq, N_HEADS * n_t, HEAD_DIM), sds(n_seq, n_t, KV_DIM),
         sds(n_seq, n_t, KV_DIM), sds(n_seq, CONV_W - 1, D_CONV), sds(D_MODEL, D_IN_EVEN, dtype=BF16)],
        x_sample, _Stacked(state_conv, 0), gmix[0], _Stacked(w_in_even, 0), segq, segk, qg, kg, cw)
    sk_rows = jnp.tile(sinks.reshape(N_KV, 1, GROUP), (1, n_t, 1)).reshape(N_KV, n_t * GROUP, 1)
    o4 = _s0_attn(q4, jnp.transpose(cache_k_win[0], (0, 2, 3, 1)), jnp.transpose(cache_v_win[0], (0, 2, 3, 1)),
                  kn_b, vn_b, sk_rows)
    new_window = lambda cache, new: jnp.concatenate(
        [cache[:, :, n_t:], new.reshape(1, n_seq, n_t, N_KV, HEAD_DIM)], axis=2)
    k_s, v_s = new_window(cache_k_win, kn_b), new_window(cache_v_win, vn_b)
    xs2, wout, wup0, wdn0 = _streamed_mlp(
        _s0_post_kernel, "s0_post", n_rows, [sds(n_rows, D_MODEL), sds(D_MODEL, D_MODEL, dtype=BF16)],
        [x_sample, ya, o4, _Stacked(w_out_even, 0), gmlp[0]], w_up, w_down, 0)
    buf_t = jnp.transpose(state_pool[0], (1, 0, 2))
    y_sample, pool_s, wpool, wup1, wdn1 = _streamed_mlp(
        _s1_kernel, "s1", n_rows,
        [sds(n_seq, n_t, D_MODEL), sds(POOL_MAX - 1, n_seq, D_MODEL), sds(*w_pool.shape[1:], dtype=BF16)],
        [xs2, buf_t, gmix[1], _Stacked(w_pool, 0), pool_scale, gmlp[1]], w_up, w_down, 1)

    x2, x2_meta, conv_p, k_p, v_p = _l0_prompt(x_prompt, sinks, meta_pad, gmix[0], win, segq, segk, qg, kg, cw,
                                               wout, gmlp[0], wup0, wdn0)
    y_prompt, pool_p = _l1_prompt(x2, x2_meta, gmix[1], wpool, pool_scale, gmlp[1], wup1, wdn1)

    kv5 = lambda a: a.reshape(1, a.shape[0], WINDOW, N_KV, HEAD_DIM)
    return (y_prompt, y_sample, conv_p[None], conv_s[None], kv5(k_p), k_s, kv5(v_p), v_s,
            pool_p[None], jnp.transpose(pool_s, (1, 0, 2))[None])
```

```python
import functools
from typing import NamedTuple

import jax
import jax.numpy as jnp
from jax import lax
from jax.experimental import pallas as pl
from jax.experimental.pallas import tpu as pltpu

F32 = jnp.float32
BF16 = jnp.bfloat16

D_MODEL = 1024
D_CONV = 512
CONV_W = 3
HEAD_DIM = 64
N_HEADS = 8
N_KV = 2
GROUP = N_HEADS // N_KV
WINDOW = 128
BLOCK = 128
Q_DIM = N_HEADS * HEAD_DIM
KV_DIM = N_KV * HEAD_DIM
POOL_WINDOWS = (2, 4, 8, 16)
POOL_GROUP_DIM = D_MODEL // len(POOL_WINDOWS)
POOL_MAX = 16
D_FF = 4 * D_MODEL
D_IN_EVEN = 3 * D_CONV + Q_DIM + 2 * KV_DIM
N_META = 16
EPS = 1e-6
NEG = -1e30

META_PAD = BLOCK - N_META
SEQ_TILE = 512
FF_CHUNK = 1024
FF_STREAM = 512
CONV_HDR = 8
SAMPLE_BB = 16
SAMPLE_NEW = 16
VMEM_LIMIT = 56 * 1024 * 1024
MLP_YIELDS = 2 * (D_FF // FF_CHUNK)
L0_MIX_YIELDS = 1 + 2 * N_KV * (SEQ_TILE // BLOCK)


def _dot(a, b):
    return jnp.dot(a, b, preferred_element_type=F32)


def _dot_nt(a, b):
    return lax.dot_general(a, b, (((1,), (1,)), ((), ())), preferred_element_type=F32)


def _dot_tn(a, b):
    return lax.dot_general(a, b, (((0,), (0,)), ((), ())), preferred_element_type=F32)


def _rms(x, g):
    ms = jnp.mean(x * x, axis=-1, keepdims=True)
    return x * lax.rsqrt(ms + EPS) * g


def _head_rms(x, seg, g):
    ms = _dot((x * x).astype(BF16), seg)
    return x * lax.rsqrt(ms + EPS) * g


def _interleave(*steppers, shares=None):
    shares = shares or [1] * len(steppers)
    results = [None] * len(steppers)
    done = [0] * len(steppers)
    live = set(range(len(steppers)))
    while live:
        i = min(live, key=lambda k: ((done[k] + 1) / shares[k], k))
        try:
            next(steppers[i])
            done[i] += 1
        except StopIteration as finished:
            results[i] = finished.value
            live.discard(i)
    return results


def _mlp_steps(x, g, wup_ref, wdn_ref):
    xn = _rms(x, g).astype(BF16)
    acc = x
    for c in range(D_FF // FF_CHUNK):
        cols = slice(c * FF_CHUNK, (c + 1) * FF_CHUNK)
        h = _dot(xn, wup_ref[:, cols])
        a = jnp.square(jnp.maximum(h, 0.0)).astype(BF16)
        yield
        acc = acc + _dot(a, wdn_ref[cols, :])
        yield
    return acc


def _mlp(x, g, wup_ref, wdn_ref):
    return _interleave(_mlp_steps(x, g, wup_ref, wdn_ref))[0]


def _in_proj(x, g, win_ref, segq_ref, segk_ref, qg, kg):
    hn = _rms(x, g).astype(BF16)
    kv_col = 3 * D_CONV + Q_DIM
    half = x.shape[0] // 2
    z = _dot(hn, win_ref[:, 0:kv_col])
    kv = jnp.concatenate([_dot(hn[0:half], win_ref[:, kv_col:]), _dot(hn[half:], win_ref[:, kv_col:])], axis=0)
    xa = z[:, 0:D_CONV]
    gc = z[:, D_CONV:2 * D_CONV]
    gb = z[:, 2 * D_CONV:3 * D_CONV]
    q = z[:, 3 * D_CONV:]
    k = kv[:, 0:KV_DIM]
    v = kv[:, KV_DIM:]
    qn = _head_rms(q, segq_ref[...], qg) * (HEAD_DIM ** -0.5)
    kn = _head_rms(k, segk_ref[...], kg)
    return gc * xa, gb, qn, kn, v


def _l0_mixer(x, rows, sinks_ref, gmix, win_ref, segq_ref, segk_ref, qg, kg, cw_ref, wout_ref,
              u_scr, k_scr, v_scr, first_scr, band_scr):
    u, gb, qn, kn, v = _in_proj(x, gmix, win_ref, segq_ref, segk_ref, qg, kg)

    u_scr[CONV_HDR:CONV_HDR + rows, :] = u
    u1 = u_scr[CONV_HDR - 1:CONV_HDR - 1 + rows, :]
    u2 = u_scr[CONV_HDR - 2:CONV_HDR - 2 + rows, :]
    ya = gb * (u2 * cw_ref[0:1, :] + u1 * cw_ref[1:2, :] + u * cw_ref[2:3, :])
    k_scr[BLOCK:BLOCK + rows, :] = kn
    v_scr[BLOCK:BLOCK + rows, :] = v
    yield

    yb_blocks = []
    for i in range(rows // BLOCK):
        qb = qn[i * BLOCK:(i + 1) * BLOCK, :]
        bias = first_scr[...] if i == 0 else band_scr[...]
        heads_t = []
        for j in range(N_KV):
            lanes = slice(j * HEAD_DIM, (j + 1) * HEAD_DIM)
            k2 = k_scr[i * BLOCK:(i + 2) * BLOCK, lanes].astype(BF16)
            v2 = v_scr[i * BLOCK:(i + 2) * BLOCK, lanes].astype(BF16)
            qs = jnp.concatenate(
                [qb[:, (j * GROUP + g) * HEAD_DIM:(j * GROUP + g + 1) * HEAD_DIM] for g in range(GROUP)],
                axis=0).astype(BF16)
            sk = jnp.concatenate(
                [jnp.full((1, BLOCK), sinks_ref[j * GROUP + g], F32) for g in range(GROUP)], axis=1)
            st = _dot_nt(k2, qs) + bias
            m = jnp.maximum(jnp.max(st, axis=0, keepdims=True), sk)
            p = jnp.exp(st - m)
            den = jnp.sum(p, axis=0, keepdims=True) + jnp.exp(sk - m)
            yield
            ot = _dot_tn(v2, p.astype(BF16)) / den
            heads_t += [ot[:, g * BLOCK:(g + 1) * BLOCK] for g in range(GROUP)]
            yield
        yb_blocks.append(jnp.concatenate(heads_t, axis=0).T)
    yb = yb_blocks[0] if len(yb_blocks) == 1 else jnp.concatenate(yb_blocks, axis=0)

    u_scr[CONV_HDR - 2:CONV_HDR, :] = u_scr[CONV_HDR - 2 + rows:CONV_HDR + rows, :]
    k_scr[0:BLOCK, :] = k_scr[rows:rows + BLOCK, :]
    v_scr[0:BLOCK, :] = v_scr[rows:rows + BLOCK, :]

    mix = jnp.concatenate([ya, yb], axis=-1).astype(BF16)
    return x + _dot(mix, wout_ref[...])


def _pipeline_steps(s, x_ref, y_ref, x1_scr, mixer, mlp):
    slot = lax.rem(s, 2)

    def mix_tile():
        x1_scr[slot] = yield from mixer(x_ref[...], SEQ_TILE)

    def mlp_tile():
        y_ref[...] = yield from mlp(x1_scr[1 - slot])

    return mlp_tile, mix_tile


def _l0_prompt_kernel(sinks_ref, x_ref, meta_ref, gmix_ref, win_ref, segq_ref, segk_ref, qg_ref, kg_ref,
                      cw_ref, wout_ref, gmlp_ref, wup_ref, wdn_ref,
                      y_ref, ymeta_ref, conv_ref, kout_ref, vout_ref,
                      u_scr, k_scr, v_scr, first_scr, band_scr, x1_scr):
    s = pl.program_id(1)
    last = pl.num_programs(1) - 1
    mixer = functools.partial(
        _l0_mixer, sinks_ref=sinks_ref, gmix=gmix_ref[...], win_ref=win_ref, segq_ref=segq_ref,
        segk_ref=segk_ref, qg=qg_ref[...], kg=kg_ref[...], cw_ref=cw_ref, wout_ref=wout_ref,
        u_scr=u_scr, k_scr=k_scr, v_scr=v_scr, first_scr=first_scr, band_scr=band_scr)
    mlp = functools.partial(_mlp_steps, g=gmlp_ref[...], wup_ref=wup_ref, wdn_ref=wdn_ref)
    mlp_tile, mix_tile = _pipeline_steps(s, x_ref, y_ref, x1_scr, mixer, mlp)

    def band_bias(first_valid_key):
        c = lax.broadcasted_iota(jnp.int32, (2 * BLOCK, GROUP * BLOCK), 0)
        q = lax.broadcasted_iota(jnp.int32, (2 * BLOCK, GROUP * BLOCK), 1) & (BLOCK - 1)
        return jnp.where((c >= q) & (c <= q + WINDOW) & (c >= first_valid_key), 0.0, NEG)

    @pl.when(s == 0)
    def _start():
        band_scr[...] = band_bias(0)
        u_scr[0:CONV_HDR, :] = jnp.zeros((CONV_HDR, D_CONV), F32)
        k_scr[0:BLOCK, :] = jnp.zeros((BLOCK, KV_DIM), F32)
        v_scr[0:BLOCK, :] = jnp.zeros((BLOCK, KV_DIM), F32)
        first_scr[...] = band_bias(BLOCK + META_PAD)
        x1_meta, = _interleave(mixer(meta_ref[...], BLOCK))
        ymeta_ref[...], = _interleave(mlp(x1_meta))
        first_scr[...] = band_bias(META_PAD)
        _interleave(mix_tile())
        first_scr[...] = band_scr[...]

    @pl.when((s > 0) & (s < last))
    def _steady():
        _interleave(mlp_tile(), mix_tile(), shares=(MLP_YIELDS, L0_MIX_YIELDS))

    @pl.when(s == last)
    def _drain():
        _interleave(mlp_tile())

    @pl.when(s == last - 1)
    def _state():
        conv_ref[...] = u_scr[CONV_HDR - 2:CONV_HDR, :]
        kout_ref[...] = k_scr[0:BLOCK, :]
        vout_ref[...] = v_scr[0:BLOCK, :]


def _l1_mixer(x, rows, is_meta, gmix, wpool_ref, pscale, h_scr):
    h = _rms(x, gmix)
    h_scr[POOL_MAX:POOL_MAX + rows, :] = h
    ys = []
    for gi, w in enumerate(POOL_WINDOWS):
        cols = slice(gi * POOL_GROUP_DIM, (gi + 1) * POOL_GROUP_DIM)
        hg = h[:, cols]
        acc = hg
        for j in range(1, w):
            acc = acc + h_scr[POOL_MAX - j:POOL_MAX - j + rows, cols]
        if is_meta:
            r = lax.broadcasted_iota(jnp.int32, (rows, 1), 0)
            pooled = acc / jnp.clip(r - (META_PAD - 1), 1, w).astype(F32)
        else:
            pooled = acc * (1.0 / w)
        ys.append(_dot((pooled - hg).astype(BF16), wpool_ref[gi]))
        yield
    h_scr[0:POOL_MAX, :] = h_scr[rows:rows + POOL_MAX, :]
    return x + jnp.concatenate(ys, axis=-1) * pscale


def _l1_prompt_kernel(x_ref, meta_ref, gmix_ref, wpool_ref, pscale_ref, gmlp_ref, wup_ref, wdn_ref,
                      y_ref, pool_ref, h_scr, x1_scr):
    s = pl.program_id(1)
    last = pl.num_programs(1) - 1
    mixer = functools.partial(_l1_mixer, gmix=gmix_ref[...], wpool_ref=wpool_ref, pscale=pscale_ref[...],
                              h_scr=h_scr)
    mlp = functools.partial(_mlp_steps, g=gmlp_ref[...], wup_ref=wup_ref, wdn_ref=wdn_ref)
    mlp_tile, mix_tile = _pipeline_steps(s, x_ref, y_ref, x1_scr,
                                         functools.partial(mixer, is_meta=False), mlp)

    @pl.when(s == 0)
    def _start():
        h_scr[0:POOL_MAX, :] = jnp.zeros((POOL_MAX, D_MODEL), F32)
        _interleave(mixer(meta_ref[...], BLOCK, True))
        _interleave(mix_tile())

    @pl.when((s > 0) & (s < last))
    def _steady():
        _interleave(mlp_tile(), mix_tile(), shares=(MLP_YIELDS, len(POOL_WINDOWS)))

    @pl.when(s == last)
    def _drain():
        _interleave(mlp_tile())

    @pl.when(s == last - 1)
    def _state():
        pool_ref[...] = h_scr[1:POOL_MAX, :]


def _time_major(ref):
    return jnp.concatenate([ref[:, t, :] for t in range(ref.shape[1])], axis=0)


def _s0_pre_kernel(x_ref, st_ref, gmix_ref, win_ref, segq_ref, segk_ref, qg_ref, kg_ref, cw_ref,
                   ya_ref, q_ref, k_ref, v_ref, conv_ref, win_bf_ref):
    n_seq, n_t, _ = x_ref.shape
    win_bf_ref[...] = win_ref[...].astype(BF16)
    u, gb, qn, kn, v = _in_proj(_time_major(x_ref), gmix_ref[...], win_bf_ref, segq_ref, segk_ref, qg_ref[...],
                                kg_ref[...])
    rows = lambda a, t: a[t * n_seq:(t + 1) * n_seq, :]
    ue = [st_ref[:, i, :] for i in range(CONV_W - 1)] + [rows(u, t) for t in range(n_t)]
    for t in range(n_t):
        y = ue[t] * cw_ref[0:1, :] + ue[t + 1] * cw_ref[1:2, :] + ue[t + 2] * cw_ref[2:3, :]
        ya_ref[t * n_seq:(t + 1) * n_seq, :] = rows(gb, t) * y
        k_ref[:, t, :] = rows(kn, t)
        v_ref[:, t, :] = rows(v, t)
        for h in range(N_HEADS):
            q_ref[:, (h // GROUP) * n_t * GROUP + t * GROUP + h % GROUP, :] = (
                rows(qn, t)[:, h * HEAD_DIM:(h + 1) * HEAD_DIM])
    for i in range(CONV_W - 1):
        conv_ref[:, i, :] = ue[n_t + i]


def _s0_attn_kernel(q_ref, ckt_ref, cvt_ref, kn_ref, vn_ref, knt_ref, vnt_ref, sk_ref,
                    o_ref, kst_ref, vst_ref, kn_scr, vn_scr):
    bb, n_t = kn_ref.shape[0], kn_ref.shape[1]
    wb = ckt_ref.shape[3]
    rows = n_t * GROUP
    kst_ref[...] = jnp.concatenate([ckt_ref[:, :, :, n_t:], knt_ref[...]], axis=-1)
    vst_ref[...] = jnp.concatenate([cvt_ref[:, :, :, n_t:], vnt_ref[...]], axis=-1)
    t = lax.broadcasted_iota(jnp.int32, (rows, wb), 0) // GROUP
    c = lax.broadcasted_iota(jnp.int32, (rows, wb), 1)
    bias_c = jnp.where(wb + t - c <= WINDOW, 0.0, NEG)[None]
    t = lax.broadcasted_iota(jnp.int32, (rows, SAMPLE_NEW), 0) // GROUP
    k = lax.broadcasted_iota(jnp.int32, (rows, SAMPLE_NEW), 1)
    bias_n = jnp.where((k <= t) & (k < n_t), 0.0, NEG)[None]
    kn_scr[...] = jnp.zeros(kn_scr.shape, F32)
    vn_scr[...] = jnp.zeros(vn_scr.shape, F32)
    kn_scr[:, 0:n_t, :] = kn_ref[...]
    vn_scr[:, 0:n_t, :] = vn_ref[...]
    for j in range(N_KV):
        lanes = slice(j * HEAD_DIM, (j + 1) * HEAD_DIM)
        q = q_ref[:, j * rows:(j + 1) * rows, :].astype(BF16)
        sk = sk_ref[j][None]
        s_c = jnp.einsum('bqd,bdk->bqk', q, ckt_ref[:, j].astype(BF16), preferred_element_type=F32) + bias_c
        s_n = jnp.einsum('bqd,bkd->bqk', q, kn_scr[:, :, lanes].astype(BF16),
                         preferred_element_type=F32) + bias_n
        m = jnp.maximum(jnp.maximum(jnp.max(s_c, axis=-1, keepdims=True),
                                    jnp.max(s_n, axis=-1, keepdims=True)), sk)
        p_c = jnp.exp(s_c - m)
        p_n = jnp.exp(s_n - m)
        den = jnp.sum(p_c, axis=-1, keepdims=True) + jnp.sum(p_n, axis=-1, keepdims=True) + jnp.exp(sk - m)
        o = (jnp.einsum('bqk,bdk->bqd', p_c.astype(BF16), cvt_ref[:, j].astype(BF16),
                        preferred_element_type=F32)
             + jnp.einsum('bqk,bkd->bqd', p_n.astype(BF16), vn_scr[:, :, lanes].astype(BF16),
                          preferred_element_type=F32))
        o_ref[:, j * rows:(j + 1) * rows, :] = o / den


def _mlp_stream_step(c, wup_ref, wdn_ref, y_ref, wup_bf_ref, wdn_bf_ref, x1_scr, xn_scr):
    wup_bf_ref[...] = wup_ref[...].astype(BF16)
    wdn_bf_ref[...] = wdn_ref[...].astype(BF16)
    a = jnp.square(jnp.maximum(_dot(xn_scr[...], wup_bf_ref[...]), 0.0)).astype(BF16)
    x1_scr[...] += _dot(a, wdn_bf_ref[...])
    if y_ref is not None:
        @pl.when(c == pl.num_programs(0) - 1)
        def _done():
            y_ref[...] = x1_scr[...]


def _s0_post_kernel(x_ref, ya_ref, o_ref, wout_ref, gmlp_ref, wup_ref, wdn_ref,
                    y_ref, wout_bf_ref, wup_bf_ref, wdn_bf_ref, x1_scr, xn_scr):
    c = pl.program_id(0)

    @pl.when(c == 0)
    def _mix():
        n_t = x_ref.shape[1]
        wout_bf_ref[...] = wout_ref[...].astype(BF16)
        yb = jnp.concatenate(
            [jnp.concatenate([o_ref[:, (h // GROUP) * n_t * GROUP + t * GROUP + h % GROUP, :]
                              for h in range(N_HEADS)], axis=-1) for t in range(n_t)], axis=0)
        mix = jnp.concatenate([ya_ref[...], yb], axis=-1).astype(BF16)
        x1 = _time_major(x_ref) + _dot(mix, wout_bf_ref[...])
        x1_scr[...] = x1
        xn_scr[...] = _rms(x1, gmlp_ref[...]).astype(BF16)

    _mlp_stream_step(c, wup_ref, wdn_ref, y_ref, wup_bf_ref, wdn_bf_ref, x1_scr, xn_scr)


def _s1_kernel(x_ref, buf_ref, gmix_ref, wpool_ref, pscale_ref, gmlp_ref, wup_ref, wdn_ref,
               y_ref, pool_ref, wpool_bf_ref, wup_bf_ref, wdn_bf_ref, x1_scr, xn_scr):
    c = pl.program_id(0)
    n_buf, n_seq, _ = buf_ref.shape
    n_t = x_ref.shape[0] // n_seq

    @pl.when(c == 0)
    def _mix():
        wpool_bf_ref[...] = wpool_ref[...].astype(BF16)
        x = x_ref[...]
        h = _rms(x, gmix_ref[...])
        he = [buf_ref[i] for i in range(n_buf)] + [h[t * n_seq:(t + 1) * n_seq, :] for t in range(n_t)]
        for i in range(n_buf):
            pool_ref[i] = he[n_t + i]
        ys = []
        for gi, w in enumerate(POOL_WINDOWS):
            cols = slice(gi * POOL_GROUP_DIM, (gi + 1) * POOL_GROUP_DIM)
            dl = []
            for t in range(n_t):
                acc = he[n_buf + t][:, cols]
                for j in range(1, w):
                    acc = acc + he[n_buf + t - j][:, cols]
                dl.append(acc * (1.0 / w) - he[n_buf + t][:, cols])
            ys.append(_dot(jnp.concatenate(dl, axis=0).astype(BF16), wpool_bf_ref[gi]))
        x1 = x + jnp.concatenate(ys, axis=-1) * pscale_ref[...]
        x1_scr[...] = x1
        xn_scr[...] = _rms(x1, gmlp_ref[...]).astype(BF16)

    _mlp_stream_step(c, wup_ref, wdn_ref, None, wup_bf_ref, wdn_bf_ref, x1_scr, xn_scr)

    @pl.when(c == pl.num_programs(0) - 1)
    def _done():
        for t in range(n_t):
            y_ref[:, t, :] = x1_scr[t * n_seq:(t + 1) * n_seq, :]


class _Stacked(NamedTuple):
    array: jax.Array
    layer: int


def _operand(arg):
    return arg.array if isinstance(arg, _Stacked) else arg


def _resident(arg, n_grid):
    if isinstance(arg, _Stacked):
        shape, idx = (None,) + arg.array.shape[1:], (arg.layer,) + (0,) * (arg.array.ndim - 1)
    else:
        shape, idx = arg.shape, (0,) * arg.ndim
    index_map = {1: lambda i: idx, 2: lambda i, j: idx}[n_grid]
    return pl.BlockSpec(shape, index_map, pipeline_mode=pl.Buffered(1))


def _params(n_grid):
    return pltpu.CompilerParams(dimension_semantics=("arbitrary",) * n_grid, vmem_limit_bytes=VMEM_LIMIT)


def _prompt_specs(n_tiles):
    tile_in = pl.BlockSpec((None, SEQ_TILE, D_MODEL), lambda b, s: (b, jnp.minimum(s, n_tiles - 1), 0))
    tile_out = pl.BlockSpec((None, SEQ_TILE, D_MODEL), lambda b, s: (b, jnp.maximum(s - 1, 0), 0))
    per_b = lambda r, c: pl.BlockSpec((None, r, c), lambda b, s: (b, 0, 0))
    return tile_in, tile_out, per_b


def _l0_prompt(x, sinks, *resident):
    n_b, seq, _ = x.shape
    n_tiles = seq // SEQ_TILE
    tile_in, tile_out, per_b = _prompt_specs(n_tiles)
    return pl.pallas_call(
        _l0_prompt_kernel,
        grid=(n_b, n_tiles + 1),
        in_specs=[pl.BlockSpec(memory_space=pltpu.SMEM), tile_in] + [_resident(a, 2) for a in resident],
        out_specs=[tile_out, per_b(BLOCK, D_MODEL), per_b(CONV_W - 1, D_CONV), per_b(WINDOW, KV_DIM),
                   per_b(WINDOW, KV_DIM)],
        out_shape=[jax.ShapeDtypeStruct((n_b, seq, D_MODEL), F32),
                   jax.ShapeDtypeStruct((n_b, BLOCK, D_MODEL), F32),
                   jax.ShapeDtypeStruct((n_b, CONV_W - 1, D_CONV), F32),
                   jax.ShapeDtypeStruct((n_b, WINDOW, KV_DIM), F32),
                   jax.ShapeDtypeStruct((n_b, WINDOW, KV_DIM), F32)],
        scratch_shapes=[pltpu.VMEM((CONV_HDR + SEQ_TILE, D_CONV), F32),
                        pltpu.VMEM((BLOCK + SEQ_TILE, KV_DIM), F32),
                        pltpu.VMEM((BLOCK + SEQ_TILE, KV_DIM), F32),
                        pltpu.VMEM((2 * BLOCK, GROUP * BLOCK), F32),
                        pltpu.VMEM((2 * BLOCK, GROUP * BLOCK), F32),
                        pltpu.VMEM((2, SEQ_TILE, D_MODEL), F32)],
        compiler_params=_params(2),
        name="l0_prompt",
    )(sinks, x, *[_operand(a) for a in resident])


def _l1_prompt(x, xmeta, *resident):
    n_b, seq, _ = x.shape
    n_tiles = seq // SEQ_TILE
    tile_in, tile_out, per_b = _prompt_specs(n_tiles)
    return pl.pallas_call(
        _l1_prompt_kernel,
        grid=(n_b, n_tiles + 1),
        in_specs=[tile_in, per_b(BLOCK, D_MODEL)] + [_resident(a, 2) for a in resident],
        out_specs=[tile_out, per_b(POOL_MAX - 1, D_MODEL)],
        out_shape=[jax.ShapeDtypeStruct((n_b, seq, D_MODEL), F32),
                   jax.ShapeDtypeStruct((n_b, POOL_MAX - 1, D_MODEL), F32)],
        scratch_shapes=[pltpu.VMEM((POOL_MAX + SEQ_TILE, D_MODEL), F32),
                        pltpu.VMEM((2, SEQ_TILE, D_MODEL), F32)],
        compiler_params=_params(2),
        name="l1_prompt",
    )(x, xmeta, *[_operand(a) for a in resident])


def _single_step(kernel, name, out_shape, *args):
    return pl.pallas_call(
        kernel,
        grid=(1,),
        in_specs=[_resident(a, 1) for a in args],
        out_specs=[pl.BlockSpec(o.shape, lambda i, n=len(o.shape): (0,) * n) for o in out_shape],
        out_shape=out_shape,
        compiler_params=_params(1),
        name=name,
    )(*[_operand(a) for a in args])


def _streamed_mlp(kernel, name, n_rows, outs, consts, w_up, w_down, layer):
    const = lambda shape: pl.BlockSpec(shape, lambda c, n=len(shape): (0,) * n)
    return pl.pallas_call(
        kernel,
        grid=(D_FF // FF_STREAM,),
        in_specs=[_resident(a, 1) for a in consts]
        + [pl.BlockSpec((None, D_MODEL, FF_STREAM), lambda c: (layer, 0, c)),
           pl.BlockSpec((None, FF_STREAM, D_MODEL), lambda c: (layer, c, 0))],
        out_specs=[const(o.shape) for o in outs]
        + [pl.BlockSpec((D_MODEL, FF_STREAM), lambda c: (0, c)),
           pl.BlockSpec((FF_STREAM, D_MODEL), lambda c: (c, 0))],
        out_shape=list(outs) + [jax.ShapeDtypeStruct((D_MODEL, D_FF), BF16),
                                jax.ShapeDtypeStruct((D_FF, D_MODEL), BF16)],
        scratch_shapes=[pltpu.VMEM((n_rows, D_MODEL), F32), pltpu.VMEM((n_rows, D_MODEL), BF16)],
        compiler_params=_params(1),
        name=name,
    )(*[_operand(a) for a in consts], w_up, w_down)


def _s0_attn(q4, ckt, cvt, kn, vn, knt, vnt, sk_rows):
    n_seq, _, _, wb = ckt.shape
    n_t = kn.shape[1]
    rows = n_t * GROUP
    blk = lambda *tail: pl.BlockSpec((SAMPLE_BB,) + tail, lambda i: (i,) + (0,) * len(tail))
    return pl.pallas_call(
        _s0_attn_kernel,
        grid=(n_seq // SAMPLE_BB,),
        in_specs=[blk(N_KV * rows, HEAD_DIM), blk(N_KV, HEAD_DIM, wb), blk(N_KV, HEAD_DIM, wb),
                  blk(n_t, KV_DIM), blk(n_t, KV_DIM), blk(N_KV, HEAD_DIM, n_t), blk(N_KV, HEAD_DIM, n_t),
                  pl.BlockSpec((N_KV, rows, 1), lambda i: (0, 0, 0))],
        out_specs=[blk(N_KV * rows, HEAD_DIM), blk(N_KV, HEAD_DIM, wb), blk(N_KV, HEAD_DIM, wb)],
        out_shape=[jax.ShapeDtypeStruct((n_seq, N_KV * rows, HEAD_DIM), F32),
                   jax.ShapeDtypeStruct(ckt.shape, F32), jax.ShapeDtypeStruct(cvt.shape, F32)],
        scratch_shapes=[pltpu.VMEM((SAMPLE_BB, SAMPLE_NEW, KV_DIM), F32),
                        pltpu.VMEM((SAMPLE_BB, SAMPLE_NEW, KV_DIM), F32)],
        compiler_params=_params(1),
        name="s0_attn",
    )(q4, ckt, cvt, kn, vn, knt, vnt, sk_rows)


def kernel(x_prompt, x_sample, state_conv, cache_k_win, cache_v_win, state_pool, meta_tokens, norm_mix, norm_mlp, w_in_even, conv_w, q_norm, k_norm, attn_sinks, w_out_even, w_pool, pool_scale, w_up, w_down):
    n_seq, n_t, _ = x_sample.shape
    wb = cache_k_win.shape[2]
    assert x_prompt.shape[1] % SEQ_TILE == 0 and x_prompt.shape[1] >= 2 * SEQ_TILE and n_seq % SAMPLE_BB == 0
    assert wb == WINDOW and n_t <= SAMPLE_NEW

    gmix = [_Stacked(norm_mix[:, None, :], l) for l in range(2)]
    gmlp = [_Stacked(norm_mlp[:, None, :], l) for l in range(2)]
    qg = jnp.tile(q_norm[0], N_HEADS)[None, :]
    kg = jnp.tile(k_norm[0], N_KV)[None, :]
    segq = jnp.kron(jnp.eye(N_HEADS, dtype=F32), jnp.full((HEAD_DIM, HEAD_DIM), 1.0 / HEAD_DIM, F32)).astype(BF16)
    segk = segq[:KV_DIM, :KV_DIM]
    cw = _Stacked(conv_w, 0)
    sinks = attn_sinks[0]
    meta_pad = jnp.pad(meta_tokens, ((META_PAD, 0), (0, 0)))
    sds = lambda *shape, dtype=F32: jax.ShapeDtypeStruct(shape, dtype)

    n_rows = n_t * n_seq
    ya, q4, kn_b, vn_b, conv_s, win = _single_step(
        _s0_pre_kernel, "s0_pre",
        [sds(n_rows, D_CONV), sds(n_seq, N_HEADS * n_t, HEAD_DIM), sds(n_seq, n_t, KV_DIM),
         sds(n_seq, n_t, KV_DIM), sds(n_seq, CONV_W - 1, D_CONV), sds(D_MODEL, D_IN_EVEN, dtype=BF16)],
        x_sample, _Stacked(state_conv, 0), gmix[0], _Stacked(w_in_even, 0), segq, segk, qg, kg, cw)
    sk_rows = jnp.tile(sinks.reshape(N_KV, 1, GROUP), (1, n_t, 1)).reshape(N_KV, n_t * GROUP, 1)
    to_dev = lambda a: jnp.transpose(a.reshape(n_seq, -1, N_KV, HEAD_DIM), (0, 2, 3, 1))
    o4, kst, vst = _s0_attn(q4, to_dev(cache_k_win), to_dev(cache_v_win), kn_b, vn_b, to_dev(kn_b), to_dev(vn_b),
                            sk_rows)
    k_s, v_s = (jnp.transpose(a, (0, 3, 1, 2))[None] for a in (kst, vst))
    xs2, wout, wup0, wdn0 = _streamed_mlp(
        _s0_post_kernel, "s0_post", n_rows, [sds(n_rows, D_MODEL), sds(D_MODEL, D_MODEL, dtype=BF16)],
        [x_sample, ya, o4, _Stacked(w_out_even, 0), gmlp[0]], w_up, w_down, 0)
    buf_t = jnp.transpose(state_pool[0], (1, 0, 2))
    y_sample, pool_s, wpool, wup1, wdn1 = _streamed_mlp(
        _s1_kernel, "s1", n_rows,
        [sds(n_seq, n_t, D_MODEL), sds(POOL_MAX - 1, n_seq, D_MODEL), sds(*w_pool.shape[1:], dtype=BF16)],
        [xs2, buf_t, gmix[1], _Stacked(w_pool, 0), pool_scale, gmlp[1]], w_up, w_down, 1)

    x2, x2_meta, conv_p, k_p, v_p = _l0_prompt(x_prompt, sinks, meta_pad, gmix[0], win, segq, segk, qg, kg, cw,
                                               wout, gmlp[0], wup0, wdn0)
    y_prompt, pool_p = _l1_prompt(x2, x2_meta, gmix[1], wpool, pool_scale, gmlp[1], wup1, wdn1)

    kv5 = lambda a: a.reshape(1, a.shape[0], WINDOW, N_KV, HEAD_DIM)
    return (y_prompt, y_sample, conv_p[None], conv_s[None], kv5(k_p), k_s, kv5(v_p), v_s,
            pool_p[None], jnp.transpose(pool_s, (1, 0, 2))[None])
```

```python
import functools
from typing import NamedTuple

import jax
import jax.numpy as jnp
from jax import lax
from jax.experimental import pallas as pl
from jax.experimental.pallas import tpu as pltpu

F32 = jnp.float32
BF16 = jnp.bfloat16

D_MODEL = 1024
D_CONV = 512
CONV_W = 3
HEAD_DIM = 64
N_HEADS = 8
N_KV = 2
GROUP = N_HEADS // N_KV
WINDOW = 128
BLOCK = 128
Q_DIM = N_HEADS * HEAD_DIM
KV_DIM = N_KV * HEAD_DIM
POOL_WINDOWS = (2, 4, 8, 16)
POOL_GROUP_DIM = D_MODEL // len(POOL_WINDOWS)
POOL_MAX = 16
D_FF = 4 * D_MODEL
D_IN_EVEN = 3 * D_CONV + Q_DIM + 2 * KV_DIM
N_META = 16
EPS = 1e-6
NEG = -1e30

META_PAD = BLOCK - N_META
SEQ_TILE = 512
FF_CHUNK = 1024
FF_STREAM = 512
CONV_HDR = 8
SAMPLE_BB = 16
SAMPLE_NEW = 16
VMEM_LIMIT = 56 * 1024 * 1024
MLP_YIELDS = 2 * (D_FF // FF_CHUNK)
L0_MIX_YIELDS = 1 + 2 * N_KV * (SEQ_TILE // BLOCK)


def _dot(a, b):
    return jnp.dot(a, b, preferred_element_type=F32)


def _dot_nt(a, b):
    return lax.dot_general(a, b, (((1,), (1,)), ((), ())), preferred_element_type=F32)


def _dot_tn(a, b):
    return lax.dot_general(a, b, (((0,), (0,)), ((), ())), preferred_element_type=F32)


def _rms(x, g):
    ms = jnp.mean(x * x, axis=-1, keepdims=True)
    return x * lax.rsqrt(ms + EPS) * g


def _head_rms(x, seg, g):
    ms = _dot((x * x).astype(BF16), seg)
    return x * lax.rsqrt(ms + EPS) * g


def _interleave(*steppers, shares=None):
    shares = shares or [1] * len(steppers)
    results = [None] * len(steppers)
    done = [0] * len(steppers)
    live = set(range(len(steppers)))
    while live:
        i = min(live, key=lambda k: ((done[k] + 1) / shares[k], k))
        try:
            next(steppers[i])
            done[i] += 1
        except StopIteration as finished:
            results[i] = finished.value
            live.discard(i)
    return results


def _mlp_steps(x, g, wup_ref, wdn_ref):
    xn = _rms(x, g).astype(BF16)
    acc = x
    for c in range(D_FF // FF_CHUNK):
        cols = slice(c * FF_CHUNK, (c + 1) * FF_CHUNK)
        h = _dot(xn, wup_ref[:, cols])
        a = jnp.square(jnp.maximum(h, 0.0)).astype(BF16)
        yield
        acc = acc + _dot(a, wdn_ref[cols, :])
        yield
    return acc


def _mlp(x, g, wup_ref, wdn_ref):
    return _interleave(_mlp_steps(x, g, wup_ref, wdn_ref))[0]


def _in_proj(x, g, win_ref, segq_ref, segk_ref, qg, kg):
    hn = _rms(x, g).astype(BF16)
    kv_col = 3 * D_CONV + Q_DIM
    half = x.shape[0] // 2
    z = _dot(hn, win_ref[:, 0:kv_col])
    kv = jnp.concatenate([_dot(hn[0:half], win_ref[:, kv_col:]), _dot(hn[half:], win_ref[:, kv_col:])], axis=0)
    xa = z[:, 0:D_CONV]
    gc = z[:, D_CONV:2 * D_CONV]
    gb = z[:, 2 * D_CONV:3 * D_CONV]
    q = z[:, 3 * D_CONV:]
    k = kv[:, 0:KV_DIM]
    v = kv[:, KV_DIM:]
    qn = _head_rms(q, segq_ref[...], qg) * (HEAD_DIM ** -0.5)
    kn = _head_rms(k, segk_ref[...], kg)
    return gc * xa, gb, qn, kn, v


def _l0_mixer(x, rows, sinks_ref, gmix, win_ref, segq_ref, segk_ref, qg, kg, cw_ref, wout_ref,
              u_scr, k_scr, v_scr, first_scr, band_scr):
    u, gb, qn, kn, v = _in_proj(x, gmix, win_ref, segq_ref, segk_ref, qg, kg)

    u_scr[CONV_HDR:CONV_HDR + rows, :] = u
    u1 = u_scr[CONV_HDR - 1:CONV_HDR - 1 + rows, :]
    u2 = u_scr[CONV_HDR - 2:CONV_HDR - 2 + rows, :]
    ya = gb * (u2 * cw_ref[0:1, :] + u1 * cw_ref[1:2, :] + u * cw_ref[2:3, :])
    k_scr[BLOCK:BLOCK + rows, :] = kn
    v_scr[BLOCK:BLOCK + rows, :] = v
    yield

    yb_blocks = []
    for i in range(rows // BLOCK):
        qb = qn[i * BLOCK:(i + 1) * BLOCK, :]
        bias = first_scr[...] if i == 0 else band_scr[...]
        heads_t = []
        for j in range(N_KV):
            lanes = slice(j * HEAD_DIM, (j + 1) * HEAD_DIM)
            k2 = k_scr[i * BLOCK:(i + 2) * BLOCK, lanes].astype(BF16)
            v2 = v_scr[i * BLOCK:(i + 2) * BLOCK, lanes].astype(BF16)
            qs = jnp.concatenate(
                [qb[:, (j * GROUP + g) * HEAD_DIM:(j * GROUP + g + 1) * HEAD_DIM] for g in range(GROUP)],
                axis=0).astype(BF16)
            sk = jnp.concatenate(
                [jnp.full((1, BLOCK), sinks_ref[j * GROUP + g], F32) for g in range(GROUP)], axis=1)
            st = _dot_nt(k2, qs) + bias
            m = jnp.maximum(jnp.max(st, axis=0, keepdims=True), sk)
            p = jnp.exp(st - m)
            den = jnp.sum(p, axis=0, keepdims=True) + jnp.exp(sk - m)
            yield
            ot = _dot_tn(v2, p.astype(BF16)) / den
            heads_t += [ot[:, g * BLOCK:(g + 1) * BLOCK] for g in range(GROUP)]
            yield
        yb_blocks.append(jnp.concatenate(heads_t, axis=0).T)
    yb = yb_blocks[0] if len(yb_blocks) == 1 else jnp.concatenate(yb_blocks, axis=0)

    u_scr[CONV_HDR - 2:CONV_HDR, :] = u_scr[CONV_HDR - 2 + rows:CONV_HDR + rows, :]
    k_scr[0:BLOCK, :] = k_scr[rows:rows + BLOCK, :]
    v_scr[0:BLOCK, :] = v_scr[rows:rows + BLOCK, :]

    mix = jnp.concatenate([ya, yb], axis=-1).astype(BF16)
    return x + _dot(mix, wout_ref[...])


def _pipeline_steps(s, x_ref, y_ref, x1_scr, mixer, mlp):
    slot = lax.rem(s, 2)

    def mix_tile():
        x1_scr[slot] = yield from mixer(x_ref[...], SEQ_TILE)

    def mlp_tile():
        y_ref[...] = yield from mlp(x1_scr[1 - slot])

    return mlp_tile, mix_tile


def _l0_prompt_kernel(sinks_ref, x_ref, meta_ref, gmix_ref, win_ref, segq_ref, segk_ref, qg_ref, kg_ref,
                      cw_ref, wout_ref, gmlp_ref, wup_ref, wdn_ref,
                      y_ref, ymeta_ref, conv_ref, kout_ref, vout_ref,
                      u_scr, k_scr, v_scr, first_scr, band_scr, x1_scr):
    s = pl.program_id(1)
    last = pl.num_programs(1) - 1
    mixer = functools.partial(
        _l0_mixer, sinks_ref=sinks_ref, gmix=gmix_ref[...], win_ref=win_ref, segq_ref=segq_ref,
        segk_ref=segk_ref, qg=qg_ref[...], kg=kg_ref[...], cw_ref=cw_ref, wout_ref=wout_ref,
        u_scr=u_scr, k_scr=k_scr, v_scr=v_scr, first_scr=first_scr, band_scr=band_scr)
    mlp = functools.partial(_mlp_steps, g=gmlp_ref[...], wup_ref=wup_ref, wdn_ref=wdn_ref)
    mlp_tile, mix_tile = _pipeline_steps(s, x_ref, y_ref, x1_scr, mixer, mlp)

    def band_bias(first_valid_key):
        c = lax.broadcasted_iota(jnp.int32, (2 * BLOCK, GROUP * BLOCK), 0)
        q = lax.broadcasted_iota(jnp.int32, (2 * BLOCK, GROUP * BLOCK), 1) & (BLOCK - 1)
        return jnp.where((c >= q) & (c <= q + WINDOW) & (c >= first_valid_key), 0.0, NEG)

    @pl.when(s == 0)
    def _start():
        band_scr[...] = band_bias(0)
        u_scr[0:CONV_HDR, :] = jnp.zeros((CONV_HDR, D_CONV), F32)
        k_scr[0:BLOCK, :] = jnp.zeros((BLOCK, KV_DIM), F32)
        v_scr[0:BLOCK, :] = jnp.zeros((BLOCK, KV_DIM), F32)
        first_scr[...] = band_bias(BLOCK + META_PAD)
        x1_meta, = _interleave(mixer(meta_ref[...], BLOCK))
        ymeta_ref[...], = _interleave(mlp(x1_meta))
        first_scr[...] = band_bias(META_PAD)
        _interleave(mix_tile())
        first_scr[...] = band_scr[...]

    @pl.when((s > 0) & (s < last))
    def _steady():
        _interleave(mlp_tile(), mix_tile(), shares=(MLP_YIELDS, L0_MIX_YIELDS))

    @pl.when(s == last)
    def _drain():
        _interleave(mlp_tile())

    @pl.when(s == last - 1)
    def _state():
        conv_ref[...] = u_scr[CONV_HDR - 2:CONV_HDR, :]
        kout_ref[...] = k_scr[0:BLOCK, :]
        vout_ref[...] = v_scr[0:BLOCK, :]


def _l1_mixer(x, rows, is_meta, gmix, wpool_ref, pscale, h_scr):
    h = _rms(x, gmix)
    h_scr[POOL_MAX:POOL_MAX + rows, :] = h
    ys = []
    for gi, w in enumerate(POOL_WINDOWS):
        cols = slice(gi * POOL_GROUP_DIM, (gi + 1) * POOL_GROUP_DIM)
        hg = h[:, cols]
        acc = hg
        for j in range(1, w):
            acc = acc + h_scr[POOL_MAX - j:POOL_MAX - j + rows, cols]
        if is_meta:
            r = lax.broadcasted_iota(jnp.int32, (rows, 1), 0)
            pooled = acc / jnp.clip(r - (META_PAD - 1), 1, w).astype(F32)
        else:
            pooled = acc * (1.0 / w)
        ys.append(_dot((pooled - hg).astype(BF16), wpool_ref[gi]))
        yield
    h_scr[0:POOL_MAX, :] = h_scr[rows:rows + POOL_MAX, :]
    return x + jnp.concatenate(ys, axis=-1) * pscale


def _l1_prompt_kernel(x_ref, meta_ref, gmix_ref, wpool_ref, pscale_ref, gmlp_ref, wup_ref, wdn_ref,
                      y_ref, pool_ref, h_scr, x1_scr):
    s = pl.program_id(1)
    last = pl.num_programs(1) - 1
    mixer = functools.partial(_l1_mixer, gmix=gmix_ref[...], wpool_ref=wpool_ref, pscale=pscale_ref[...],
                              h_scr=h_scr)
    mlp = functools.partial(_mlp_steps, g=gmlp_ref[...], wup_ref=wup_ref, wdn_ref=wdn_ref)
    mlp_tile, mix_tile = _pipeline_steps(s, x_ref, y_ref, x1_scr,
                                         functools.partial(mixer, is_meta=False), mlp)

    @pl.when(s == 0)
    def _start():
        h_scr[0:POOL_MAX, :] = jnp.zeros((POOL_MAX, D_MODEL), F32)
        _interleave(mixer(meta_ref[...], BLOCK, True))
        _interleave(mix_tile())

    @pl.when((s > 0) & (s < last))
    def _steady():
        _interleave(mlp_tile(), mix_tile(), shares=(MLP_YIELDS, len(POOL_WINDOWS)))

    @pl.when(s == last)
    def _drain():
        _interleave(mlp_tile())

    @pl.when(s == last - 1)
    def _state():
        pool_ref[...] = h_scr[1:POOL_MAX, :]


def _time_major(ref):
    return jnp.concatenate([ref[:, t, :] for t in range(ref.shape[1])], axis=0)


def _s0_pre_kernel(x_ref, st_ref, gmix_ref, win_ref, segq_ref, segk_ref, qg_ref, kg_ref, cw_ref,
                   ya_ref, q_ref, k_ref, v_ref, conv_ref, win_bf_ref):
    n_seq, n_t, _ = x_ref.shape
    win_bf_ref[...] = win_ref[...].astype(BF16)
    u, gb, qn, kn, v = _in_proj(_time_major(x_ref), gmix_ref[...], win_bf_ref, segq_ref, segk_ref, qg_ref[...],
                                kg_ref[...])
    rows = lambda a, t: a[t * n_seq:(t + 1) * n_seq, :]
    ue = [st_ref[:, i, :] for i in range(CONV_W - 1)] + [rows(u, t) for t in range(n_t)]
    for t in range(n_t):
        y = ue[t] * cw_ref[0:1, :] + ue[t + 1] * cw_ref[1:2, :] + ue[t + 2] * cw_ref[2:3, :]
        ya_ref[t * n_seq:(t + 1) * n_seq, :] = rows(gb, t) * y
        k_ref[:, t, :] = rows(kn, t)
        v_ref[:, t, :] = rows(v, t)
        for h in range(N_HEADS):
            q_ref[:, (h // GROUP) * n_t * GROUP + t * GROUP + h % GROUP, :] = (
                rows(qn, t)[:, h * HEAD_DIM:(h + 1) * HEAD_DIM])
    for i in range(CONV_W - 1):
        conv_ref[:, i, :] = ue[n_t + i]


def _s0_attn_kernel(q_ref, ckt_ref, cvt_ref, kn_ref, vn_ref, sk_ref, o_ref, kst_ref, vst_ref, kn_scr, vn_scr):
    bb, n_t = kn_ref.shape[0], kn_ref.shape[1]
    wb = ckt_ref.shape[3]
    rows = n_t * GROUP
    t = lax.broadcasted_iota(jnp.int32, (rows, wb), 0) // GROUP
    c = lax.broadcasted_iota(jnp.int32, (rows, wb), 1)
    bias_c = jnp.where(wb + t - c <= WINDOW, 0.0, NEG)[None]
    t = lax.broadcasted_iota(jnp.int32, (rows, SAMPLE_NEW), 0) // GROUP
    k = lax.broadcasted_iota(jnp.int32, (rows, SAMPLE_NEW), 1)
    bias_n = jnp.where((k <= t) & (k < n_t), 0.0, NEG)[None]
    kn_scr[...] = jnp.zeros(kn_scr.shape, F32)
    vn_scr[...] = jnp.zeros(vn_scr.shape, F32)
    kn_scr[:, 0:n_t, :] = kn_ref[...]
    vn_scr[:, 0:n_t, :] = vn_ref[...]
    for old_ref, new_scr, out_ref in ((ckt_ref, kn_scr, kst_ref), (cvt_ref, vn_scr, vst_ref)):
        new_t = jnp.swapaxes(new_scr[:, 0:8, :], 1, 2).reshape(bb, N_KV, HEAD_DIM, 8)
        out_ref[...] = jnp.concatenate([old_ref[:, :, :, n_t:], new_t[:, :, :, 0:n_t]], axis=-1)
    for j in range(N_KV):
        lanes = slice(j * HEAD_DIM, (j + 1) * HEAD_DIM)
        q = q_ref[:, j * rows:(j + 1) * rows, :].astype(BF16)
        sk = sk_ref[j][None]
        s_c = jnp.einsum('bqd,bdk->bqk', q, ckt_ref[:, j].astype(BF16), preferred_element_type=F32) + bias_c
        s_n = jnp.einsum('bqd,bkd->bqk', q, kn_scr[:, :, lanes].astype(BF16),
                         preferred_element_type=F32) + bias_n
        m = jnp.maximum(jnp.maximum(jnp.max(s_c, axis=-1, keepdims=True),
                                    jnp.max(s_n, axis=-1, keepdims=True)), sk)
        p_c = jnp.exp(s_c - m)
        p_n = jnp.exp(s_n - m)
        den = jnp.sum(p_c, axis=-1, keepdims=True) + jnp.sum(p_n, axis=-1, keepdims=True) + jnp.exp(sk - m)
        o = (jnp.einsum('bqk,bdk->bqd', p_c.astype(BF16), cvt_ref[:, j].astype(BF16),
                        preferred_element_type=F32)
             + jnp.einsum('bqk,bkd->bqd', p_n.astype(BF16), vn_scr[:, :, lanes].astype(BF16),
                          preferred_element_type=F32))
        o_ref[:, j * rows:(j + 1) * rows, :] = o / den


def _mlp_stream_step(c, wup_ref, wdn_ref, y_ref, wup_bf_ref, wdn_bf_ref, x1_scr, xn_scr):
    wup_bf_ref[...] = wup_ref[...].astype(BF16)
    wdn_bf_ref[...] = wdn_ref[...].astype(BF16)
    a = jnp.square(jnp.maximum(_dot(xn_scr[...], wup_bf_ref[...]), 0.0)).astype(BF16)
    x1_scr[...] += _dot(a, wdn_bf_ref[...])
    if y_ref is not None:
        @pl.when(c == pl.num_programs(0) - 1)
        def _done():
            y_ref[...] = x1_scr[...]


def _s0_post_kernel(x_ref, ya_ref, o_ref, wout_ref, gmlp_ref, wup_ref, wdn_ref,
                    y_ref, wout_bf_ref, wup_bf_ref, wdn_bf_ref, x1_scr, xn_scr):
    c = pl.program_id(0)

    @pl.when(c == 0)
    def _mix():
        n_t = x_ref.shape[1]
        wout_bf_ref[...] = wout_ref[...].astype(BF16)
        yb = jnp.concatenate(
            [jnp.concatenate([o_ref[:, (h // GROUP) * n_t * GROUP + t * GROUP + h % GROUP, :]
                              for h in range(N_HEADS)], axis=-1) for t in range(n_t)], axis=0)
        mix = jnp.concatenate([ya_ref[...], yb], axis=-1).astype(BF16)
        x1 = _time_major(x_ref) + _dot(mix, wout_bf_ref[...])
        x1_scr[...] = x1
        xn_scr[...] = _rms(x1, gmlp_ref[...]).astype(BF16)

    _mlp_stream_step(c, wup_ref, wdn_ref, y_ref, wup_bf_ref, wdn_bf_ref, x1_scr, xn_scr)


def _s1_kernel(x_ref, buf_ref, gmix_ref, wpool_ref, pscale_ref, gmlp_ref, wup_ref, wdn_ref,
               y_ref, pool_ref, wpool_bf_ref, wup_bf_ref, wdn_bf_ref, x1_scr, xn_scr):
    c = pl.program_id(0)
    n_buf, n_seq, _ = buf_ref.shape
    n_t = x_ref.shape[0] // n_seq

    @pl.when(c == 0)
    def _mix():
        wpool_bf_ref[...] = wpool_ref[...].astype(BF16)
        x = x_ref[...]
        h = _rms(x, gmix_ref[...])
        he = [buf_ref[i] for i in range(n_buf)] + [h[t * n_seq:(t + 1) * n_seq, :] for t in range(n_t)]
        for i in range(n_buf):
            pool_ref[i] = he[n_t + i]
        ys = []
        for gi, w in enumerate(POOL_WINDOWS):
            cols = slice(gi * POOL_GROUP_DIM, (gi + 1) * POOL_GROUP_DIM)
            dl = []
            for t in range(n_t):
                acc = he[n_buf + t][:, cols]
                for j in range(1, w):
                    acc = acc + he[n_buf + t - j][:, cols]
                dl.append(acc * (1.0 / w) - he[n_buf + t][:, cols])
            ys.append(_dot(jnp.concatenate(dl, axis=0).astype(BF16), wpool_bf_ref[gi]))
        x1 = x + jnp.concatenate(ys, axis=-1) * pscale_ref[...]
        x1_scr[...] = x1
        xn_scr[...] = _rms(x1, gmlp_ref[...]).astype(BF16)

    _mlp_stream_step(c, wup_ref, wdn_ref, None, wup_bf_ref, wdn_bf_ref, x1_scr, xn_scr)

    @pl.when(c == pl.num_programs(0) - 1)
    def _done():
        for t in range(n_t):
            y_ref[:, t, :] = x1_scr[t * n_seq:(t + 1) * n_seq, :]


class _Stacked(NamedTuple):
    array: jax.Array
    layer: int


def _operand(arg):
    return arg.array if isinstance(arg, _Stacked) else arg


def _resident(arg, n_grid):
    if isinstance(arg, _Stacked):
        shape, idx = (None,) + arg.array.shape[1:], (arg.layer,) + (0,) * (arg.array.ndim - 1)
    else:
        shape, idx = arg.shape, (0,) * arg.ndim
    index_map = {1: lambda i: idx, 2: lambda i, j: idx}[n_grid]
    return pl.BlockSpec(shape, index_map, pipeline_mode=pl.Buffered(1))


def _params(n_grid):
    return pltpu.CompilerParams(dimension_semantics=("arbitrary",) * n_grid, vmem_limit_bytes=VMEM_LIMIT)


def _prompt_specs(n_tiles):
    tile_in = pl.BlockSpec((None, SEQ_TILE, D_MODEL), lambda b, s: (b, jnp.minimum(s, n_tiles - 1), 0))
    tile_out = pl.BlockSpec((None, SEQ_TILE, D_MODEL), lambda b, s: (b, jnp.maximum(s - 1, 0), 0))
    per_b = lambda r, c: pl.BlockSpec((None, r, c), lambda b, s: (b, 0, 0))
    return tile_in, tile_out, per_b


def _l0_prompt(x, sinks, *resident):
    n_b, seq, _ = x.shape
    n_tiles = seq // SEQ_TILE
    tile_in, tile_out, per_b = _prompt_specs(n_tiles)
    return pl.pallas_call(
        _l0_prompt_kernel,
        grid=(n_b, n_tiles + 1),
        in_specs=[pl.BlockSpec(memory_space=pltpu.SMEM), tile_in] + [_resident(a, 2) for a in resident],
        out_specs=[tile_out, per_b(BLOCK, D_MODEL), per_b(CONV_W - 1, D_CONV), per_b(WINDOW, KV_DIM),
                   per_b(WINDOW, KV_DIM)],
        out_shape=[jax.ShapeDtypeStruct((n_b, seq, D_MODEL), F32),
                   jax.ShapeDtypeStruct((n_b, BLOCK, D_MODEL), F32),
                   jax.ShapeDtypeStruct((n_b, CONV_W - 1, D_CONV), F32),
                   jax.ShapeDtypeStruct((n_b, WINDOW, KV_DIM), F32),
                   jax.ShapeDtypeStruct((n_b, WINDOW, KV_DIM), F32)],
        scratch_shapes=[pltpu.VMEM((CONV_HDR + SEQ_TILE, D_CONV), F32),
                        pltpu.VMEM((BLOCK + SEQ_TILE, KV_DIM), F32),
                        pltpu.VMEM((BLOCK + SEQ_TILE, KV_DIM), F32),
                        pltpu.VMEM((2 * BLOCK, GROUP * BLOCK), F32),
                        pltpu.VMEM((2 * BLOCK, GROUP * BLOCK), F32),
                        pltpu.VMEM((2, SEQ_TILE, D_MODEL), F32)],
        compiler_params=_params(2),
        name="l0_prompt",
    )(sinks, x, *[_operand(a) for a in resident])


def _l1_prompt(x, xmeta, *resident):
    n_b, seq, _ = x.shape
    n_tiles = seq // SEQ_TILE
    tile_in, tile_out, per_b = _prompt_specs(n_tiles)
    return pl.pallas_call(
        _l1_prompt_kernel,
        grid=(n_b, n_tiles + 1),
        in_specs=[tile_in, per_b(BLOCK, D_MODEL)] + [_resident(a, 2) for a in resident],
        out_specs=[tile_out, per_b(POOL_MAX - 1, D_MODEL)],
        out_shape=[jax.ShapeDtypeStruct((n_b, seq, D_MODEL), F32),
                   jax.ShapeDtypeStruct((n_b, POOL_MAX - 1, D_MODEL), F32)],
        scratch_shapes=[pltpu.VMEM((POOL_MAX + SEQ_TILE, D_MODEL), F32),
                        pltpu.VMEM((2, SEQ_TILE, D_MODEL), F32)],
        compiler_params=_params(2),
        name="l1_prompt",
    )(x, xmeta, *[_operand(a) for a in resident])


def _single_step(kernel, name, out_shape, *args):
    return pl.pallas_call(
        kernel,
        grid=(1,),
        in_specs=[_resident(a, 1) for a in args],
        out_specs=[pl.BlockSpec(o.shape, lambda i, n=len(o.shape): (0,) * n) for o in out_shape],
        out_shape=out_shape,
        compiler_params=_params(1),
        name=name,
    )(*[_operand(a) for a in args])


def _streamed_mlp(kernel, name, n_rows, outs, consts, w_up, w_down, layer):
    const = lambda shape: pl.BlockSpec(shape, lambda c, n=len(shape): (0,) * n)
    return pl.pallas_call(
        kernel,
        grid=(D_FF // FF_STREAM,),
        in_specs=[_resident(a, 1) for a in consts]
        + [pl.BlockSpec((None, D_MODEL, FF_STREAM), lambda c: (layer, 0, c)),
           pl.BlockSpec((None, FF_STREAM, D_MODEL), lambda c: (layer, c, 0))],
        out_specs=[const(o.shape) for o in outs]
        + [pl.BlockSpec((D_MODEL, FF_STREAM), lambda c: (0, c)),
           pl.BlockSpec((FF_STREAM, D_MODEL), lambda c: (c, 0))],
        out_shape=list(outs) + [jax.ShapeDtypeStruct((D_MODEL, D_FF), BF16),
                                jax.ShapeDtypeStruct((D_FF, D_MODEL), BF16)],
        scratch_shapes=[pltpu.VMEM((n_rows, D_MODEL), F32), pltpu.VMEM((n_rows, D_MODEL), BF16)],
        compiler_params=_params(1),
        name=name,
    )(*[_operand(a) for a in consts], w_up, w_down)


def _s0_attn(q4, ckt, cvt, kn, vn, sk_rows):
    n_seq, _, _, wb = ckt.shape
    n_t = kn.shape[1]
    rows = n_t * GROUP
    blk = lambda *tail: pl.BlockSpec((SAMPLE_BB,) + tail, lambda i: (i,) + (0,) * len(tail))
    return pl.pallas_call(
        _s0_attn_kernel,
        grid=(n_seq // SAMPLE_BB,),
        in_specs=[blk(N_KV * rows, HEAD_DIM), blk(N_KV, HEAD_DIM, wb), blk(N_KV, HEAD_DIM, wb),
                  blk(n_t, KV_DIM), blk(n_t, KV_DIM), pl.BlockSpec((N_KV, rows, 1), lambda i: (0, 0, 0))],
        out_specs=[blk(N_KV * rows, HEAD_DIM), blk(N_KV, HEAD_DIM, wb), blk(N_KV, HEAD_DIM, wb)],
        out_shape=[jax.ShapeDtypeStruct((n_seq, N_KV * rows, HEAD_DIM), F32),
                   jax.ShapeDtypeStruct(ckt.shape, F32), jax.ShapeDtypeStruct(cvt.shape, F32)],
        scratch_shapes=[pltpu.VMEM((SAMPLE_BB, SAMPLE_NEW, KV_DIM), F32),
                        pltpu.VMEM((SAMPLE_BB, SAMPLE_NEW, KV_DIM), F32)],
        compiler_params=_params(1),
        name="s0_attn",
    )(q4, ckt, cvt, kn, vn, sk_rows)


def kernel(x_prompt, x_sample, state_conv, cache_k_win, cache_v_win, state_pool, meta_tokens, norm_mix, norm_mlp, w_in_even, conv_w, q_norm, k_norm, attn_sinks, w_out_even, w_pool, pool_scale, w_up, w_down):
    n_seq, n_t, _ = x_sample.shape
    wb = cache_k_win.shape[2]
    assert x_prompt.shape[1] % SEQ_TILE == 0 and x_prompt.shape[1] >= 2 * SEQ_TILE and n_seq % SAMPLE_BB == 0
    assert wb == WINDOW and n_t <= SAMPLE_NEW

    gmix = [_Stacked(norm_mix[:, None, :], l) for l in range(2)]
    gmlp = [_Stacked(norm_mlp[:, None, :], l) for l in range(2)]
    qg = jnp.tile(q_norm[0], N_HEADS)[None, :]
    kg = jnp.tile(k_norm[0], N_KV)[None, :]
    segq = jnp.kron(jnp.eye(N_HEADS, dtype=F32), jnp.full((HEAD_DIM, HEAD_DIM), 1.0 / HEAD_DIM, F32)).astype(BF16)
    segk = segq[:KV_DIM, :KV_DIM]
    cw = _Stacked(conv_w, 0)
    sinks = attn_sinks[0]
    meta_pad = jnp.pad(meta_tokens, ((META_PAD, 0), (0, 0)))
    sds = lambda *shape, dtype=F32: jax.ShapeDtypeStruct(shape, dtype)

    n_rows = n_t * n_seq
    ya, q4, kn_b, vn_b, conv_s, win = _single_step(
        _s0_pre_kernel, "s0_pre",
        [sds(n_rows, D_CONV), sds(n_seq, N_HEADS * n_t, HEAD_DIM), sds(n_seq, n_t, KV_DIM),
         sds(n_seq, n_t, KV_DIM), sds(n_seq, CONV_W - 1, D_CONV), sds(D_MODEL, D_IN_EVEN, dtype=BF16)],
        x_sample, _Stacked(state_conv, 0), gmix[0], _Stacked(w_in_even, 0), segq, segk, qg, kg, cw)
    sk_rows = jnp.tile(sinks.reshape(N_KV, 1, GROUP), (1, n_t, 1)).reshape(N_KV, n_t * GROUP, 1)
    to_dev = lambda a: jnp.transpose(a.reshape(n_seq, -1, N_KV, HEAD_DIM), (0, 2, 3, 1))
    o4, kst, vst = _s0_attn(q4, to_dev(cache_k_win), to_dev(cache_v_win), kn_b, vn_b, sk_rows)
    k_s, v_s = (jnp.transpose(a, (0, 3, 1, 2))[None] for a in (kst, vst))
    xs2, wout, wup0, wdn0 = _streamed_mlp(
        _s0_post_kernel, "s0_post", n_rows, [sds(n_rows, D_MODEL), sds(D_MODEL, D_MODEL, dtype=BF16)],
        [x_sample, ya, o4, _Stacked(w_out_even, 0), gmlp[0]], w_up, w_down, 0)
    buf_t = jnp.transpose(state_pool[0], (1, 0, 2))
    y_sample, pool_s, wpool, wup1, wdn1 = _streamed_mlp(
        _s1_kernel, "s1", n_rows,
        [sds(n_seq, n_t, D_MODEL), sds(POOL_MAX - 1, n_seq, D_MODEL), sds(*w_pool.shape[1:], dtype=BF16)],
        [xs2, buf_t, gmix[1], _Stacked(w_pool, 0), pool_scale, gmlp[1]], w_up, w_down, 1)

    x2, x2_meta, conv_p, k_p, v_p = _l0_prompt(x_prompt, sinks, meta_pad, gmix[0], win, segq, segk, qg, kg, cw,
                                               wout, gmlp[0], wup0, wdn0)
    y_prompt, pool_p = _l1_prompt(x2, x2_meta, gmix[1], wpool, pool_scale, gmlp[1], wup1, wdn1)

    kv5 = lambda a: a.reshape(1, a.shape[0], WINDOW, N_KV, HEAD_DIM)
    return (y_prompt, y_sample, conv_p[None], conv_s[None], kv5(k_p), k_s, kv5(v_p), v_s,
            pool_p[None], jnp.transpose(pool_s, (1, 0, 2))[None])
```

```python
import functools
from typing import NamedTuple

import jax
import jax.numpy as jnp
import numpy as np
from jax import lax
from jax.experimental import pallas as pl
from jax.experimental.pallas import tpu as pltpu

F32 = jnp.float32
BF16 = jnp.bfloat16

D_MODEL = 1024
D_CONV = 512
CONV_W = 3
HEAD_DIM = 64
N_HEADS = 8
N_KV = 2
GROUP = N_HEADS // N_KV
WINDOW = 128
BLOCK = 128
Q_DIM = N_HEADS * HEAD_DIM
KV_DIM = N_KV * HEAD_DIM
POOL_WINDOWS = (2, 4, 8, 16)
POOL_GROUP_DIM = D_MODEL // len(POOL_WINDOWS)
POOL_MAX = 16
D_FF = 4 * D_MODEL
D_IN_EVEN = 3 * D_CONV + Q_DIM + 2 * KV_DIM
N_META = 16
EPS = 1e-6
NEG = -1e30

META_PAD = BLOCK - N_META
SEQ_TILE = 512
FF_CHUNK = 1024
FF_STREAM = 512
CONV_HDR = 8
SAMPLE_BB = 16
SAMPLE_NEW = 8
VMEM_LIMIT = 56 * 1024 * 1024
MLP_YIELDS = 2 * (D_FF // FF_CHUNK)
L0_MIX_YIELDS = 1 + 2 * N_KV * (SEQ_TILE // BLOCK)


def _dot(a, b):
    return jnp.dot(a, b, preferred_element_type=F32)


def _dot_nt(a, b):
    return lax.dot_general(a, b, (((1,), (1,)), ((), ())), preferred_element_type=F32)


def _dot_tn(a, b):
    return lax.dot_general(a, b, (((0,), (0,)), ((), ())), preferred_element_type=F32)


def _rms(x, g):
    ms = jnp.mean(x * x, axis=-1, keepdims=True)
    return x * lax.rsqrt(ms + EPS) * g


def _head_rms(x, seg, g):
    ms = _dot((x * x).astype(BF16), seg)
    return x * lax.rsqrt(ms + EPS) * g


def _interleave(*steppers, shares=None):
    shares = shares or [1] * len(steppers)
    results = [None] * len(steppers)
    done = [0] * len(steppers)
    live = set(range(len(steppers)))
    while live:
        i = min(live, key=lambda k: ((done[k] + 1) / shares[k], k))
        try:
            next(steppers[i])
            done[i] += 1
        except StopIteration as finished:
            results[i] = finished.value
            live.discard(i)
    return results


def _mlp_steps(x, g, wup_ref, wdn_ref):
    xn = _rms(x, g).astype(BF16)
    acc = x
    for c in range(D_FF // FF_CHUNK):
        cols = slice(c * FF_CHUNK, (c + 1) * FF_CHUNK)
        h = _dot(xn, wup_ref[:, cols])
        a = jnp.square(jnp.maximum(h, 0.0)).astype(BF16)
        yield
        acc = acc + _dot(a, wdn_ref[cols, :])
        yield
    return acc


def _mlp(x, g, wup_ref, wdn_ref):
    return _interleave(_mlp_steps(x, g, wup_ref, wdn_ref))[0]


def _in_proj(x, g, win_ref, segq_ref, segk_ref, qg, kg):
    hn = _rms(x, g).astype(BF16)
    kv_col = 3 * D_CONV + Q_DIM
    half = x.shape[0] // 2
    z = _dot(hn, win_ref[:, 0:kv_col])
    kv = jnp.concatenate([_dot(hn[0:half], win_ref[:, kv_col:]), _dot(hn[half:], win_ref[:, kv_col:])], axis=0)
    xa = z[:, 0:D_CONV]
    gc = z[:, D_CONV:2 * D_CONV]
    gb = z[:, 2 * D_CONV:3 * D_CONV]
    q = z[:, 3 * D_CONV:]
    k = kv[:, 0:KV_DIM]
    v = kv[:, KV_DIM:]
    qn = _head_rms(q, segq_ref[...], qg) * (HEAD_DIM ** -0.5)
    kn = _head_rms(k, segk_ref[...], kg)
    return gc * xa, gb, qn, kn, v


def _l0_mixer(x, rows, sinks_ref, gmix, win_ref, segq_ref, segk_ref, qg, kg, cw_ref, wout_ref,
              u_scr, k_scr, v_scr, first_scr, band_scr):
    u, gb, qn, kn, v = _in_proj(x, gmix, win_ref, segq_ref, segk_ref, qg, kg)

    u_scr[CONV_HDR:CONV_HDR + rows, :] = u
    u1 = u_scr[CONV_HDR - 1:CONV_HDR - 1 + rows, :]
    u2 = u_scr[CONV_HDR - 2:CONV_HDR - 2 + rows, :]
    ya = gb * (u2 * cw_ref[0:1, :] + u1 * cw_ref[1:2, :] + u * cw_ref[2:3, :])
    k_scr[BLOCK:BLOCK + rows, :] = kn
    v_scr[BLOCK:BLOCK + rows, :] = v
    yield

    yb_blocks = []
    for i in range(rows // BLOCK):
        qb = qn[i * BLOCK:(i + 1) * BLOCK, :]
        bias = first_scr[...] if i == 0 else band_scr[...]
        heads_t = []
        for j in range(N_KV):
            lanes = slice(j * HEAD_DIM, (j + 1) * HEAD_DIM)
            k2 = k_scr[i * BLOCK:(i + 2) * BLOCK, lanes].astype(BF16)
            v2 = v_scr[i * BLOCK:(i + 2) * BLOCK, lanes].astype(BF16)
            qs = jnp.concatenate(
                [qb[:, (j * GROUP + g) * HEAD_DIM:(j * GROUP + g + 1) * HEAD_DIM] for g in range(GROUP)],
                axis=0).astype(BF16)
            sk = jnp.concatenate(
                [jnp.full((1, BLOCK), sinks_ref[j * GROUP + g], F32) for g in range(GROUP)], axis=1)
            st = _dot_nt(k2, qs) + bias
            m = jnp.maximum(jnp.max(st, axis=0, keepdims=True), sk)
            p = jnp.exp(st - m)
            den = jnp.sum(p, axis=0, keepdims=True) + jnp.exp(sk - m)
            yield
            ot = _dot_tn(v2, p.astype(BF16)) / den
            heads_t += [ot[:, g * BLOCK:(g + 1) * BLOCK] for g in range(GROUP)]
            yield
        yb_blocks.append(jnp.concatenate(heads_t, axis=0).T)
    yb = yb_blocks[0] if len(yb_blocks) == 1 else jnp.concatenate(yb_blocks, axis=0)

    u_scr[CONV_HDR - 2:CONV_HDR, :] = u_scr[CONV_HDR - 2 + rows:CONV_HDR + rows, :]
    k_scr[0:BLOCK, :] = k_scr[rows:rows + BLOCK, :]
    v_scr[0:BLOCK, :] = v_scr[rows:rows + BLOCK, :]

    mix = jnp.concatenate([ya, yb], axis=-1).astype(BF16)
    return x + _dot(mix, wout_ref[...])


def _pipeline_steps(s, x_ref, y_ref, x1_scr, mixer, mlp):
    slot = lax.rem(s, 2)

    def mix_tile():
        x1_scr[slot] = yield from mixer(x_ref[...], SEQ_TILE)

    def mlp_tile():
        y_ref[...] = yield from mlp(x1_scr[1 - slot])

    return mlp_tile, mix_tile


def _l0_prompt_kernel(sinks_ref, x_ref, meta_ref, gmix_ref, win_ref, segq_ref, segk_ref, qg_ref, kg_ref,
                      cw_ref, wout_ref, gmlp_ref, wup_ref, wdn_ref,
                      y_ref, ymeta_ref, conv_ref, kout_ref, vout_ref,
                      u_scr, k_scr, v_scr, first_scr, band_scr, x1_scr):
    s = pl.program_id(1)
    last = pl.num_programs(1) - 1
    mixer = functools.partial(
        _l0_mixer, sinks_ref=sinks_ref, gmix=gmix_ref[...], win_ref=win_ref, segq_ref=segq_ref,
        segk_ref=segk_ref, qg=qg_ref[...], kg=kg_ref[...], cw_ref=cw_ref, wout_ref=wout_ref,
        u_scr=u_scr, k_scr=k_scr, v_scr=v_scr, first_scr=first_scr, band_scr=band_scr)
    mlp = functools.partial(_mlp_steps, g=gmlp_ref[...], wup_ref=wup_ref, wdn_ref=wdn_ref)
    mlp_tile, mix_tile = _pipeline_steps(s, x_ref, y_ref, x1_scr, mixer, mlp)

    def band_bias(first_valid_key):
        c = lax.broadcasted_iota(jnp.int32, (2 * BLOCK, GROUP * BLOCK), 0)
        q = lax.broadcasted_iota(jnp.int32, (2 * BLOCK, GROUP * BLOCK), 1) & (BLOCK - 1)
        return jnp.where((c >= q) & (c <= q + WINDOW) & (c >= first_valid_key), 0.0, NEG)

    @pl.when(s == 0)
    def _start():
        band_scr[...] = band_bias(0)
        u_scr[0:CONV_HDR, :] = jnp.zeros((CONV_HDR, D_CONV), F32)
        k_scr[0:BLOCK, :] = jnp.zeros((BLOCK, KV_DIM), F32)
        v_scr[0:BLOCK, :] = jnp.zeros((BLOCK, KV_DIM), F32)
        first_scr[...] = band_bias(BLOCK + META_PAD)
        x_meta = jnp.concatenate([jnp.zeros((META_PAD, D_MODEL), F32), meta_ref[...]], axis=0)
        x1_meta, = _interleave(mixer(x_meta, BLOCK))
        ymeta_ref[...], = _interleave(mlp(x1_meta))
        first_scr[...] = band_bias(META_PAD)
        _interleave(mix_tile())
        first_scr[...] = band_scr[...]

    @pl.when((s > 0) & (s < last))
    def _steady():
        _interleave(mlp_tile(), mix_tile(), shares=(MLP_YIELDS, L0_MIX_YIELDS))

    @pl.when(s == last)
    def _drain():
        _interleave(mlp_tile())

    @pl.when(s == last - 1)
    def _state():
        conv_ref[...] = u_scr[CONV_HDR - 2:CONV_HDR, :]
        kout_ref[...] = k_scr[0:BLOCK, :]
        vout_ref[...] = v_scr[0:BLOCK, :]


def _l1_mixer(x, rows, is_meta, gmix, wpool_ref, pscale, h_scr):
    h = _rms(x, gmix)
    h_scr[POOL_MAX:POOL_MAX + rows, :] = h
    ys = []
    for gi, w in enumerate(POOL_WINDOWS):
        cols = slice(gi * POOL_GROUP_DIM, (gi + 1) * POOL_GROUP_DIM)
        hg = h[:, cols]
        acc = hg
        for j in range(1, w):
            acc = acc + h_scr[POOL_MAX - j:POOL_MAX - j + rows, cols]
        if is_meta:
            r = lax.broadcasted_iota(jnp.int32, (rows, 1), 0)
            pooled = acc / jnp.clip(r - (META_PAD - 1), 1, w).astype(F32)
        else:
            pooled = acc * (1.0 / w)
        ys.append(_dot((pooled - hg).astype(BF16), wpool_ref[gi]))
        yield
    h_scr[0:POOL_MAX, :] = h_scr[rows:rows + POOL_MAX, :]
    return x + jnp.concatenate(ys, axis=-1) * pscale


def _l1_prompt_kernel(x_ref, meta_ref, gmix_ref, wpool_ref, pscale_ref, gmlp_ref, wup_ref, wdn_ref,
                      y_ref, pool_ref, h_scr, x1_scr):
    s = pl.program_id(1)
    last = pl.num_programs(1) - 1
    mixer = functools.partial(_l1_mixer, gmix=gmix_ref[...], wpool_ref=wpool_ref, pscale=pscale_ref[...],
                              h_scr=h_scr)
    mlp = functools.partial(_mlp_steps, g=gmlp_ref[...], wup_ref=wup_ref, wdn_ref=wdn_ref)
    mlp_tile, mix_tile = _pipeline_steps(s, x_ref, y_ref, x1_scr,
                                         functools.partial(mixer, is_meta=False), mlp)

    @pl.when(s == 0)
    def _start():
        h_scr[0:POOL_MAX, :] = jnp.zeros((POOL_MAX, D_MODEL), F32)
        _interleave(mixer(meta_ref[...], BLOCK, True))
        _interleave(mix_tile())

    @pl.when((s > 0) & (s < last))
    def _steady():
        _interleave(mlp_tile(), mix_tile(), shares=(MLP_YIELDS, len(POOL_WINDOWS)))

    @pl.when(s == last)
    def _drain():
        _interleave(mlp_tile())

    @pl.when(s == last - 1)
    def _state():
        pool_ref[...] = h_scr[1:POOL_MAX, :]


def _time_major(ref):
    return jnp.concatenate([ref[:, t, :] for t in range(ref.shape[1])], axis=0)


def _s0_pre_kernel(x_ref, st_ref, gmix_ref, win_ref, segq_ref, segk_ref, qg_ref, kg_ref, cw_ref,
                   ya_ref, q_ref, k_ref, v_ref, conv_ref, win_bf_ref):
    n_seq, n_t, _ = x_ref.shape
    win_bf_ref[...] = win_ref[...].astype(BF16)
    u, gb, qn, kn, v = _in_proj(_time_major(x_ref), gmix_ref[...], win_bf_ref, segq_ref, segk_ref, qg_ref[...],
                                kg_ref[...])
    rows = lambda a, t: a[t * n_seq:(t + 1) * n_seq, :]
    ue = [st_ref[:, i, :] for i in range(CONV_W - 1)] + [rows(u, t) for t in range(n_t)]
    for t in range(n_t):
        y = ue[t] * cw_ref[0:1, :] + ue[t + 1] * cw_ref[1:2, :] + ue[t + 2] * cw_ref[2:3, :]
        ya_ref[t * n_seq:(t + 1) * n_seq, :] = rows(gb, t) * y
        k_ref[:, t, :] = rows(kn, t)
        v_ref[:, t, :] = rows(v, t)
        for h in range(N_HEADS):
            q_ref[:, (h // GROUP) * n_t * GROUP + t * GROUP + h % GROUP, :] = (
                rows(qn, t)[:, h * HEAD_DIM:(h + 1) * HEAD_DIM])
    for i in range(CONV_W - 1):
        conv_ref[:, i, :] = ue[n_t + i]


def _s0_attn_kernel(q_ref, ckt_ref, cvt_ref, kn_ref, vn_ref, sk_ref, o_ref, kst_ref, vst_ref, kn_scr, vn_scr):
    bb, n_t = kn_ref.shape[0], kn_ref.shape[1]
    wb = ckt_ref.shape[3]
    rows = n_t * GROUP
    ext = []
    for old_ref, new_ref, scr, out_ref in ((ckt_ref, kn_ref, kn_scr, kst_ref), (cvt_ref, vn_ref, vn_scr, vst_ref)):
        scr[...] = jnp.zeros(scr.shape, F32)
        scr[:, 0:n_t, :] = new_ref[...]
        new_t = jnp.swapaxes(scr[...], 1, 2).reshape(bb, N_KV, HEAD_DIM, SAMPLE_NEW)
        ext.append(jnp.concatenate([old_ref[...], new_t], axis=-1))
        out_ref[...] = ext[-1][:, :, :, n_t:n_t + wb]
    t = lax.broadcasted_iota(jnp.int32, (rows, wb + SAMPLE_NEW), 0) // GROUP
    c = lax.broadcasted_iota(jnp.int32, (rows, wb + SAMPLE_NEW), 1)
    bias = jnp.where((c <= wb + t) & (c >= wb + t - WINDOW), 0.0, NEG)[None]
    for j in range(N_KV):
        q = q_ref[:, j * rows:(j + 1) * rows, :].astype(BF16)
        sk = sk_ref[j][None]
        s = jnp.einsum('bqd,bdk->bqk', q, ext[0][:, j].astype(BF16), preferred_element_type=F32) + bias
        m = jnp.maximum(jnp.max(s, axis=-1, keepdims=True), sk)
        p = jnp.exp(s - m)
        den = jnp.sum(p, axis=-1, keepdims=True) + jnp.exp(sk - m)
        o = jnp.einsum('bqk,bdk->bqd', p.astype(BF16), ext[1][:, j].astype(BF16), preferred_element_type=F32)
        o_ref[:, j * rows:(j + 1) * rows, :] = o / den


def _mlp_stream_step(c, wup_ref, wdn_ref, y_ref, wup_bf_ref, wdn_bf_ref, x1_scr, xn_scr):
    wup_bf_ref[...] = wup_ref[...].astype(BF16)
    wdn_bf_ref[...] = wdn_ref[...].astype(BF16)
    a = jnp.square(jnp.maximum(_dot(xn_scr[...], wup_bf_ref[...]), 0.0)).astype(BF16)
    x1_scr[...] += _dot(a, wdn_bf_ref[...])
    if y_ref is not None:
        @pl.when(c == pl.num_programs(0) - 1)
        def _done():
            y_ref[...] = x1_scr[...]


def _s0_post_kernel(x_ref, ya_ref, o_ref, wout_ref, gmlp_ref, wup_ref, wdn_ref,
                    y_ref, wout_bf_ref, wup_bf_ref, wdn_bf_ref, x1_scr, xn_scr):
    c = pl.program_id(0)

    @pl.when(c == 0)
    def _mix():
        n_t = x_ref.shape[1]
        wout_bf_ref[...] = wout_ref[...].astype(BF16)
        yb = jnp.concatenate(
            [jnp.concatenate([o_ref[:, (h // GROUP) * n_t * GROUP + t * GROUP + h % GROUP, :]
                              for h in range(N_HEADS)], axis=-1) for t in range(n_t)], axis=0)
        mix = jnp.concatenate([ya_ref[...], yb], axis=-1).astype(BF16)
        x1 = _time_major(x_ref) + _dot(mix, wout_bf_ref[...])
        x1_scr[...] = x1
        xn_scr[...] = _rms(x1, gmlp_ref[...]).astype(BF16)

    _mlp_stream_step(c, wup_ref, wdn_ref, y_ref, wup_bf_ref, wdn_bf_ref, x1_scr, xn_scr)


def _s1_kernel(x_ref, buf_ref, gmix_ref, wpool_ref, pscale_ref, gmlp_ref, wup_ref, wdn_ref,
               y_ref, pool_ref, wpool_bf_ref, wup_bf_ref, wdn_bf_ref, x1_scr, xn_scr):
    c = pl.program_id(0)
    n_buf, n_seq, _ = buf_ref.shape
    n_t = x_ref.shape[0] // n_seq

    @pl.when(c == 0)
    def _mix():
        wpool_bf_ref[...] = wpool_ref[...].astype(BF16)
        x = x_ref[...]
        h = _rms(x, gmix_ref[...])
        he = [buf_ref[i] for i in range(n_buf)] + [h[t * n_seq:(t + 1) * n_seq, :] for t in range(n_t)]
        for i in range(n_buf):
            pool_ref[i] = he[n_t + i]
        ys = []
        for gi, w in enumerate(POOL_WINDOWS):
            cols = slice(gi * POOL_GROUP_DIM, (gi + 1) * POOL_GROUP_DIM)
            dl = []
            for t in range(n_t):
                acc = he[n_buf + t][:, cols]
                for j in range(1, w):
                    acc = acc + he[n_buf + t - j][:, cols]
                dl.append(acc * (1.0 / w) - he[n_buf + t][:, cols])
            ys.append(_dot(jnp.concatenate(dl, axis=0).astype(BF16), wpool_bf_ref[gi]))
        x1 = x + jnp.concatenate(ys, axis=-1) * pscale_ref[...]
        x1_scr[...] = x1
        xn_scr[...] = _rms(x1, gmlp_ref[...]).astype(BF16)

    _mlp_stream_step(c, wup_ref, wdn_ref, None, wup_bf_ref, wdn_bf_ref, x1_scr, xn_scr)

    @pl.when(c == pl.num_programs(0) - 1)
    def _done():
        for t in range(n_t):
            y_ref[:, t, :] = x1_scr[t * n_seq:(t + 1) * n_seq, :]


class _Stacked(NamedTuple):
    array: jax.Array
    layer: int


def _operand(arg):
    return arg.array if isinstance(arg, _Stacked) else arg


def _resident(arg, n_grid):
    if isinstance(arg, _Stacked):
        shape, idx = (None,) + arg.array.shape[1:], (arg.layer,) + (0,) * (arg.array.ndim - 1)
    else:
        shape, idx = arg.shape, (0,) * arg.ndim
    index_map = {1: lambda i: idx, 2: lambda i, j: idx}[n_grid]
    return pl.BlockSpec(shape, index_map, pipeline_mode=pl.Buffered(1))


def _params(n_grid):
    return pltpu.CompilerParams(dimension_semantics=("arbitrary",) * n_grid, vmem_limit_bytes=VMEM_LIMIT)


def _prompt_specs(n_tiles):
    tile_in = pl.BlockSpec((None, SEQ_TILE, D_MODEL), lambda b, s: (b, jnp.minimum(s, n_tiles - 1), 0))
    tile_out = pl.BlockSpec((None, SEQ_TILE, D_MODEL), lambda b, s: (b, jnp.maximum(s - 1, 0), 0))
    per_b = lambda r, c: pl.BlockSpec((None, r, c), lambda b, s: (b, 0, 0))
    return tile_in, tile_out, per_b


def _l0_prompt(x, sinks, *resident):
    n_b, seq, _ = x.shape
    n_tiles = seq // SEQ_TILE
    tile_in, tile_out, per_b = _prompt_specs(n_tiles)
    return pl.pallas_call(
        _l0_prompt_kernel,
        grid=(n_b, n_tiles + 1),
        in_specs=[pl.BlockSpec(memory_space=pltpu.SMEM), tile_in] + [_resident(a, 2) for a in resident],
        out_specs=[tile_out, per_b(BLOCK, D_MODEL), per_b(CONV_W - 1, D_CONV), per_b(WINDOW, KV_DIM),
                   per_b(WINDOW, KV_DIM)],
        out_shape=[jax.ShapeDtypeStruct((n_b, seq, D_MODEL), F32),
                   jax.ShapeDtypeStruct((n_b, BLOCK, D_MODEL), F32),
                   jax.ShapeDtypeStruct((n_b, CONV_W - 1, D_CONV), F32),
                   jax.ShapeDtypeStruct((n_b, WINDOW, KV_DIM), F32),
                   jax.ShapeDtypeStruct((n_b, WINDOW, KV_DIM), F32)],
        scratch_shapes=[pltpu.VMEM((CONV_HDR + SEQ_TILE, D_CONV), F32),
                        pltpu.VMEM((BLOCK + SEQ_TILE, KV_DIM), F32),
                        pltpu.VMEM((BLOCK + SEQ_TILE, KV_DIM), F32),
                        pltpu.VMEM((2 * BLOCK, GROUP * BLOCK), F32),
                        pltpu.VMEM((2 * BLOCK, GROUP * BLOCK), F32),
                        pltpu.VMEM((2, SEQ_TILE, D_MODEL), F32)],
        compiler_params=_params(2),
        name="l0_prompt",
    )(sinks, x, *[_operand(a) for a in resident])


def _l1_prompt(x, xmeta, *resident):
    n_b, seq, _ = x.shape
    n_tiles = seq // SEQ_TILE
    tile_in, tile_out, per_b = _prompt_specs(n_tiles)
    return pl.pallas_call(
        _l1_prompt_kernel,
        grid=(n_b, n_tiles + 1),
        in_specs=[tile_in, per_b(BLOCK, D_MODEL)] + [_resident(a, 2) for a in resident],
        out_specs=[tile_out, per_b(POOL_MAX - 1, D_MODEL)],
        out_shape=[jax.ShapeDtypeStruct((n_b, seq, D_MODEL), F32),
                   jax.ShapeDtypeStruct((n_b, POOL_MAX - 1, D_MODEL), F32)],
        scratch_shapes=[pltpu.VMEM((POOL_MAX + SEQ_TILE, D_MODEL), F32),
                        pltpu.VMEM((2, SEQ_TILE, D_MODEL), F32)],
        compiler_params=_params(2),
        name="l1_prompt",
    )(x, xmeta, *[_operand(a) for a in resident])


def _single_step(kernel, name, out_shape, *args):
    return pl.pallas_call(
        kernel,
        grid=(1,),
        in_specs=[_resident(a, 1) for a in args],
        out_specs=[pl.BlockSpec(o.shape, lambda i, n=len(o.shape): (0,) * n) for o in out_shape],
        out_shape=out_shape,
        compiler_params=_params(1),
        name=name,
    )(*[_operand(a) for a in args])


def _streamed_mlp(kernel, name, n_rows, outs, consts, w_up, w_down, layer):
    const = lambda shape: pl.BlockSpec(shape, lambda c, n=len(shape): (0,) * n)
    return pl.pallas_call(
        kernel,
        grid=(D_FF // FF_STREAM,),
        in_specs=[_resident(a, 1) for a in consts]
        + [pl.BlockSpec((None, D_MODEL, FF_STREAM), lambda c: (layer, 0, c)),
           pl.BlockSpec((None, FF_STREAM, D_MODEL), lambda c: (layer, c, 0))],
        out_specs=[const(o.shape) for o in outs]
        + [pl.BlockSpec((D_MODEL, FF_STREAM), lambda c: (0, c)),
           pl.BlockSpec((FF_STREAM, D_MODEL), lambda c: (c, 0))],
        out_shape=list(outs) + [jax.ShapeDtypeStruct((D_MODEL, D_FF), BF16),
                                jax.ShapeDtypeStruct((D_FF, D_MODEL), BF16)],
        scratch_shapes=[pltpu.VMEM((n_rows, D_MODEL), F32), pltpu.VMEM((n_rows, D_MODEL), BF16)],
        compiler_params=_params(1),
        name=name,
    )(*[_operand(a) for a in consts], w_up, w_down)


def _s0_attn(q4, ckt, cvt, kn, vn, sk_rows):
    n_seq, _, _, wb = ckt.shape
    n_t = kn.shape[1]
    rows = n_t * GROUP
    blk = lambda *tail: pl.BlockSpec((SAMPLE_BB,) + tail, lambda i: (i,) + (0,) * len(tail))
    return pl.pallas_call(
        _s0_attn_kernel,
        grid=(n_seq // SAMPLE_BB,),
        in_specs=[blk(N_KV * rows, HEAD_DIM), blk(N_KV, HEAD_DIM, wb), blk(N_KV, HEAD_DIM, wb),
                  blk(n_t, KV_DIM), blk(n_t, KV_DIM), pl.BlockSpec((N_KV, rows, 1), lambda i: (0, 0, 0))],
        out_specs=[blk(N_KV * rows, HEAD_DIM), blk(N_KV, HEAD_DIM, wb), blk(N_KV, HEAD_DIM, wb)],
        out_shape=[jax.ShapeDtypeStruct((n_seq, N_KV * rows, HEAD_DIM), F32),
                   jax.ShapeDtypeStruct(ckt.shape, F32), jax.ShapeDtypeStruct(cvt.shape, F32)],
        scratch_shapes=[pltpu.VMEM((SAMPLE_BB, SAMPLE_NEW, KV_DIM), F32),
                        pltpu.VMEM((SAMPLE_BB, SAMPLE_NEW, KV_DIM), F32)],
        compiler_params=_params(1),
        name="s0_attn",
    )(q4, ckt, cvt, kn, vn, sk_rows)


def kernel(x_prompt, x_sample, state_conv, cache_k_win, cache_v_win, state_pool, meta_tokens, norm_mix, norm_mlp, w_in_even, conv_w, q_norm, k_norm, attn_sinks, w_out_even, w_pool, pool_scale, w_up, w_down):
    n_seq, n_t, _ = x_sample.shape
    wb = cache_k_win.shape[2]
    assert x_prompt.shape[1] % SEQ_TILE == 0 and x_prompt.shape[1] >= 2 * SEQ_TILE and n_seq % SAMPLE_BB == 0
    assert wb == WINDOW and n_t <= SAMPLE_NEW

    gmix = [_Stacked(norm_mix[:, None, :], l) for l in range(2)]
    gmlp = [_Stacked(norm_mlp[:, None, :], l) for l in range(2)]
    qg = jnp.tile(q_norm[0], N_HEADS)[None, :]
    kg = jnp.tile(k_norm[0], N_KV)[None, :]
    seg = np.kron(np.eye(N_HEADS, dtype=np.float32), np.full((HEAD_DIM, HEAD_DIM), 1.0 / HEAD_DIM, np.float32))
    segq, segk = jnp.asarray(seg, BF16), jnp.asarray(seg[:KV_DIM, :KV_DIM], BF16)
    cw = _Stacked(conv_w, 0)
    sinks = attn_sinks[0]
    sds = lambda *shape, dtype=F32: jax.ShapeDtypeStruct(shape, dtype)

    n_rows = n_t * n_seq
    ya, q4, kn_b, vn_b, conv_s, win = _single_step(
        _s0_pre_kernel, "s0_pre",
        [sds(n_rows, D_CONV), sds(n_seq, N_HEADS * n_t, HEAD_DIM), sds(n_seq, n_t, KV_DIM),
         sds(n_seq, n_t, KV_DIM), sds(n_seq, CONV_W - 1, D_CONV), sds(D_MODEL, D_IN_EVEN, dtype=BF16)],
        x_sample, _Stacked(state_conv, 0), gmix[0], _Stacked(w_in_even, 0), segq, segk, qg, kg, cw)
    sk_rows = jnp.tile(sinks.reshape(N_KV, 1, GROUP), (1, n_t, 1)).reshape(N_KV, n_t * GROUP, 1)
    to_dev = lambda a: jnp.transpose(a.reshape(n_seq, -1, N_KV, HEAD_DIM), (0, 2, 3, 1))
    o4, kst, vst = _s0_attn(q4, to_dev(cache_k_win), to_dev(cache_v_win), kn_b, vn_b, sk_rows)
    k_s, v_s = (jnp.transpose(a, (0, 3, 1, 2))[None] for a in (kst, vst))
    xs2, wout, wup0, wdn0 = _streamed_mlp(
        _s0_post_kernel, "s0_post", n_rows, [sds(n_rows, D_MODEL), sds(D_MODEL, D_MODEL, dtype=BF16)],
        [x_sample, ya, o4, _Stacked(w_out_even, 0), gmlp[0]], w_up, w_down, 0)
    buf_t = jnp.transpose(state_pool[0], (1, 0, 2))
    y_sample, pool_s, wpool, wup1, wdn1 = _streamed_mlp(
        _s1_kernel, "s1", n_rows,
        [sds(n_seq, n_t, D_MODEL), sds(POOL_MAX - 1, n_seq, D_MODEL), sds(*w_pool.shape[1:], dtype=BF16)],
        [xs2, buf_t, gmix[1], _Stacked(w_pool, 0), pool_scale, gmlp[1]], w_up, w_down, 1)

    x2, x2_meta, conv_p, k_p, v_p = _l0_prompt(x_prompt, sinks, meta_tokens, gmix[0], win, segq, segk, qg, kg, cw,
                                               wout, gmlp[0], wup0, wdn0)
    y_prompt, pool_p = _l1_prompt(x2, x2_meta, gmix[1], wpool, pool_scale, gmlp[1], wup1, wdn1)

    kv5 = lambda a: a.reshape(1, a.shape[0], WINDOW, N_KV, HEAD_DIM)
    return (y_prompt, y_sample, conv_p[None], conv_s[None], kv5(k_p), k_s, kv5(v_p), v_s,
            pool_p[None], jnp.transpose(pool_s, (1, 0, 2))[None])
```

```python
import functools
from typing import NamedTuple

import jax
import jax.numpy as jnp
import numpy as np
from jax import lax
from jax.experimental import pallas as pl
from jax.experimental.pallas import tpu as pltpu

F32 = jnp.float32
BF16 = jnp.bfloat16

D_MODEL = 1024
D_CONV = 512
CONV_W = 3
HEAD_DIM = 64
N_HEADS = 8
N_KV = 2
GROUP = N_HEADS // N_KV
WINDOW = 128
BLOCK = 128
Q_DIM = N_HEADS * HEAD_DIM
KV_DIM = N_KV * HEAD_DIM
POOL_WINDOWS = (2, 4, 8, 16)
POOL_GROUP_DIM = D_MODEL // len(POOL_WINDOWS)
POOL_MAX = 16
D_FF = 4 * D_MODEL
D_IN_EVEN = 3 * D_CONV + Q_DIM + 2 * KV_DIM
N_META = 16
EPS = 1e-6
NEG = -1e30

META_PAD = BLOCK - N_META
SEQ_TILE = 512
FF_CHUNK = 1024
FF_STREAM = 512
CONV_HDR = 8
SAMPLE_BB = 16
SAMPLE_NEW = 8
VMEM_LIMIT = 56 * 1024 * 1024
MLP_YIELDS = 2 * (D_FF // FF_CHUNK)
L0_MIX_YIELDS = 1 + 2 * N_KV * (SEQ_TILE // BLOCK)


def _dot(a, b):
    return jnp.dot(a, b, preferred_element_type=F32)


def _dot_nt(a, b):
    return lax.dot_general(a, b, (((1,), (1,)), ((), ())), preferred_element_type=F32)


def _dot_tn(a, b):
    return lax.dot_general(a, b, (((0,), (0,)), ((), ())), preferred_element_type=F32)


def _rms(x, g):
    ms = jnp.mean(x * x, axis=-1, keepdims=True)
    return x * lax.rsqrt(ms + EPS) * g


def _head_rms(x, seg, g):
    ms = _dot((x * x).astype(BF16), seg)
    return x * lax.rsqrt(ms + EPS) * g


def _interleave(*steppers, shares=None):
    shares = shares or [1] * len(steppers)
    results = [None] * len(steppers)
    done = [0] * len(steppers)
    live = set(range(len(steppers)))
    while live:
        i = min(live, key=lambda k: ((done[k] + 1) / shares[k], k))
        try:
            next(steppers[i])
            done[i] += 1
        except StopIteration as finished:
            results[i] = finished.value
            live.discard(i)
    return results


def _mlp_steps(x, g, wup_ref, wdn_ref):
    xn = _rms(x, g).astype(BF16)
    acc = x
    for c in range(D_FF // FF_CHUNK):
        cols = slice(c * FF_CHUNK, (c + 1) * FF_CHUNK)
        h = _dot(xn, wup_ref[:, cols])
        a = jnp.square(jnp.maximum(h, 0.0)).astype(BF16)
        yield
        acc = acc + _dot(a, wdn_ref[cols, :])
        yield
    return acc


def _mlp(x, g, wup_ref, wdn_ref):
    return _interleave(_mlp_steps(x, g, wup_ref, wdn_ref))[0]


def _in_proj(x, g, win_ref, segq_ref, segk_ref, qg, kg):
    hn = _rms(x, g).astype(BF16)
    kv_col = 3 * D_CONV + Q_DIM
    half = x.shape[0] // 2
    z = _dot(hn, win_ref[:, 0:kv_col])
    kv = jnp.concatenate([_dot(hn[0:half], win_ref[:, kv_col:]), _dot(hn[half:], win_ref[:, kv_col:])], axis=0)
    xa = z[:, 0:D_CONV]
    gc = z[:, D_CONV:2 * D_CONV]
    gb = z[:, 2 * D_CONV:3 * D_CONV]
    q = z[:, 3 * D_CONV:]
    k = kv[:, 0:KV_DIM]
    v = kv[:, KV_DIM:]
    qn = _head_rms(q, segq_ref[...], qg) * (HEAD_DIM ** -0.5)
    kn = _head_rms(k, segk_ref[...], kg)
    return gc * xa, gb, qn, kn, v


def _l0_mixer(x, rows, sinks_ref, gmix, win_ref, segq_ref, segk_ref, qg, kg, cw_ref, wout_ref,
              u_scr, k_scr, v_scr, first_scr, band_scr):
    u, gb, qn, kn, v = _in_proj(x, gmix, win_ref, segq_ref, segk_ref, qg, kg)

    u_scr[CONV_HDR:CONV_HDR + rows, :] = u
    u1 = u_scr[CONV_HDR - 1:CONV_HDR - 1 + rows, :]
    u2 = u_scr[CONV_HDR - 2:CONV_HDR - 2 + rows, :]
    ya = gb * (u2 * cw_ref[0:1, :] + u1 * cw_ref[1:2, :] + u * cw_ref[2:3, :])
    k_scr[BLOCK:BLOCK + rows, :] = kn
    v_scr[BLOCK:BLOCK + rows, :] = v
    yield

    yb_blocks = []
    for i in range(rows // BLOCK):
        qb = qn[i * BLOCK:(i + 1) * BLOCK, :]
        bias = first_scr[...] if i == 0 else band_scr[...]
        heads_t = []
        for j in range(N_KV):
            lanes = slice(j * HEAD_DIM, (j + 1) * HEAD_DIM)
            k2 = k_scr[i * BLOCK:(i + 2) * BLOCK, lanes].astype(BF16)
            v2 = v_scr[i * BLOCK:(i + 2) * BLOCK, lanes].astype(BF16)
            qs = jnp.concatenate(
                [qb[:, (j * GROUP + g) * HEAD_DIM:(j * GROUP + g + 1) * HEAD_DIM] for g in range(GROUP)],
                axis=0).astype(BF16)
            sk = jnp.concatenate(
                [jnp.full((1, BLOCK), sinks_ref[j * GROUP + g], F32) for g in range(GROUP)], axis=1)
            st = _dot_nt(k2, qs) + bias
            m = jnp.maximum(jnp.max(st, axis=0, keepdims=True), sk)
            p = jnp.exp(st - m)
            den = jnp.sum(p, axis=0, keepdims=True) + jnp.exp(sk - m)
            yield
            ot = _dot_tn(v2, p.astype(BF16)) / den
            heads_t += [ot[:, g * BLOCK:(g + 1) * BLOCK] for g in range(GROUP)]
            yield
        yb_blocks.append(jnp.concatenate(heads_t, axis=0).T)
    yb = yb_blocks[0] if len(yb_blocks) == 1 else jnp.concatenate(yb_blocks, axis=0)

    u_scr[CONV_HDR - 2:CONV_HDR, :] = u_scr[CONV_HDR - 2 + rows:CONV_HDR + rows, :]
    k_scr[0:BLOCK, :] = k_scr[rows:rows + BLOCK, :]
    v_scr[0:BLOCK, :] = v_scr[rows:rows + BLOCK, :]

    mix = jnp.concatenate([ya, yb], axis=-1).astype(BF16)
    return x + _dot(mix, wout_ref[...])


def _pipeline_steps(s, x_ref, y_ref, x1_scr, mixer, mlp):
    slot = lax.rem(s, 2)

    def mix_tile():
        x1_scr[slot] = yield from mixer(x_ref[...], SEQ_TILE)

    def mlp_tile():
        y_ref[...] = yield from mlp(x1_scr[1 - slot])

    return mlp_tile, mix_tile


def _l0_prompt_kernel(sinks_ref, x_ref, meta_ref, gmix_ref, win_ref, segq_ref, segk_ref, qg_ref, kg_ref,
                      cw_ref, wout_ref, gmlp_ref, wup_ref, wdn_ref,
                      y_ref, ymeta_ref, conv_ref, kout_ref, vout_ref,
                      u_scr, k_scr, v_scr, first_scr, band_scr, x1_scr):
    s = pl.program_id(1)
    last = pl.num_programs(1) - 1
    mixer = functools.partial(
        _l0_mixer, sinks_ref=sinks_ref, gmix=gmix_ref[0:1, :], win_ref=win_ref, segq_ref=segq_ref,
        segk_ref=segk_ref, qg=qg_ref[...], kg=kg_ref[...], cw_ref=cw_ref, wout_ref=wout_ref,
        u_scr=u_scr, k_scr=k_scr, v_scr=v_scr, first_scr=first_scr, band_scr=band_scr)
    mlp = functools.partial(_mlp_steps, g=gmlp_ref[0:1, :], wup_ref=wup_ref, wdn_ref=wdn_ref)
    mlp_tile, mix_tile = _pipeline_steps(s, x_ref, y_ref, x1_scr, mixer, mlp)

    def band_bias(first_valid_key):
        c = lax.broadcasted_iota(jnp.int32, (2 * BLOCK, GROUP * BLOCK), 0)
        q = lax.broadcasted_iota(jnp.int32, (2 * BLOCK, GROUP * BLOCK), 1) & (BLOCK - 1)
        return jnp.where((c >= q) & (c <= q + WINDOW) & (c >= first_valid_key), 0.0, NEG)

    @pl.when(s == 0)
    def _start():
        band_scr[...] = band_bias(0)
        u_scr[0:CONV_HDR, :] = jnp.zeros((CONV_HDR, D_CONV), F32)
        k_scr[0:BLOCK, :] = jnp.zeros((BLOCK, KV_DIM), F32)
        v_scr[0:BLOCK, :] = jnp.zeros((BLOCK, KV_DIM), F32)
        first_scr[...] = band_bias(BLOCK + META_PAD)
        x_meta = jnp.concatenate([jnp.zeros((META_PAD, D_MODEL), F32), meta_ref[...]], axis=0)
        x1_meta, = _interleave(mixer(x_meta, BLOCK))
        first_scr[...] = band_bias(META_PAD)
        ymeta_ref[...], _ = _interleave(mlp(x1_meta), mix_tile(), shares=(MLP_YIELDS, L0_MIX_YIELDS))
        first_scr[...] = band_scr[...]

    @pl.when((s > 0) & (s < last))
    def _steady():
        _interleave(mlp_tile(), mix_tile(), shares=(MLP_YIELDS, L0_MIX_YIELDS))

    @pl.when(s == last)
    def _drain():
        _interleave(mlp_tile())

    @pl.when(s == last - 1)
    def _state():
        conv_ref[...] = u_scr[CONV_HDR - 2:CONV_HDR, :]
        kout_ref[...] = k_scr[0:BLOCK, :]
        vout_ref[...] = v_scr[0:BLOCK, :]


def _l1_mixer(x, rows, is_meta, gmix, wpool_ref, pscale, h_scr):
    h = _rms(x, gmix)
    h_scr[POOL_MAX:POOL_MAX + rows, :] = h
    ys = []
    for gi, w in enumerate(POOL_WINDOWS):
        cols = slice(gi * POOL_GROUP_DIM, (gi + 1) * POOL_GROUP_DIM)
        hg = h[:, cols]
        acc = hg
        for j in range(1, w):
            acc = acc + h_scr[POOL_MAX - j:POOL_MAX - j + rows, cols]
        if is_meta:
            r = lax.broadcasted_iota(jnp.int32, (rows, 1), 0)
            pooled = acc / jnp.clip(r - (META_PAD - 1), 1, w).astype(F32)
        else:
            pooled = acc * (1.0 / w)
        ys.append(_dot((pooled - hg).astype(BF16), wpool_ref[gi]))
        yield
    h_scr[0:POOL_MAX, :] = h_scr[rows:rows + POOL_MAX, :]
    return x + jnp.concatenate(ys, axis=-1) * pscale


def _l1_prompt_kernel(x_ref, meta_ref, gmix_ref, wpool_ref, pscale_ref, gmlp_ref, wup_ref, wdn_ref,
                      y_ref, pool_ref, h_scr, x1_scr):
    s = pl.program_id(1)
    last = pl.num_programs(1) - 1
    mixer = functools.partial(_l1_mixer, gmix=gmix_ref[1:2, :], wpool_ref=wpool_ref, pscale=pscale_ref[...],
                              h_scr=h_scr)
    mlp = functools.partial(_mlp_steps, g=gmlp_ref[1:2, :], wup_ref=wup_ref, wdn_ref=wdn_ref)
    mlp_tile, mix_tile = _pipeline_steps(s, x_ref, y_ref, x1_scr,
                                         functools.partial(mixer, is_meta=False), mlp)

    @pl.when(s == 0)
    def _start():
        h_scr[0:POOL_MAX, :] = jnp.zeros((POOL_MAX, D_MODEL), F32)
        _interleave(mixer(meta_ref[...], BLOCK, True))
        _interleave(mix_tile())

    @pl.when((s > 0) & (s < last))
    def _steady():
        _interleave(mlp_tile(), mix_tile(), shares=(MLP_YIELDS, len(POOL_WINDOWS)))

    @pl.when(s == last)
    def _drain():
        _interleave(mlp_tile())

    @pl.when(s == last - 1)
    def _state():
        pool_ref[...] = h_scr[1:POOL_MAX, :]


def _time_major(ref):
    return jnp.concatenate([ref[:, t, :] for t in range(ref.shape[1])], axis=0)


def _s0_pre_kernel(x_ref, st_ref, gmix_ref, win_ref, segq_ref, segk_ref, qg_ref, kg_ref, cw_ref,
                   ya_ref, q_ref, k_ref, v_ref, conv_ref, win_bf_ref):
    n_seq, n_t, _ = x_ref.shape
    win_bf_ref[...] = win_ref[...].astype(BF16)
    u, gb, qn, kn, v = _in_proj(_time_major(x_ref), gmix_ref[0:1, :], win_bf_ref, segq_ref, segk_ref, qg_ref[...],
                                kg_ref[...])
    rows = lambda a, t: a[t * n_seq:(t + 1) * n_seq, :]
    ue = [st_ref[:, i, :] for i in range(CONV_W - 1)] + [rows(u, t) for t in range(n_t)]
    for t in range(n_t):
        y = ue[t] * cw_ref[0:1, :] + ue[t + 1] * cw_ref[1:2, :] + ue[t + 2] * cw_ref[2:3, :]
        ya_ref[t * n_seq:(t + 1) * n_seq, :] = rows(gb, t) * y
        k_ref[:, t, :] = rows(kn, t)
        v_ref[:, t, :] = rows(v, t)
        for h in range(N_HEADS):
            q_ref[:, (h // GROUP) * n_t * GROUP + t * GROUP + h % GROUP, :] = (
                rows(qn, t)[:, h * HEAD_DIM:(h + 1) * HEAD_DIM])
    for i in range(CONV_W - 1):
        conv_ref[:, i, :] = ue[n_t + i]


def _s0_attn_kernel(q_ref, ckt_ref, cvt_ref, kn_ref, vn_ref, sk_ref, o_ref, kst_ref, vst_ref, kn_scr, vn_scr):
    bb, n_t = kn_ref.shape[0], kn_ref.shape[1]
    wb = ckt_ref.shape[3]
    rows = n_t * GROUP
    ext = []
    for old_ref, new_ref, scr, out_ref in ((ckt_ref, kn_ref, kn_scr, kst_ref), (cvt_ref, vn_ref, vn_scr, vst_ref)):
        scr[...] = jnp.zeros(scr.shape, F32)
        scr[:, 0:n_t, :] = new_ref[...]
        new_t = jnp.swapaxes(scr[...], 1, 2).reshape(bb, N_KV, HEAD_DIM, SAMPLE_NEW)
        ext.append(jnp.concatenate([old_ref[...], new_t], axis=-1))
        out_ref[...] = ext[-1][:, :, :, n_t:n_t + wb]
    t = lax.broadcasted_iota(jnp.int32, (rows, wb + SAMPLE_NEW), 0) // GROUP
    c = lax.broadcasted_iota(jnp.int32, (rows, wb + SAMPLE_NEW), 1)
    bias = jnp.where((c <= wb + t) & (c >= wb + t - WINDOW), 0.0, NEG)[None]
    for j in range(N_KV):
        q = q_ref[:, j * rows:(j + 1) * rows, :].astype(BF16)
        sk = sk_ref[j][None]
        s = jnp.einsum('bqd,bdk->bqk', q, ext[0][:, j].astype(BF16), preferred_element_type=F32) + bias
        m = jnp.maximum(jnp.max(s, axis=-1, keepdims=True), sk)
        p = jnp.exp(s - m)
        den = jnp.sum(p, axis=-1, keepdims=True) + jnp.exp(sk - m)
        o = jnp.einsum('bqk,bdk->bqd', p.astype(BF16), ext[1][:, j].astype(BF16), preferred_element_type=F32)
        o_ref[:, j * rows:(j + 1) * rows, :] = o / den


def _mlp_stream_step(c, wup_ref, wdn_ref, y_ref, wup_bf_ref, wdn_bf_ref, x1_scr, xn_scr):
    wup_bf_ref[...] = wup_ref[...].astype(BF16)
    wdn_bf_ref[...] = wdn_ref[...].astype(BF16)
    a = jnp.square(jnp.maximum(_dot(xn_scr[...], wup_bf_ref[...]), 0.0)).astype(BF16)
    x1_scr[...] += _dot(a, wdn_bf_ref[...])
    if y_ref is not None:
        @pl.when(c == pl.num_programs(0) - 1)
        def _done():
            y_ref[...] = x1_scr[...]


def _s0_post_kernel(x_ref, ya_ref, o_ref, wout_ref, gmlp_ref, wup_ref, wdn_ref,
                    y_ref, wout_bf_ref, wup_bf_ref, wdn_bf_ref, x1_scr, xn_scr):
    c = pl.program_id(0)

    @pl.when(c == 0)
    def _mix():
        n_t = x_ref.shape[1]
        wout_bf_ref[...] = wout_ref[...].astype(BF16)
        yb = jnp.concatenate(
            [jnp.concatenate([o_ref[:, (h // GROUP) * n_t * GROUP + t * GROUP + h % GROUP, :]
                              for h in range(N_HEADS)], axis=-1) for t in range(n_t)], axis=0)
        mix = jnp.concatenate([ya_ref[...], yb], axis=-1).astype(BF16)
        x1 = _time_major(x_ref) + _dot(mix, wout_bf_ref[...])
        x1_scr[...] = x1
        xn_scr[...] = _rms(x1, gmlp_ref[0:1, :]).astype(BF16)

    _mlp_stream_step(c, wup_ref, wdn_ref, y_ref, wup_bf_ref, wdn_bf_ref, x1_scr, xn_scr)


def _s1_kernel(x_ref, buf_ref, gmix_ref, wpool_ref, pscale_ref, gmlp_ref, wup_ref, wdn_ref,
               y_ref, pool_ref, wpool_bf_ref, wup_bf_ref, wdn_bf_ref, x1_scr, xn_scr):
    c = pl.program_id(0)
    n_buf, n_seq, _ = buf_ref.shape
    n_t = x_ref.shape[0] // n_seq

    @pl.when(c == 0)
    def _mix():
        wpool_bf_ref[...] = wpool_ref[...].astype(BF16)
        x = x_ref[...]
        h = _rms(x, gmix_ref[1:2, :])
        he = [buf_ref[i] for i in range(n_buf)] + [h[t * n_seq:(t + 1) * n_seq, :] for t in range(n_t)]
        for i in range(n_buf):
            pool_ref[i] = he[n_t + i]
        ys = []
        for gi, w in enumerate(POOL_WINDOWS):
            cols = slice(gi * POOL_GROUP_DIM, (gi + 1) * POOL_GROUP_DIM)
            dl = []
            for t in range(n_t):
                acc = he[n_buf + t][:, cols]
                for j in range(1, w):
                    acc = acc + he[n_buf + t - j][:, cols]
                dl.append(acc * (1.0 / w) - he[n_buf + t][:, cols])
            ys.append(_dot(jnp.concatenate(dl, axis=0).astype(BF16), wpool_bf_ref[gi]))
        x1 = x + jnp.concatenate(ys, axis=-1) * pscale_ref[...]
        x1_scr[...] = x1
        xn_scr[...] = _rms(x1, gmlp_ref[1:2, :]).astype(BF16)

    _mlp_stream_step(c, wup_ref, wdn_ref, None, wup_bf_ref, wdn_bf_ref, x1_scr, xn_scr)

    @pl.when(c == pl.num_programs(0) - 1)
    def _done():
        for t in range(n_t):
            y_ref[:, t, :] = x1_scr[t * n_seq:(t + 1) * n_seq, :]


class _Stacked(NamedTuple):
    array: jax.Array
    layer: int


def _operand(arg):
    return arg.array if isinstance(arg, _Stacked) else arg


def _resident(arg, n_grid):
    if isinstance(arg, _Stacked):
        shape, idx = (None,) + arg.array.shape[1:], (arg.layer,) + (0,) * (arg.array.ndim - 1)
    else:
        shape, idx = arg.shape, (0,) * arg.ndim
    index_map = {1: lambda i: idx, 2: lambda i, j: idx}[n_grid]
    return pl.BlockSpec(shape, index_map, pipeline_mode=pl.Buffered(1))


def _params(n_grid):
    return pltpu.CompilerParams(dimension_semantics=("arbitrary",) * n_grid, vmem_limit_bytes=VMEM_LIMIT)


def _prompt_specs(n_tiles):
    tile_in = pl.BlockSpec((None, SEQ_TILE, D_MODEL), lambda b, s: (b, jnp.minimum(s, n_tiles - 1), 0))
    tile_out = pl.BlockSpec((None, SEQ_TILE, D_MODEL), lambda b, s: (b, jnp.maximum(s - 1, 0), 0))
    per_b = lambda r, c: pl.BlockSpec((None, r, c), lambda b, s: (b, 0, 0))
    return tile_in, tile_out, per_b


def _l0_prompt(x, sinks, *resident):
    n_b, seq, _ = x.shape
    n_tiles = seq // SEQ_TILE
    tile_in, tile_out, per_b = _prompt_specs(n_tiles)
    return pl.pallas_call(
        _l0_prompt_kernel,
        grid=(n_b, n_tiles + 1),
        in_specs=[pl.BlockSpec(memory_space=pltpu.SMEM), tile_in] + [_resident(a, 2) for a in resident],
        out_specs=[tile_out, per_b(BLOCK, D_MODEL), per_b(CONV_W - 1, D_CONV), per_b(WINDOW, KV_DIM),
                   per_b(WINDOW, KV_DIM)],
        out_shape=[jax.ShapeDtypeStruct((n_b, seq, D_MODEL), F32),
                   jax.ShapeDtypeStruct((n_b, BLOCK, D_MODEL), F32),
                   jax.ShapeDtypeStruct((n_b, CONV_W - 1, D_CONV), F32),
                   jax.ShapeDtypeStruct((n_b, WINDOW, KV_DIM), F32),
                   jax.ShapeDtypeStruct((n_b, WINDOW, KV_DIM), F32)],
        scratch_shapes=[pltpu.VMEM((CONV_HDR + SEQ_TILE, D_CONV), F32),
                        pltpu.VMEM((BLOCK + SEQ_TILE, KV_DIM), F32),
                        pltpu.VMEM((BLOCK + SEQ_TILE, KV_DIM), F32),
                        pltpu.VMEM((2 * BLOCK, GROUP * BLOCK), F32),
                        pltpu.VMEM((2 * BLOCK, GROUP * BLOCK), F32),
                        pltpu.VMEM((2, SEQ_TILE, D_MODEL), F32)],
        compiler_params=_params(2),
        name="l0_prompt",
    )(sinks, x, *[_operand(a) for a in resident])


def _l1_prompt(x, xmeta, *resident):
    n_b, seq, _ = x.shape
    n_tiles = seq // SEQ_TILE
    tile_in, tile_out, per_b = _prompt_specs(n_tiles)
    return pl.pallas_call(
        _l1_prompt_kernel,
        grid=(n_b, n_tiles + 1),
        in_specs=[tile_in, per_b(BLOCK, D_MODEL)] + [_resident(a, 2) for a in resident],
        out_specs=[tile_out, per_b(POOL_MAX - 1, D_MODEL)],
        out_shape=[jax.ShapeDtypeStruct((n_b, seq, D_MODEL), F32),
                   jax.ShapeDtypeStruct((n_b, POOL_MAX - 1, D_MODEL), F32)],
        scratch_shapes=[pltpu.VMEM((POOL_MAX + SEQ_TILE, D_MODEL), F32),
                        pltpu.VMEM((2, SEQ_TILE, D_MODEL), F32)],
        compiler_params=_params(2),
        name="l1_prompt",
    )(x, xmeta, *[_operand(a) for a in resident])


def _single_step(kernel, name, out_shape, *args):
    return pl.pallas_call(
        kernel,
        grid=(1,),
        in_specs=[_resident(a, 1) for a in args],
        out_specs=[pl.BlockSpec(o.shape, lambda i, n=len(o.shape): (0,) * n) for o in out_shape],
        out_shape=out_shape,
        compiler_params=_params(1),
        name=name,
    )(*[_operand(a) for a in args])


def _streamed_mlp(kernel, name, n_rows, outs, consts, w_up, w_down, layer):
    const = lambda shape: pl.BlockSpec(shape, lambda c, n=len(shape): (0,) * n)
    return pl.pallas_call(
        kernel,
        grid=(D_FF // FF_STREAM,),
        in_specs=[_resident(a, 1) for a in consts]
        + [pl.BlockSpec((None, D_MODEL, FF_STREAM), lambda c: (layer, 0, c)),
           pl.BlockSpec((None, FF_STREAM, D_MODEL), lambda c: (layer, c, 0))],
        out_specs=[const(o.shape) for o in outs]
        + [pl.BlockSpec((D_MODEL, FF_STREAM), lambda c: (0, c)),
           pl.BlockSpec((FF_STREAM, D_MODEL), lambda c: (c, 0))],
        out_shape=list(outs) + [jax.ShapeDtypeStruct((D_MODEL, D_FF), BF16),
                                jax.ShapeDtypeStruct((D_FF, D_MODEL), BF16)],
        scratch_shapes=[pltpu.VMEM((n_rows, D_MODEL), F32), pltpu.VMEM((n_rows, D_MODEL), BF16)],
        compiler_params=_params(1),
        name=name,
    )(*[_operand(a) for a in consts], w_up, w_down)


def _s0_attn(q4, ckt, cvt, kn, vn, sk_rows):
    n_seq, _, _, wb = ckt.shape
    n_t = kn.shape[1]
    rows = n_t * GROUP
    blk = lambda *tail: pl.BlockSpec((SAMPLE_BB,) + tail, lambda i: (i,) + (0,) * len(tail))
    return pl.pallas_call(
        _s0_attn_kernel,
        grid=(n_seq // SAMPLE_BB,),
        in_specs=[blk(N_KV * rows, HEAD_DIM), blk(N_KV, HEAD_DIM, wb), blk(N_KV, HEAD_DIM, wb),
                  blk(n_t, KV_DIM), blk(n_t, KV_DIM), pl.BlockSpec((N_KV, rows, 1), lambda i: (0, 0, 0))],
        out_specs=[blk(N_KV * rows, HEAD_DIM), blk(N_KV, HEAD_DIM, wb), blk(N_KV, HEAD_DIM, wb)],
        out_shape=[jax.ShapeDtypeStruct((n_seq, N_KV * rows, HEAD_DIM), F32),
                   jax.ShapeDtypeStruct(ckt.shape, F32), jax.ShapeDtypeStruct(cvt.shape, F32)],
        scratch_shapes=[pltpu.VMEM((SAMPLE_BB, SAMPLE_NEW, KV_DIM), F32),
                        pltpu.VMEM((SAMPLE_BB, SAMPLE_NEW, KV_DIM), F32)],
        compiler_params=_params(1),
        name="s0_attn",
    )(q4, ckt, cvt, kn, vn, sk_rows)


def kernel(x_prompt, x_sample, state_conv, cache_k_win, cache_v_win, state_pool, meta_tokens, norm_mix, norm_mlp, w_in_even, conv_w, q_norm, k_norm, attn_sinks, w_out_even, w_pool, pool_scale, w_up, w_down):
    n_seq, n_t, _ = x_sample.shape
    wb = cache_k_win.shape[2]
    assert x_prompt.shape[1] % SEQ_TILE == 0 and x_prompt.shape[1] >= 2 * SEQ_TILE and n_seq % SAMPLE_BB == 0
    assert wb == WINDOW and n_t <= SAMPLE_NEW

    gmix = [norm_mix, norm_mix]
    gmlp = [norm_mlp, norm_mlp]
    qg = jnp.tile(q_norm[0], N_HEADS)[None, :]
    kg = jnp.tile(k_norm[0], N_KV)[None, :]
    seg = np.kron(np.eye(N_HEADS, dtype=np.float32), np.full((HEAD_DIM, HEAD_DIM), 1.0 / HEAD_DIM, np.float32))
    segq, segk = jnp.asarray(seg, BF16), jnp.asarray(seg[:KV_DIM, :KV_DIM], BF16)
    cw = _Stacked(conv_w, 0)
    sinks = attn_sinks[0]
    sds = lambda *shape, dtype=F32: jax.ShapeDtypeStruct(shape, dtype)

    n_rows = n_t * n_seq
    ya, q4, kn_b, vn_b, conv_s, win = _single_step(
        _s0_pre_kernel, "s0_pre",
        [sds(n_rows, D_CONV), sds(n_seq, N_HEADS * n_t, HEAD_DIM), sds(n_seq, n_t, KV_DIM),
         sds(n_seq, n_t, KV_DIM), sds(n_seq, CONV_W - 1, D_CONV), sds(D_MODEL, D_IN_EVEN, dtype=BF16)],
        x_sample, _Stacked(state_conv, 0), gmix[0], _Stacked(w_in_even, 0), segq, segk, qg, kg, cw)
    sk_rows = jnp.tile(sinks.reshape(N_KV, 1, GROUP), (1, n_t, 1)).reshape(N_KV, n_t * GROUP, 1)
    to_dev = lambda a: jnp.transpose(a.reshape(n_seq, -1, N_KV, HEAD_DIM), (0, 2, 3, 1))
    o4, kst, vst = _s0_attn(q4, to_dev(cache_k_win), to_dev(cache_v_win), kn_b, vn_b, sk_rows)
    k_s, v_s = (jnp.transpose(a, (0, 3, 1, 2))[None] for a in (kst, vst))
    xs2, wout, wup0, wdn0 = _streamed_mlp(
        _s0_post_kernel, "s0_post", n_rows, [sds(n_rows, D_MODEL), sds(D_MODEL, D_MODEL, dtype=BF16)],
        [x_sample, ya, o4, _Stacked(w_out_even, 0), gmlp[0]], w_up, w_down, 0)
    buf_t = jnp.transpose(state_pool[0], (1, 0, 2))
    y_sample, pool_s, wpool, wup1, wdn1 = _streamed_mlp(
        _s1_kernel, "s1", n_rows,
        [sds(n_seq, n_t, D_MODEL), sds(POOL_MAX - 1, n_seq, D_MODEL), sds(*w_pool.shape[1:], dtype=BF16)],
        [xs2, buf_t, gmix[1], _Stacked(w_pool, 0), pool_scale, gmlp[1]], w_up, w_down, 1)

    x2, x2_meta, conv_p, k_p, v_p = _l0_prompt(x_prompt, sinks, meta_tokens, gmix[0], win, segq, segk, qg, kg, cw,
                                               wout, gmlp[0], wup0, wdn0)
    y_prompt, pool_p = _l1_prompt(x2, x2_meta, gmix[1], wpool, pool_scale, gmlp[1], wup1, wdn1)

    kv5 = lambda a: a.reshape(1, a.shape[0], WINDOW, N_KV, HEAD_DIM)
    return (y_prompt, y_sample, conv_p[None], conv_s[None], kv5(k_p), k_s, kv5(v_p), v_s,
            pool_p[None], jnp.transpose(pool_s, (1, 0, 2))[None])
```

```python
import functools
from typing import NamedTuple

import jax
import jax.numpy as jnp
import numpy as np
from jax import lax
from jax.experimental import pallas as pl
from jax.experimental.pallas import tpu as pltpu

F32 = jnp.float32
BF16 = jnp.bfloat16

D_MODEL = 1024
D_CONV = 512
CONV_W = 3
HEAD_DIM = 64
N_HEADS = 8
N_KV = 2
GROUP = N_HEADS // N_KV
WINDOW = 128
BLOCK = 128
Q_DIM = N_HEADS * HEAD_DIM
KV_DIM = N_KV * HEAD_DIM
POOL_WINDOWS = (2, 4, 8, 16)
POOL_GROUP_DIM = D_MODEL // len(POOL_WINDOWS)
POOL_MAX = 16
D_FF = 4 * D_MODEL
D_IN_EVEN = 3 * D_CONV + Q_DIM + 2 * KV_DIM
N_META = 16
EPS = 1e-6
NEG = -1e30

META_PAD = BLOCK - N_META
SEQ_TILE = 512
FF_CHUNK = 1024
FF_STREAM = 512
CONV_HDR = 8
SAMPLE_BB = 16
SAMPLE_NEW = 8
VMEM_LIMIT = 56 * 1024 * 1024
MLP_YIELDS = 2 * (D_FF // FF_CHUNK)
L0_MIX_YIELDS = 1 + 2 * N_KV * (SEQ_TILE // BLOCK)


def _dot(a, b):
    return jnp.dot(a, b, preferred_element_type=F32)


def _dot_nt(a, b):
    return lax.dot_general(a, b, (((1,), (1,)), ((), ())), preferred_element_type=F32)


def _dot_tn(a, b):
    return lax.dot_general(a, b, (((0,), (0,)), ((), ())), preferred_element_type=F32)


def _rms(x, g):
    ms = jnp.mean(x * x, axis=-1, keepdims=True)
    return x * lax.rsqrt(ms + EPS) * g


def _head_rms(x, seg, g):
    ms = _dot((x * x).astype(BF16), seg)
    return x * lax.rsqrt(ms + EPS) * g


def _interleave(*steppers, shares=None):
    shares = shares or [1] * len(steppers)
    results = [None] * len(steppers)
    done = [0] * len(steppers)
    live = set(range(len(steppers)))
    while live:
        i = min(live, key=lambda k: ((done[k] + 1) / shares[k], k))
        try:
            next(steppers[i])
            done[i] += 1
        except StopIteration as finished:
            results[i] = finished.value
            live.discard(i)
    return results


def _mlp_steps(x, g, wup_ref, wdn_ref):
    xn = _rms(x, g).astype(BF16)
    acc = x
    for c in range(D_FF // FF_CHUNK):
        cols = slice(c * FF_CHUNK, (c + 1) * FF_CHUNK)
        h = _dot(xn, wup_ref[:, cols])
        a = jnp.square(jnp.maximum(h, 0.0)).astype(BF16)
        yield
        acc = acc + _dot(a, wdn_ref[cols, :])
        yield
    return acc


def _in_proj(x, g, win_ref, segq_ref, segk_ref, qg, kg):
    hn = _rms(x, g).astype(BF16)
    kv_col = 3 * D_CONV + Q_DIM
    half = x.shape[0] // 2
    z = _dot(hn, win_ref[:, 0:kv_col])
    kv = jnp.concatenate([_dot(hn[0:half], win_ref[:, kv_col:]), _dot(hn[half:], win_ref[:, kv_col:])], axis=0)
    xa = z[:, 0:D_CONV]
    gc = z[:, D_CONV:2 * D_CONV]
    gb = z[:, 2 * D_CONV:3 * D_CONV]
    q = z[:, 3 * D_CONV:]
    k = kv[:, 0:KV_DIM]
    v = kv[:, KV_DIM:]
    qn = _head_rms(q, segq_ref[...], qg) * (HEAD_DIM ** -0.5)
    kn = _head_rms(k, segk_ref[...], kg)
    return gc * xa, gb, qn, kn, v


def _l0_mixer(x, rows, sinks_ref, gmix, win_ref, segq_ref, segk_ref, qg, kg, cw_ref, wout_ref,
              u_scr, k_scr, v_scr, first_scr, band_scr):
    u, gb, qn, kn, v = _in_proj(x, gmix, win_ref, segq_ref, segk_ref, qg, kg)

    u_scr[CONV_HDR:CONV_HDR + rows, :] = u
    u1 = u_scr[CONV_HDR - 1:CONV_HDR - 1 + rows, :]
    u2 = u_scr[CONV_HDR - 2:CONV_HDR - 2 + rows, :]
    ya = gb * (u2 * cw_ref[0:1, :] + u1 * cw_ref[1:2, :] + u * cw_ref[2:3, :])
    k_scr[BLOCK:BLOCK + rows, :] = kn
    v_scr[BLOCK:BLOCK + rows, :] = v
    yield

    yb_blocks = []
    for i in range(rows // BLOCK):
        qb = qn[i * BLOCK:(i + 1) * BLOCK, :]
        bias = first_scr[...] if i == 0 else band_scr[...]
        heads_t = []
        for j in range(N_KV):
            lanes = slice(j * HEAD_DIM, (j + 1) * HEAD_DIM)
            k2 = k_scr[i * BLOCK:(i + 2) * BLOCK, lanes].astype(BF16)
            v2 = v_scr[i * BLOCK:(i + 2) * BLOCK, lanes].astype(BF16)
            qs = jnp.concatenate(
                [qb[:, (j * GROUP + g) * HEAD_DIM:(j * GROUP + g + 1) * HEAD_DIM] for g in range(GROUP)],
                axis=0).astype(BF16)
            sk = jnp.concatenate(
                [jnp.full((1, BLOCK), sinks_ref[j * GROUP + g], F32) for g in range(GROUP)], axis=1)
            st = _dot_nt(k2, qs) + bias
            m = jnp.maximum(jnp.max(st, axis=0, keepdims=True), sk)
            p = jnp.exp(st - m)
            den = jnp.sum(p, axis=0, keepdims=True) + jnp.exp(sk - m)
            yield
            ot = _dot_tn(v2, p.astype(BF16)) / den
            heads_t += [ot[:, g * BLOCK:(g + 1) * BLOCK] for g in range(GROUP)]
            yield
        yb_blocks.append(jnp.concatenate(heads_t, axis=0).T)
    yb = yb_blocks[0] if len(yb_blocks) == 1 else jnp.concatenate(yb_blocks, axis=0)

    u_scr[CONV_HDR - 2:CONV_HDR, :] = u_scr[CONV_HDR - 2 + rows:CONV_HDR + rows, :]
    k_scr[0:BLOCK, :] = k_scr[rows:rows + BLOCK, :]
    v_scr[0:BLOCK, :] = v_scr[rows:rows + BLOCK, :]

    mix = jnp.concatenate([ya, yb], axis=-1).astype(BF16)
    return x + _dot(mix, wout_ref[...])


def _pipeline_steps(s, x_ref, y_ref, x1_scr, mixer, mlp):
    slot = lax.rem(s, 2)

    def mix_tile():
        x1_scr[slot] = yield from mixer(x_ref[...], SEQ_TILE)

    def mlp_tile():
        y_ref[...] = yield from mlp(x1_scr[1 - slot])

    return mlp_tile, mix_tile


def _l0_prompt_kernel(sinks_ref, x_ref, meta_ref, gmix_ref, win_ref, segq_ref, segk_ref, qg_ref, kg_ref,
                      cw_ref, wout_ref, gmlp_ref, wup_ref, wdn_ref,
                      y_ref, ymeta_ref, conv_ref, kout_ref, vout_ref,
                      u_scr, k_scr, v_scr, first_scr, band_scr, x1_scr):
    s = pl.program_id(1)
    last = pl.num_programs(1) - 1
    mixer = functools.partial(
        _l0_mixer, sinks_ref=sinks_ref, gmix=gmix_ref[0:1, :], win_ref=win_ref, segq_ref=segq_ref,
        segk_ref=segk_ref, qg=qg_ref[...], kg=kg_ref[...], cw_ref=cw_ref, wout_ref=wout_ref,
        u_scr=u_scr, k_scr=k_scr, v_scr=v_scr, first_scr=first_scr, band_scr=band_scr)
    mlp = functools.partial(_mlp_steps, g=gmlp_ref[0:1, :], wup_ref=wup_ref, wdn_ref=wdn_ref)
    mlp_tile, mix_tile = _pipeline_steps(s, x_ref, y_ref, x1_scr, mixer, mlp)

    def band_bias(first_valid_key):
        c = lax.broadcasted_iota(jnp.int32, (2 * BLOCK, GROUP * BLOCK), 0)
        q = lax.broadcasted_iota(jnp.int32, (2 * BLOCK, GROUP * BLOCK), 1) & (BLOCK - 1)
        return jnp.where((c >= q) & (c <= q + WINDOW) & (c >= first_valid_key), 0.0, NEG)

    @pl.when(s == 0)
    def _start():
        band_scr[...] = band_bias(0)
        u_scr[0:CONV_HDR, :] = jnp.zeros((CONV_HDR, D_CONV), F32)
        k_scr[0:BLOCK, :] = jnp.zeros((BLOCK, KV_DIM), F32)
        v_scr[0:BLOCK, :] = jnp.zeros((BLOCK, KV_DIM), F32)
        first_scr[...] = band_bias(BLOCK + META_PAD)
        x_meta = jnp.concatenate([jnp.zeros((META_PAD, D_MODEL), F32), meta_ref[...]], axis=0)
        x1_meta, = _interleave(mixer(x_meta, BLOCK))
        first_scr[...] = band_bias(META_PAD)
        ymeta_ref[...], _ = _interleave(mlp(x1_meta), mix_tile(), shares=(MLP_YIELDS, L0_MIX_YIELDS))
        first_scr[...] = band_scr[...]

    @pl.when((s > 0) & (s < last))
    def _steady():
        _interleave(mlp_tile(), mix_tile(), shares=(MLP_YIELDS, L0_MIX_YIELDS))

    @pl.when(s == last)
    def _drain():
        _interleave(mlp_tile())

    @pl.when(s == last - 1)
    def _state():
        conv_ref[...] = u_scr[CONV_HDR - 2:CONV_HDR, :]
        kout_ref[...] = k_scr[0:BLOCK, :]
        vout_ref[...] = v_scr[0:BLOCK, :]


def _l1_mixer(x, rows, is_meta, gmix, wpool_ref, pscale, h_scr):
    h = _rms(x, gmix)
    h_scr[POOL_MAX:POOL_MAX + rows, :] = h
    ys = []
    for gi, w in enumerate(POOL_WINDOWS):
        cols = slice(gi * POOL_GROUP_DIM, (gi + 1) * POOL_GROUP_DIM)
        hg = h[:, cols]
        acc = hg
        for j in range(1, w):
            acc = acc + h_scr[POOL_MAX - j:POOL_MAX - j + rows, cols]
        if is_meta:
            r = lax.broadcasted_iota(jnp.int32, (rows, 1), 0)
            pooled = acc / jnp.clip(r - (META_PAD - 1), 1, w).astype(F32)
        else:
            pooled = acc * (1.0 / w)
        ys.append(_dot((pooled - hg).astype(BF16), wpool_ref[gi]))
        yield
    h_scr[0:POOL_MAX, :] = h_scr[rows:rows + POOL_MAX, :]
    return x + jnp.concatenate(ys, axis=-1) * pscale


def _l1_prompt_kernel(x_ref, meta_ref, gmix_ref, wpool_ref, pscale_ref, gmlp_ref, wup_ref, wdn_ref,
                      y_ref, pool_ref, h_scr, x1_scr):
    s = pl.program_id(1)
    last = pl.num_programs(1) - 1
    mixer = functools.partial(_l1_mixer, gmix=gmix_ref[1:2, :], wpool_ref=wpool_ref, pscale=pscale_ref[...],
                              h_scr=h_scr)
    mlp = functools.partial(_mlp_steps, g=gmlp_ref[1:2, :], wup_ref=wup_ref, wdn_ref=wdn_ref)
    mlp_tile, mix_tile = _pipeline_steps(s, x_ref, y_ref, x1_scr,
                                         functools.partial(mixer, is_meta=False), mlp)

    @pl.when(s == 0)
    def _start():
        h_scr[0:POOL_MAX, :] = jnp.zeros((POOL_MAX, D_MODEL), F32)
        _interleave(mixer(meta_ref[...], BLOCK, True))
        _interleave(mix_tile())

    @pl.when((s > 0) & (s < last))
    def _steady():
        _interleave(mlp_tile(), mix_tile(), shares=(MLP_YIELDS, len(POOL_WINDOWS)))

    @pl.when(s == last)
    def _drain():
        _interleave(mlp_tile())

    @pl.when(s == last - 1)
    def _state():
        pool_ref[...] = h_scr[1:POOL_MAX, :]


def _time_major(ref):
    return jnp.concatenate([ref[:, t, :] for t in range(ref.shape[1])], axis=0)


def _s0_pre_kernel(x_ref, st_ref, gmix_ref, win_ref, segq_ref, segk_ref, qg_ref, kg_ref, cw_ref,
                   ya_ref, q_ref, k_ref, v_ref, conv_ref, win_bf_ref):
    n_seq, n_t, _ = x_ref.shape
    win_bf_ref[...] = win_ref[...].astype(BF16)
    u, gb, qn, kn, v = _in_proj(_time_major(x_ref), gmix_ref[0:1, :], win_bf_ref, segq_ref, segk_ref, qg_ref[...],
                                kg_ref[...])
    rows = lambda a, t: a[t * n_seq:(t + 1) * n_seq, :]
    ue = [st_ref[:, i, :] for i in range(CONV_W - 1)] + [rows(u, t) for t in range(n_t)]
    for t in range(n_t):
        y = ue[t] * cw_ref[0:1, :] + ue[t + 1] * cw_ref[1:2, :] + ue[t + 2] * cw_ref[2:3, :]
        ya_ref[t * n_seq:(t + 1) * n_seq, :] = rows(gb, t) * y
        k_ref[:, t, :] = rows(kn, t)
        v_ref[:, t, :] = rows(v, t)
        for h in range(N_HEADS):
            q_ref[:, (h // GROUP) * n_t * GROUP + t * GROUP + h % GROUP, :] = (
                rows(qn, t)[:, h * HEAD_DIM:(h + 1) * HEAD_DIM])
    for i in range(CONV_W - 1):
        conv_ref[:, i, :] = ue[n_t + i]


def _s0_attn_kernel(q_ref, ckt_ref, cvt_ref, kn_ref, vn_ref, sk_ref, o_ref, kst_ref, vst_ref, kn_scr, vn_scr):
    bb, n_t = kn_ref.shape[0], kn_ref.shape[1]
    wb = ckt_ref.shape[3]
    rows = n_t * GROUP
    ext = []
    for old_ref, new_ref, scr, out_ref in ((ckt_ref, kn_ref, kn_scr, kst_ref), (cvt_ref, vn_ref, vn_scr, vst_ref)):
        scr[...] = jnp.zeros(scr.shape, F32)
        scr[:, 0:n_t, :] = new_ref[...]
        new_t = jnp.swapaxes(scr[...], 1, 2).reshape(bb, N_KV, HEAD_DIM, SAMPLE_NEW)
        ext.append(jnp.concatenate([old_ref[...], new_t], axis=-1))
        out_ref[...] = ext[-1][:, :, :, n_t:n_t + wb]
    t = lax.broadcasted_iota(jnp.int32, (rows, wb + SAMPLE_NEW), 0) // GROUP
    c = lax.broadcasted_iota(jnp.int32, (rows, wb + SAMPLE_NEW), 1)
    bias = jnp.where((c <= wb + t) & (c >= wb + t - WINDOW), 0.0, NEG)[None]
    for j in range(N_KV):
        q = q_ref[:, j * rows:(j + 1) * rows, :].astype(BF16)
        sk = sk_ref[j][None]
        s = jnp.einsum('bqd,bdk->bqk', q, ext[0][:, j].astype(BF16), preferred_element_type=F32) + bias
        m = jnp.maximum(jnp.max(s, axis=-1, keepdims=True), sk)
        p = jnp.exp(s - m)
        den = jnp.sum(p, axis=-1, keepdims=True) + jnp.exp(sk - m)
        o = jnp.einsum('bqk,bdk->bqd', p.astype(BF16), ext[1][:, j].astype(BF16), preferred_element_type=F32)
        o_ref[:, j * rows:(j + 1) * rows, :] = o / den


def _mlp_stream_step(c, wup_ref, wdn_ref, y_ref, wup_bf_ref, wdn_bf_ref, x1_scr, xn_scr):
    wup_bf_ref[...] = wup_ref[...].astype(BF16)
    wdn_bf_ref[...] = wdn_ref[...].astype(BF16)
    a = jnp.square(jnp.maximum(_dot(xn_scr[...], wup_bf_ref[...]), 0.0)).astype(BF16)
    x1_scr[...] += _dot(a, wdn_bf_ref[...])
    if y_ref is not None:
        @pl.when(c == pl.num_programs(0) - 1)
        def _done():
            y_ref[...] = x1_scr[...]


def _s0_post_kernel(x_ref, ya_ref, o_ref, wout_ref, gmlp_ref, wup_ref, wdn_ref,
                    y_ref, wout_bf_ref, wup_bf_ref, wdn_bf_ref, x1_scr, xn_scr):
    c = pl.program_id(0)

    @pl.when(c == 0)
    def _mix():
        n_t = x_ref.shape[1]
        wout_bf_ref[...] = wout_ref[...].astype(BF16)
        yb = jnp.concatenate(
            [jnp.concatenate([o_ref[:, (h // GROUP) * n_t * GROUP + t * GROUP + h % GROUP, :]
                              for h in range(N_HEADS)], axis=-1) for t in range(n_t)], axis=0)
        mix = jnp.concatenate([ya_ref[...], yb], axis=-1).astype(BF16)
        x1 = _time_major(x_ref) + _dot(mix, wout_bf_ref[...])
        x1_scr[...] = x1
        xn_scr[...] = _rms(x1, gmlp_ref[0:1, :]).astype(BF16)

    _mlp_stream_step(c, wup_ref, wdn_ref, y_ref, wup_bf_ref, wdn_bf_ref, x1_scr, xn_scr)


def _s1_kernel(x_ref, buf_ref, gmix_ref, wpool_ref, pscale_ref, gmlp_ref, wup_ref, wdn_ref,
               y_ref, pool_ref, wpool_bf_ref, wup_bf_ref, wdn_bf_ref, x1_scr, xn_scr):
    c = pl.program_id(0)
    n_buf, n_seq, _ = buf_ref.shape
    n_t = x_ref.shape[0] // n_seq

    @pl.when(c == 0)
    def _mix():
        wpool_bf_ref[...] = wpool_ref[...].astype(BF16)
        x = x_ref[...]
        h = _rms(x, gmix_ref[1:2, :])
        he = [buf_ref[i] for i in range(n_buf)] + [h[t * n_seq:(t + 1) * n_seq, :] for t in range(n_t)]
        for i in range(n_buf):
            pool_ref[i] = he[n_t + i]
        ys = []
        for gi, w in enumerate(POOL_WINDOWS):
            cols = slice(gi * POOL_GROUP_DIM, (gi + 1) * POOL_GROUP_DIM)
            dl = []
            for t in range(n_t):
                acc = he[n_buf + t][:, cols]
                for j in range(1, w):
                    acc = acc + he[n_buf + t - j][:, cols]
                dl.append(acc * (1.0 / w) - he[n_buf + t][:, cols])
            ys.append(_dot(jnp.concatenate(dl, axis=0).astype(BF16), wpool_bf_ref[gi]))
        x1 = x + jnp.concatenate(ys, axis=-1) * pscale_ref[...]
        x1_scr[...] = x1
        xn_scr[...] = _rms(x1, gmlp_ref[1:2, :]).astype(BF16)

    _mlp_stream_step(c, wup_ref, wdn_ref, None, wup_bf_ref, wdn_bf_ref, x1_scr, xn_scr)

    @pl.when(c == pl.num_programs(0) - 1)
    def _done():
        for t in range(n_t):
            y_ref[:, t, :] = x1_scr[t * n_seq:(t + 1) * n_seq, :]


class _Stacked(NamedTuple):
    array: jax.Array
    layer: int


def _operand(arg):
    return arg.array if isinstance(arg, _Stacked) else arg


def _resident(arg, n_grid):
    if isinstance(arg, _Stacked):
        shape, idx = (None,) + arg.array.shape[1:], (arg.layer,) + (0,) * (arg.array.ndim - 1)
    else:
        shape, idx = arg.shape, (0,) * arg.ndim
    index_map = {1: lambda i: idx, 2: lambda i, j: idx}[n_grid]
    return pl.BlockSpec(shape, index_map, pipeline_mode=pl.Buffered(1))


def _params(n_grid):
    return pltpu.CompilerParams(dimension_semantics=("arbitrary",) * n_grid, vmem_limit_bytes=VMEM_LIMIT)


def _prompt_specs(n_tiles):
    tile_in = pl.BlockSpec((None, SEQ_TILE, D_MODEL), lambda b, s: (b, jnp.minimum(s, n_tiles - 1), 0))
    tile_out = pl.BlockSpec((None, SEQ_TILE, D_MODEL), lambda b, s: (b, jnp.maximum(s - 1, 0), 0))
    per_b = lambda r, c: pl.BlockSpec((None, r, c), lambda b, s: (b, 0, 0))
    return tile_in, tile_out, per_b


def _l0_prompt(x, sinks, *resident):
    n_b, seq, _ = x.shape
    n_tiles = seq // SEQ_TILE
    tile_in, tile_out, per_b = _prompt_specs(n_tiles)
    return pl.pallas_call(
        _l0_prompt_kernel,
        grid=(n_b, n_tiles + 1),
        in_specs=[pl.BlockSpec(memory_space=pltpu.SMEM), tile_in] + [_resident(a, 2) for a in resident],
        out_specs=[tile_out, per_b(BLOCK, D_MODEL), per_b(CONV_W - 1, D_CONV), per_b(WINDOW, KV_DIM),
                   per_b(WINDOW, KV_DIM)],
        out_shape=[jax.ShapeDtypeStruct((n_b, seq, D_MODEL), F32),
                   jax.ShapeDtypeStruct((n_b, BLOCK, D_MODEL), F32),
                   jax.ShapeDtypeStruct((n_b, CONV_W - 1, D_CONV), F32),
                   jax.ShapeDtypeStruct((n_b, WINDOW, KV_DIM), F32),
                   jax.ShapeDtypeStruct((n_b, WINDOW, KV_DIM), F32)],
        scratch_shapes=[pltpu.VMEM((CONV_HDR + SEQ_TILE, D_CONV), F32),
                        pltpu.VMEM((BLOCK + SEQ_TILE, KV_DIM), F32),
                        pltpu.VMEM((BLOCK + SEQ_TILE, KV_DIM), F32),
                        pltpu.VMEM((2 * BLOCK, GROUP * BLOCK), F32),
                        pltpu.VMEM((2 * BLOCK, GROUP * BLOCK), F32),
                        pltpu.VMEM((2, SEQ_TILE, D_MODEL), F32)],
        compiler_params=_params(2),
        name="l0_prompt",
    )(sinks, x, *[_operand(a) for a in resident])


def _l1_prompt(x, xmeta, *resident):
    n_b, seq, _ = x.shape
    n_tiles = seq // SEQ_TILE
    tile_in, tile_out, per_b = _prompt_specs(n_tiles)
    return pl.pallas_call(
        _l1_prompt_kernel,
        grid=(n_b, n_tiles + 1),
        in_specs=[tile_in, per_b(BLOCK, D_MODEL)] + [_resident(a, 2) for a in resident],
        out_specs=[tile_out, per_b(POOL_MAX - 1, D_MODEL)],
        out_shape=[jax.ShapeDtypeStruct((n_b, seq, D_MODEL), F32),
                   jax.ShapeDtypeStruct((n_b, POOL_MAX - 1, D_MODEL), F32)],
        scratch_shapes=[pltpu.VMEM((POOL_MAX + SEQ_TILE, D_MODEL), F32),
                        pltpu.VMEM((2, SEQ_TILE, D_MODEL), F32)],
        compiler_params=_params(2),
        name="l1_prompt",
    )(x, xmeta, *[_operand(a) for a in resident])


def _single_step(kernel, name, out_shape, *args):
    return pl.pallas_call(
        kernel,
        grid=(1,),
        in_specs=[_resident(a, 1) for a in args],
        out_specs=[pl.BlockSpec(o.shape, lambda i, n=len(o.shape): (0,) * n) for o in out_shape],
        out_shape=out_shape,
        compiler_params=_params(1),
        name=name,
    )(*[_operand(a) for a in args])


def _streamed_mlp(kernel, name, n_rows, outs, consts, w_up, w_down, layer):
    const = lambda shape: pl.BlockSpec(shape, lambda c, n=len(shape): (0,) * n)
    return pl.pallas_call(
        kernel,
        grid=(D_FF // FF_STREAM,),
        in_specs=[_resident(a, 1) for a in consts]
        + [pl.BlockSpec((None, D_MODEL, FF_STREAM), lambda c: (layer, 0, c)),
           pl.BlockSpec((None, FF_STREAM, D_MODEL), lambda c: (layer, c, 0))],
        out_specs=[const(o.shape) for o in outs]
        + [pl.BlockSpec((D_MODEL, FF_STREAM), lambda c: (0, c)),
           pl.BlockSpec((FF_STREAM, D_MODEL), lambda c: (c, 0))],
        out_shape=list(outs) + [jax.ShapeDtypeStruct((D_MODEL, D_FF), BF16),
                                jax.ShapeDtypeStruct((D_FF, D_MODEL), BF16)],
        scratch_shapes=[pltpu.VMEM((n_rows, D_MODEL), F32), pltpu.VMEM((n_rows, D_MODEL), BF16)],
        compiler_params=_params(1),
        name=name,
    )(*[_operand(a) for a in consts], w_up, w_down)


def _s0_attn(q4, ckt, cvt, kn, vn, sk_rows):
    n_seq, _, _, wb = ckt.shape
    n_t = kn.shape[1]
    rows = n_t * GROUP
    blk = lambda *tail: pl.BlockSpec((SAMPLE_BB,) + tail, lambda i: (i,) + (0,) * len(tail))
    return pl.pallas_call(
        _s0_attn_kernel,
        grid=(n_seq // SAMPLE_BB,),
        in_specs=[blk(N_KV * rows, HEAD_DIM), blk(N_KV, HEAD_DIM, wb), blk(N_KV, HEAD_DIM, wb),
                  blk(n_t, KV_DIM), blk(n_t, KV_DIM), pl.BlockSpec((N_KV, rows, 1), lambda i: (0, 0, 0))],
        out_specs=[blk(N_KV * rows, HEAD_DIM), blk(N_KV, HEAD_DIM, wb), blk(N_KV, HEAD_DIM, wb)],
        out_shape=[jax.ShapeDtypeStruct((n_seq, N_KV * rows, HEAD_DIM), F32),
                   jax.ShapeDtypeStruct(ckt.shape, F32), jax.ShapeDtypeStruct(cvt.shape, F32)],
        scratch_shapes=[pltpu.VMEM((SAMPLE_BB, SAMPLE_NEW, KV_DIM), F32),
                        pltpu.VMEM((SAMPLE_BB, SAMPLE_NEW, KV_DIM), F32)],
        compiler_params=_params(1),
        name="s0_attn",
    )(q4, ckt, cvt, kn, vn, sk_rows)


def kernel(x_prompt, x_sample, state_conv, cache_k_win, cache_v_win, state_pool, meta_tokens, norm_mix, norm_mlp, w_in_even, conv_w, q_norm, k_norm, attn_sinks, w_out_even, w_pool, pool_scale, w_up, w_down):
    n_seq, n_t, _ = x_sample.shape
    wb = cache_k_win.shape[2]
    assert x_prompt.shape[1] % SEQ_TILE == 0 and x_prompt.shape[1] >= 2 * SEQ_TILE and n_seq % SAMPLE_BB == 0
    assert wb == WINDOW and n_t <= SAMPLE_NEW

    gmix = [norm_mix, norm_mix]
    gmlp = [norm_mlp, norm_mlp]
    qg = jnp.tile(q_norm[0], N_HEADS)[None, :]
    kg = jnp.tile(k_norm[0], N_KV)[None, :]
    seg = np.kron(np.eye(N_HEADS, dtype=np.float32), np.full((HEAD_DIM, HEAD_DIM), 1.0 / HEAD_DIM, np.float32))
    segq, segk = jnp.asarray(seg, BF16), jnp.asarray(seg[:KV_DIM, :KV_DIM], BF16)
    cw = _Stacked(conv_w, 0)
    sinks = attn_sinks[0]
    sds = lambda *shape, dtype=F32: jax.ShapeDtypeStruct(shape, dtype)

    n_rows = n_t * n_seq
    ya, q4, kn_b, vn_b, conv_s, win = _single_step(
        _s0_pre_kernel, "s0_pre",
        [sds(n_rows, D_CONV), sds(n_seq, N_HEADS * n_t, HEAD_DIM), sds(n_seq, n_t, KV_DIM),
         sds(n_seq, n_t, KV_DIM), sds(n_seq, CONV_W - 1, D_CONV), sds(D_MODEL, D_IN_EVEN, dtype=BF16)],
        x_sample, _Stacked(state_conv, 0), gmix[0], _Stacked(w_in_even, 0), segq, segk, qg, kg, cw)
    sk_rows = jnp.tile(sinks.reshape(N_KV, 1, GROUP), (1, n_t, 1)).reshape(N_KV, n_t * GROUP, 1)
    to_dev = lambda a: jnp.transpose(a.reshape(n_seq, -1, N_KV, HEAD_DIM), (0, 2, 3, 1))
    o4, kst, vst = _s0_attn(q4, to_dev(cache_k_win), to_dev(cache_v_win), kn_b, vn_b, sk_rows)
    k_s, v_s = (jnp.transpose(a, (0, 3, 1, 2))[None] for a in (kst, vst))
    xs2, wout, wup0, wdn0 = _streamed_mlp(
        _s0_post_kernel, "s0_post", n_rows, [sds(n_rows, D_MODEL), sds(D_MODEL, D_MODEL, dtype=BF16)],
        [x_sample, ya, o4, _Stacked(w_out_even, 0), gmlp[0]], w_up, w_down, 0)
    buf_t = jnp.transpose(state_pool[0], (1, 0, 2))
    y_sample, pool_s, wpool, wup1, wdn1 = _streamed_mlp(
        _s1_kernel, "s1", n_rows,
        [sds(n_seq, n_t, D_MODEL), sds(POOL_MAX - 1, n_seq, D_MODEL), sds(*w_pool.shape[1:], dtype=BF16)],
        [xs2, buf_t, gmix[1], _Stacked(w_pool, 0), pool_scale, gmlp[1]], w_up, w_down, 1)

    x2, x2_meta, conv_p, k_p, v_p = _l0_prompt(x_prompt, sinks, meta_tokens, gmix[0], win, segq, segk, qg, kg, cw,
                                               wout, gmlp[0], wup0, wdn0)
    y_prompt, pool_p = _l1_prompt(x2, x2_meta, gmix[1], wpool, pool_scale, gmlp[1], wup1, wdn1)

    kv5 = lambda a: a.reshape(1, a.shape[0], WINDOW, N_KV, HEAD_DIM)
    return (y_prompt, y_sample, conv_p[None], conv_s[None], kv5(k_p), k_s, kv5(v_p), v_s,
            pool_p[None], jnp.transpose(pool_s, (1, 0, 2))[None])
```

```python
import functools
from typing import NamedTuple

import jax
import jax.numpy as jnp
import numpy as np
from jax import lax
from jax.experimental import pallas as pl
from jax.experimental.pallas import tpu as pltpu

F32 = jnp.float32
BF16 = jnp.bfloat16

D_MODEL = 1024
D_CONV = 512
CONV_W = 3
HEAD_DIM = 64
N_HEADS = 8
N_KV = 2
GROUP = N_HEADS // N_KV
WINDOW = 128
BLOCK = 128
Q_DIM = N_HEADS * HEAD_DIM
KV_DIM = N_KV * HEAD_DIM
POOL_WINDOWS = (2, 4, 8, 16)
POOL_GROUP_DIM = D_MODEL // len(POOL_WINDOWS)
POOL_MAX = 16
D_FF = 4 * D_MODEL
D_IN_EVEN = 3 * D_CONV + Q_DIM + 2 * KV_DIM
N_META = 16
EPS = 1e-6
NEG = -1e30

META_PAD = BLOCK - N_META
SEQ_TILE = 512
FF_CHUNK = 1024
FF_STREAM = 512
PRE_CHUNK = 768
CONV_HDR = 8
SAMPLE_BB = 16
SAMPLE_NEW = 8
VMEM_LIMIT = 56 * 1024 * 1024
MLP_YIELDS = 2 * (D_FF // FF_CHUNK)
L0_MIX_YIELDS = 1 + 2 * N_KV * (SEQ_TILE // BLOCK)


def _dot(a, b):
    return jnp.dot(a, b, preferred_element_type=F32)


def _dot_nt(a, b):
    return lax.dot_general(a, b, (((1,), (1,)), ((), ())), preferred_element_type=F32)


def _dot_tn(a, b):
    return lax.dot_general(a, b, (((0,), (0,)), ((), ())), preferred_element_type=F32)


def _rms(x, g):
    ms = jnp.mean(x * x, axis=-1, keepdims=True)
    return x * lax.rsqrt(ms + EPS) * g


def _head_rms(x, seg, g):
    ms = _dot((x * x).astype(BF16), seg)
    return x * lax.rsqrt(ms + EPS) * g


def _interleave(*steppers, shares=None):
    shares = shares or [1] * len(steppers)
    results = [None] * len(steppers)
    done = [0] * len(steppers)
    live = set(range(len(steppers)))
    while live:
        i = min(live, key=lambda k: ((done[k] + 1) / shares[k], k))
        try:
            next(steppers[i])
            done[i] += 1
        except StopIteration as finished:
            results[i] = finished.value
            live.discard(i)
    return results


def _mlp_steps(x, g, wup_ref, wdn_ref):
    xn = _rms(x, g).astype(BF16)
    acc = x
    for c in range(D_FF // FF_CHUNK):
        cols = slice(c * FF_CHUNK, (c + 1) * FF_CHUNK)
        h = _dot(xn, wup_ref[:, cols])
        a = jnp.square(jnp.maximum(h, 0.0)).astype(BF16)
        yield
        acc = acc + _dot(a, wdn_ref[cols, :])
        yield
    return acc


def _in_proj(x, g, win_ref, segq_ref, segk_ref, qg, kg):
    hn = _rms(x, g).astype(BF16)
    kv_col = 3 * D_CONV + Q_DIM
    half = x.shape[0] // 2
    z = _dot(hn, win_ref[:, 0:kv_col])
    kv = jnp.concatenate([_dot(hn[0:half], win_ref[:, kv_col:]), _dot(hn[half:], win_ref[:, kv_col:])], axis=0)
    return _split_norm(z, kv, segq_ref, segk_ref, qg, kg)


def _split_norm(z, kv, segq_ref, segk_ref, qg, kg):
    xa = z[:, 0:D_CONV]
    gc = z[:, D_CONV:2 * D_CONV]
    gb = z[:, 2 * D_CONV:3 * D_CONV]
    q = z[:, 3 * D_CONV:]
    k = kv[:, 0:KV_DIM]
    v = kv[:, KV_DIM:]
    qn = _head_rms(q, segq_ref[...], qg) * (HEAD_DIM ** -0.5)
    kn = _head_rms(k, segk_ref[...], kg)
    return gc * xa, gb, qn, kn, v


def _l0_mixer(x, rows, sinks_ref, gmix, win_ref, segq_ref, segk_ref, qg, kg, cw_ref, wout_ref,
              u_scr, k_scr, v_scr, first_scr, band_scr):
    u, gb, qn, kn, v = _in_proj(x, gmix, win_ref, segq_ref, segk_ref, qg, kg)

    u_scr[CONV_HDR:CONV_HDR + rows, :] = u
    u1 = u_scr[CONV_HDR - 1:CONV_HDR - 1 + rows, :]
    u2 = u_scr[CONV_HDR - 2:CONV_HDR - 2 + rows, :]
    ya = gb * (u2 * cw_ref[0:1, :] + u1 * cw_ref[1:2, :] + u * cw_ref[2:3, :])
    k_scr[BLOCK:BLOCK + rows, :] = kn
    v_scr[BLOCK:BLOCK + rows, :] = v
    yield

    yb_blocks = []
    for i in range(rows // BLOCK):
        qb = qn[i * BLOCK:(i + 1) * BLOCK, :]
        bias = first_scr[...] if i == 0 else band_scr[...]
        heads_t = []
        for j in range(N_KV):
            lanes = slice(j * HEAD_DIM, (j + 1) * HEAD_DIM)
            k2 = k_scr[i * BLOCK:(i + 2) * BLOCK, lanes].astype(BF16)
            v2 = v_scr[i * BLOCK:(i + 2) * BLOCK, lanes].astype(BF16)
            qs = jnp.concatenate(
                [qb[:, (j * GROUP + g) * HEAD_DIM:(j * GROUP + g + 1) * HEAD_DIM] for g in range(GROUP)],
                axis=0).astype(BF16)
            sk = jnp.concatenate(
                [jnp.full((1, BLOCK), sinks_ref[j * GROUP + g], F32) for g in range(GROUP)], axis=1)
            st = _dot_nt(k2, qs) + bias
            m = jnp.maximum(jnp.max(st, axis=0, keepdims=True), sk)
            p = jnp.exp(st - m)
            den = jnp.sum(p, axis=0, keepdims=True) + jnp.exp(sk - m)
            yield
            ot = _dot_tn(v2, p.astype(BF16)) / den
            heads_t += [ot[:, g * BLOCK:(g + 1) * BLOCK] for g in range(GROUP)]
            yield
        yb_blocks.append(jnp.concatenate(heads_t, axis=0).T)
    yb = yb_blocks[0] if len(yb_blocks) == 1 else jnp.concatenate(yb_blocks, axis=0)

    u_scr[CONV_HDR - 2:CONV_HDR, :] = u_scr[CONV_HDR - 2 + rows:CONV_HDR + rows, :]
    k_scr[0:BLOCK, :] = k_scr[rows:rows + BLOCK, :]
    v_scr[0:BLOCK, :] = v_scr[rows:rows + BLOCK, :]

    mix = jnp.concatenate([ya, yb], axis=-1).astype(BF16)
    return x + _dot(mix, wout_ref[...])


def _pipeline_steps(s, x_ref, y_ref, x1_scr, mixer, mlp):
    slot = lax.rem(s, 2)

    def mix_tile():
        x1_scr[slot] = yield from mixer(x_ref[...], SEQ_TILE)

    def mlp_tile():
        y_ref[...] = yield from mlp(x1_scr[1 - slot])

    return mlp_tile, mix_tile


def _l0_prompt_kernel(sinks_ref, x_ref, meta_ref, gmix_ref, win_ref, segq_ref, segk_ref, qg_ref, kg_ref,
                      cw_ref, wout_ref, gmlp_ref, wup_ref, wdn_ref,
                      y_ref, ymeta_ref, conv_ref, kout_ref, vout_ref,
                      u_scr, k_scr, v_scr, first_scr, band_scr, x1_scr):
    s = pl.program_id(1)
    last = pl.num_programs(1) - 1
    mixer = functools.partial(
        _l0_mixer, sinks_ref=sinks_ref, gmix=gmix_ref[0:1, :], win_ref=win_ref, segq_ref=segq_ref,
        segk_ref=segk_ref, qg=qg_ref[...], kg=kg_ref[...], cw_ref=cw_ref, wout_ref=wout_ref,
        u_scr=u_scr, k_scr=k_scr, v_scr=v_scr, first_scr=first_scr, band_scr=band_scr)
    mlp = functools.partial(_mlp_steps, g=gmlp_ref[0:1, :], wup_ref=wup_ref, wdn_ref=wdn_ref)
    mlp_tile, mix_tile = _pipeline_steps(s, x_ref, y_ref, x1_scr, mixer, mlp)

    def band_bias(first_valid_key):
        c = lax.broadcasted_iota(jnp.int32, (2 * BLOCK, GROUP * BLOCK), 0)
        q = lax.broadcasted_iota(jnp.int32, (2 * BLOCK, GROUP * BLOCK), 1) & (BLOCK - 1)
        return jnp.where((c >= q) & (c <= q + WINDOW) & (c >= first_valid_key), 0.0, NEG)

    @pl.when(s == 0)
    def _start():
        band_scr[...] = band_bias(0)
        u_scr[0:CONV_HDR, :] = jnp.zeros((CONV_HDR, D_CONV), F32)
        k_scr[0:BLOCK, :] = jnp.zeros((BLOCK, KV_DIM), F32)
        v_scr[0:BLOCK, :] = jnp.zeros((BLOCK, KV_DIM), F32)
        first_scr[...] = band_bias(BLOCK + META_PAD)
        x_meta = jnp.concatenate([jnp.zeros((META_PAD, D_MODEL), F32), meta_ref[...]], axis=0)
        x1_meta, = _interleave(mixer(x_meta, BLOCK))
        first_scr[...] = band_bias(META_PAD)
        ymeta_ref[...], _ = _interleave(mlp(x1_meta), mix_tile(), shares=(MLP_YIELDS, L0_MIX_YIELDS))
        first_scr[...] = band_scr[...]

    @pl.when((s > 0) & (s < last))
    def _steady():
        _interleave(mlp_tile(), mix_tile(), shares=(MLP_YIELDS, L0_MIX_YIELDS))

    @pl.when(s == last)
    def _drain():
        _interleave(mlp_tile())

    @pl.when(s == last - 1)
    def _state():
        conv_ref[...] = u_scr[CONV_HDR - 2:CONV_HDR, :]
        kout_ref[...] = k_scr[0:BLOCK, :]
        vout_ref[...] = v_scr[0:BLOCK, :]


def _l1_mixer(x, rows, is_meta, gmix, wpool_ref, pscale, h_scr):
    h = _rms(x, gmix)
    h_scr[POOL_MAX:POOL_MAX + rows, :] = h
    ys = []
    for gi, w in enumerate(POOL_WINDOWS):
        cols = slice(gi * POOL_GROUP_DIM, (gi + 1) * POOL_GROUP_DIM)
        hg = h[:, cols]
        acc = hg
        for j in range(1, w):
            acc = acc + h_scr[POOL_MAX - j:POOL_MAX - j + rows, cols]
        if is_meta:
            r = lax.broadcasted_iota(jnp.int32, (rows, 1), 0)
            pooled = acc / jnp.clip(r - (META_PAD - 1), 1, w).astype(F32)
        else:
            pooled = acc * (1.0 / w)
        ys.append(_dot((pooled - hg).astype(BF16), wpool_ref[gi]))
        yield
    h_scr[0:POOL_MAX, :] = h_scr[rows:rows + POOL_MAX, :]
    return x + jnp.concatenate(ys, axis=-1) * pscale


def _l1_prompt_kernel(x_ref, meta_ref, gmix_ref, wpool_ref, pscale_ref, gmlp_ref, wup_ref, wdn_ref,
                      y_ref, pool_ref, h_scr, x1_scr):
    s = pl.program_id(1)
    last = pl.num_programs(1) - 1
    mixer = functools.partial(_l1_mixer, gmix=gmix_ref[1:2, :], wpool_ref=wpool_ref, pscale=pscale_ref[...],
                              h_scr=h_scr)
    mlp = functools.partial(_mlp_steps, g=gmlp_ref[1:2, :], wup_ref=wup_ref, wdn_ref=wdn_ref)
    mlp_tile, mix_tile = _pipeline_steps(s, x_ref, y_ref, x1_scr,
                                         functools.partial(mixer, is_meta=False), mlp)

    @pl.when(s == 0)
    def _start():
        h_scr[0:POOL_MAX, :] = jnp.zeros((POOL_MAX, D_MODEL), F32)
        _interleave(mixer(meta_ref[...], BLOCK, True))
        _interleave(mix_tile())

    @pl.when((s > 0) & (s < last))
    def _steady():
        _interleave(mlp_tile(), mix_tile(), shares=(MLP_YIELDS, len(POOL_WINDOWS)))

    @pl.when(s == last)
    def _drain():
        _interleave(mlp_tile())

    @pl.when(s == last - 1)
    def _state():
        pool_ref[...] = h_scr[1:POOL_MAX, :]


def _time_major(ref):
    return jnp.concatenate([ref[:, t, :] for t in range(ref.shape[1])], axis=0)


def _s0_pre_kernel(x_ref, st_ref, gmix_ref, win_ref, segq_ref, segk_ref, qg_ref, kg_ref, cw_ref,
                   ya_ref, q_ref, k_ref, v_ref, conv_ref, win_bf_ref, hn_scr, z_scr):
    c = pl.program_id(0)
    n_seq, n_t, _ = x_ref.shape
    chunk = win_ref.shape[1]

    @pl.when(c == 0)
    def _norm():
        hn_scr[...] = _rms(_time_major(x_ref), gmix_ref[0:1, :]).astype(BF16)

    win_bf_ref[...] = win_ref[...].astype(BF16)
    for ci in range(z_scr.shape[1] // chunk):
        @pl.when(c == ci)
        def _project(ci=ci):
            z_scr[:, ci * chunk:(ci + 1) * chunk] = _dot(hn_scr[...], win_bf_ref[...])

    @pl.when(c == pl.num_programs(0) - 1)
    def _finish():
        kv_col = 3 * D_CONV + Q_DIM
        u, gb, qn, kn, v = _split_norm(z_scr[:, 0:kv_col], z_scr[:, kv_col:], segq_ref, segk_ref, qg_ref[...],
                                       kg_ref[...])
        rows = lambda a, t: a[t * n_seq:(t + 1) * n_seq, :]
        ue = [st_ref[:, i, :] for i in range(CONV_W - 1)] + [rows(u, t) for t in range(n_t)]
        for t in range(n_t):
            y = ue[t] * cw_ref[0:1, :] + ue[t + 1] * cw_ref[1:2, :] + ue[t + 2] * cw_ref[2:3, :]
            ya_ref[t * n_seq:(t + 1) * n_seq, :] = rows(gb, t) * y
            k_ref[:, t, :] = rows(kn, t)
            v_ref[:, t, :] = rows(v, t)
            for h in range(N_HEADS):
                q_ref[:, (h // GROUP) * n_t * GROUP + t * GROUP + h % GROUP, :] = (
                    rows(qn, t)[:, h * HEAD_DIM:(h + 1) * HEAD_DIM])
        for i in range(CONV_W - 1):
            conv_ref[:, i, :] = ue[n_t + i]


def _s0_attn_kernel(q_ref, ckt_ref, cvt_ref, kn_ref, vn_ref, sk_ref, o_ref, kst_ref, vst_ref, kn_scr, vn_scr):
    bb, n_t = kn_ref.shape[0], kn_ref.shape[1]
    wb = ckt_ref.shape[3]
    rows = n_t * GROUP
    ext = []
    for old_ref, new_ref, scr, out_ref in ((ckt_ref, kn_ref, kn_scr, kst_ref), (cvt_ref, vn_ref, vn_scr, vst_ref)):
        scr[...] = jnp.zeros(scr.shape, F32)
        scr[:, 0:n_t, :] = new_ref[...]
        new_t = jnp.swapaxes(scr[...], 1, 2).reshape(bb, N_KV, HEAD_DIM, SAMPLE_NEW)
        ext.append(jnp.concatenate([old_ref[...], new_t], axis=-1))
        out_ref[...] = ext[-1][:, :, :, n_t:n_t + wb]
    t = lax.broadcasted_iota(jnp.int32, (rows, wb + SAMPLE_NEW), 0) // GROUP
    c = lax.broadcasted_iota(jnp.int32, (rows, wb + SAMPLE_NEW), 1)
    bias = jnp.where((c <= wb + t) & (c >= wb + t - WINDOW), 0.0, NEG)[None]
    for j in range(N_KV):
        q = q_ref[:, j * rows:(j + 1) * rows, :].astype(BF16)
        sk = sk_ref[j][None]
        s = jnp.einsum('bqd,bdk->bqk', q, ext[0][:, j].astype(BF16), preferred_element_type=F32) + bias
        m = jnp.maximum(jnp.max(s, axis=-1, keepdims=True), sk)
        p = jnp.exp(s - m)
        den = jnp.sum(p, axis=-1, keepdims=True) + jnp.exp(sk - m)
        o = jnp.einsum('bqk,bdk->bqd', p.astype(BF16), ext[1][:, j].astype(BF16), preferred_element_type=F32)
        o_ref[:, j * rows:(j + 1) * rows, :] = o / den


def _mlp_stream_step(c, wup_ref, wdn_ref, y_ref, wup_bf_ref, wdn_bf_ref, x1_scr, xn_scr):
    wup_bf_ref[...] = wup_ref[...].astype(BF16)
    wdn_bf_ref[...] = wdn_ref[...].astype(BF16)
    a = jnp.square(jnp.maximum(_dot(xn_scr[...], wup_bf_ref[...]), 0.0)).astype(BF16)
    x1_scr[...] += _dot(a, wdn_bf_ref[...])
    if y_ref is not None:
        @pl.when(c == pl.num_programs(0) - 1)
        def _done():
            y_ref[...] = x1_scr[...]


def _s0_post_kernel(x_ref, ya_ref, o_ref, wout_ref, gmlp_ref, wup_ref, wdn_ref,
                    y_ref, wout_bf_ref, wup_bf_ref, wdn_bf_ref, x1_scr, xn_scr):
    c = pl.program_id(0)

    @pl.when(c == 0)
    def _mix():
        n_t = x_ref.shape[1]
        wout_bf_ref[...] = wout_ref[...].astype(BF16)
        yb = jnp.concatenate(
            [jnp.concatenate([o_ref[:, (h // GROUP) * n_t * GROUP + t * GROUP + h % GROUP, :]
                              for h in range(N_HEADS)], axis=-1) for t in range(n_t)], axis=0)
        mix = jnp.concatenate([ya_ref[...], yb], axis=-1).astype(BF16)
        x1 = _time_major(x_ref) + _dot(mix, wout_bf_ref[...])
        x1_scr[...] = x1
        xn_scr[...] = _rms(x1, gmlp_ref[0:1, :]).astype(BF16)

    _mlp_stream_step(c, wup_ref, wdn_ref, y_ref, wup_bf_ref, wdn_bf_ref, x1_scr, xn_scr)


def _s1_kernel(x_ref, buf_ref, gmix_ref, wpool_ref, pscale_ref, gmlp_ref, wup_ref, wdn_ref,
               y_ref, pool_ref, wpool_bf_ref, wup_bf_ref, wdn_bf_ref, x1_scr, xn_scr):
    c = pl.program_id(0)
    n_buf, n_seq, _ = buf_ref.shape
    n_t = x_ref.shape[0] // n_seq

    @pl.when(c == 0)
    def _mix():
        wpool_bf_ref[...] = wpool_ref[...].astype(BF16)
        x = x_ref[...]
        h = _rms(x, gmix_ref[1:2, :])
        he = [buf_ref[i] for i in range(n_buf)] + [h[t * n_seq:(t + 1) * n_seq, :] for t in range(n_t)]
        for i in range(n_buf):
            pool_ref[i] = he[n_t + i]
        ys = []
        for gi, w in enumerate(POOL_WINDOWS):
            cols = slice(gi * POOL_GROUP_DIM, (gi + 1) * POOL_GROUP_DIM)
            dl = []
            for t in range(n_t):
                acc = he[n_buf + t][:, cols]
                for j in range(1, w):
                    acc = acc + he[n_buf + t - j][:, cols]
                dl.append(acc * (1.0 / w) - he[n_buf + t][:, cols])
            ys.append(_dot(jnp.concatenate(dl, axis=0).astype(BF16), wpool_bf_ref[gi]))
        x1 = x + jnp.concatenate(ys, axis=-1) * pscale_ref[...]
        x1_scr[...] = x1
        xn_scr[...] = _rms(x1, gmlp_ref[1:2, :]).astype(BF16)

    _mlp_stream_step(c, wup_ref, wdn_ref, None, wup_bf_ref, wdn_bf_ref, x1_scr, xn_scr)

    @pl.when(c == pl.num_programs(0) - 1)
    def _done():
        for t in range(n_t):
            y_ref[:, t, :] = x1_scr[t * n_seq:(t + 1) * n_seq, :]


class _Stacked(NamedTuple):
    array: jax.Array
    layer: int


def _operand(arg):
    return arg.array if isinstance(arg, _Stacked) else arg


def _resident(arg, n_grid):
    if isinstance(arg, _Stacked):
        shape, idx = (None,) + arg.array.shape[1:], (arg.layer,) + (0,) * (arg.array.ndim - 1)
    else:
        shape, idx = arg.shape, (0,) * arg.ndim
    index_map = {1: lambda i: idx, 2: lambda i, j: idx}[n_grid]
    return pl.BlockSpec(shape, index_map, pipeline_mode=pl.Buffered(1))


def _params(n_grid):
    return pltpu.CompilerParams(dimension_semantics=("arbitrary",) * n_grid, vmem_limit_bytes=VMEM_LIMIT)


def _prompt_specs(n_tiles):
    tile_in = pl.BlockSpec((None, SEQ_TILE, D_MODEL), lambda b, s: (b, jnp.minimum(s, n_tiles - 1), 0))
    tile_out = pl.BlockSpec((None, SEQ_TILE, D_MODEL), lambda b, s: (b, jnp.maximum(s - 1, 0), 0))
    per_b = lambda r, c: pl.BlockSpec((None, r, c), lambda b, s: (b, 0, 0))
    return tile_in, tile_out, per_b


def _l0_prompt(x, sinks, *resident):
    n_b, seq, _ = x.shape
    n_tiles = seq // SEQ_TILE
    tile_in, tile_out, per_b = _prompt_specs(n_tiles)
    return pl.pallas_call(
        _l0_prompt_kernel,
        grid=(n_b, n_tiles + 1),
        in_specs=[pl.BlockSpec(memory_space=pltpu.SMEM), tile_in] + [_resident(a, 2) for a in resident],
        out_specs=[tile_out, per_b(BLOCK, D_MODEL), per_b(CONV_W - 1, D_CONV), per_b(WINDOW, KV_DIM),
                   per_b(WINDOW, KV_DIM)],
        out_shape=[jax.ShapeDtypeStruct((n_b, seq, D_MODEL), F32),
                   jax.ShapeDtypeStruct((n_b, BLOCK, D_MODEL), F32),
                   jax.ShapeDtypeStruct((n_b, CONV_W - 1, D_CONV), F32),
                   jax.ShapeDtypeStruct((n_b, WINDOW, KV_DIM), F32),
                   jax.ShapeDtypeStruct((n_b, WINDOW, KV_DIM), F32)],
        scratch_shapes=[pltpu.VMEM((CONV_HDR + SEQ_TILE, D_CONV), F32),
                        pltpu.VMEM((BLOCK + SEQ_TILE, KV_DIM), F32),
                        pltpu.VMEM((BLOCK + SEQ_TILE, KV_DIM), F32),
                        pltpu.VMEM((2 * BLOCK, GROUP * BLOCK), F32),
                        pltpu.VMEM((2 * BLOCK, GROUP * BLOCK), F32),
                        pltpu.VMEM((2, SEQ_TILE, D_MODEL), F32)],
        compiler_params=_params(2),
        name="l0_prompt",
    )(sinks, x, *[_operand(a) for a in resident])


def _l1_prompt(x, xmeta, *resident):
    n_b, seq, _ = x.shape
    n_tiles = seq // SEQ_TILE
    tile_in, tile_out, per_b = _prompt_specs(n_tiles)
    return pl.pallas_call(
        _l1_prompt_kernel,
        grid=(n_b, n_tiles + 1),
        in_specs=[tile_in, per_b(BLOCK, D_MODEL)] + [_resident(a, 2) for a in resident],
        out_specs=[tile_out, per_b(POOL_MAX - 1, D_MODEL)],
        out_shape=[jax.ShapeDtypeStruct((n_b, seq, D_MODEL), F32),
                   jax.ShapeDtypeStruct((n_b, POOL_MAX - 1, D_MODEL), F32)],
        scratch_shapes=[pltpu.VMEM((POOL_MAX + SEQ_TILE, D_MODEL), F32),
                        pltpu.VMEM((2, SEQ_TILE, D_MODEL), F32)],
        compiler_params=_params(2),
        name="l1_prompt",
    )(x, xmeta, *[_operand(a) for a in resident])


def _s0_pre(x, outs, consts_a, w_in, consts_b):
    n_rows = outs[0].shape[0]
    const = lambda shape: pl.BlockSpec(shape, lambda c, n=len(shape): (0,) * n)
    return pl.pallas_call(
        _s0_pre_kernel,
        grid=(D_IN_EVEN // PRE_CHUNK,),
        in_specs=[_resident(a, 1) for a in (x,) + tuple(consts_a)]
        + [pl.BlockSpec((None, D_MODEL, PRE_CHUNK), lambda c: (0, 0, c))]
        + [_resident(a, 1) for a in consts_b],
        out_specs=[const(o.shape) for o in outs] + [pl.BlockSpec((D_MODEL, PRE_CHUNK), lambda c: (0, c))],
        out_shape=list(outs) + [jax.ShapeDtypeStruct((D_MODEL, D_IN_EVEN), BF16)],
        scratch_shapes=[pltpu.VMEM((n_rows, D_MODEL), BF16), pltpu.VMEM((n_rows, D_IN_EVEN), F32)],
        compiler_params=_params(1),
        name="s0_pre",
    )(x, *[_operand(a) for a in consts_a], w_in, *[_operand(a) for a in consts_b])


def _streamed_mlp(kernel, name, n_rows, outs, consts, w_up, w_down, layer):
    const = lambda shape: pl.BlockSpec(shape, lambda c, n=len(shape): (0,) * n)
    return pl.pallas_call(
        kernel,
        grid=(D_FF // FF_STREAM,),
        in_specs=[_resident(a, 1) for a in consts]
        + [pl.BlockSpec((None, D_MODEL, FF_STREAM), lambda c: (layer, 0, c)),
           pl.BlockSpec((None, FF_STREAM, D_MODEL), lambda c: (layer, c, 0))],
        out_specs=[const(o.shape) for o in outs]
        + [pl.BlockSpec((D_MODEL, FF_STREAM), lambda c: (0, c)),
           pl.BlockSpec((FF_STREAM, D_MODEL), lambda c: (c, 0))],
        out_shape=list(outs) + [jax.ShapeDtypeStruct((D_MODEL, D_FF), BF16),
                                jax.ShapeDtypeStruct((D_FF, D_MODEL), BF16)],
        scratch_shapes=[pltpu.VMEM((n_rows, D_MODEL), F32), pltpu.VMEM((n_rows, D_MODEL), BF16)],
        compiler_params=_params(1),
        name=name,
    )(*[_operand(a) for a in consts], w_up, w_down)


def _s0_attn(q4, ckt, cvt, kn, vn, sk_rows):
    n_seq, _, _, wb = ckt.shape
    n_t = kn.shape[1]
    rows = n_t * GROUP
    blk = lambda *tail: pl.BlockSpec((SAMPLE_BB,) + tail, lambda i: (i,) + (0,) * len(tail))
    return pl.pallas_call(
        _s0_attn_kernel,
        grid=(n_seq // SAMPLE_BB,),
        in_specs=[blk(N_KV * rows, HEAD_DIM), blk(N_KV, HEAD_DIM, wb), blk(N_KV, HEAD_DIM, wb),
                  blk(n_t, KV_DIM), blk(n_t, KV_DIM), pl.BlockSpec((N_KV, rows, 1), lambda i: (0, 0, 0))],
        out_specs=[blk(N_KV * rows, HEAD_DIM), blk(N_KV, HEAD_DIM, wb), blk(N_KV, HEAD_DIM, wb)],
        out_shape=[jax.ShapeDtypeStruct((n_seq, N_KV * rows, HEAD_DIM), F32),
                   jax.ShapeDtypeStruct(ckt.shape, F32), jax.ShapeDtypeStruct(cvt.shape, F32)],
        scratch_shapes=[pltpu.VMEM((SAMPLE_BB, SAMPLE_NEW, KV_DIM), F32),
                        pltpu.VMEM((SAMPLE_BB, SAMPLE_NEW, KV_DIM), F32)],
        compiler_params=_params(1),
        name="s0_attn",
    )(q4, ckt, cvt, kn, vn, sk_rows)


def kernel(x_prompt, x_sample, state_conv, cache_k_win, cache_v_win, state_pool, meta_tokens, norm_mix, norm_mlp, w_in_even, conv_w, q_norm, k_norm, attn_sinks, w_out_even, w_pool, pool_scale, w_up, w_down):
    n_seq, n_t, _ = x_sample.shape
    wb = cache_k_win.shape[2]
    assert x_prompt.shape[1] % SEQ_TILE == 0 and x_prompt.shape[1] >= 2 * SEQ_TILE and n_seq % SAMPLE_BB == 0
    assert wb == WINDOW and n_t <= SAMPLE_NEW

    gmix = [norm_mix, norm_mix]
    gmlp = [norm_mlp, norm_mlp]
    qg = jnp.tile(q_norm[0], N_HEADS)[None, :]
    kg = jnp.tile(k_norm[0], N_KV)[None, :]
    seg = np.kron(np.eye(N_HEADS, dtype=np.float32), np.full((HEAD_DIM, HEAD_DIM), 1.0 / HEAD_DIM, np.float32))
    segq, segk = jnp.asarray(seg, BF16), jnp.asarray(seg[:KV_DIM, :KV_DIM], BF16)
    cw = _Stacked(conv_w, 0)
    sinks = attn_sinks[0]
    sds = lambda *shape, dtype=F32: jax.ShapeDtypeStruct(shape, dtype)

    n_rows = n_t * n_seq
    ya, q4, kn_b, vn_b, conv_s, win = _s0_pre(
        x_sample,
        [sds(n_rows, D_CONV), sds(n_seq, N_HEADS * n_t, HEAD_DIM), sds(n_seq, n_t, KV_DIM),
         sds(n_seq, n_t, KV_DIM), sds(n_seq, CONV_W - 1, D_CONV)],
        [_Stacked(state_conv, 0), gmix[0]], w_in_even, [segq, segk, qg, kg, cw])
    sk_rows = jnp.tile(sinks.reshape(N_KV, 1, GROUP), (1, n_t, 1)).reshape(N_KV, n_t * GROUP, 1)
    to_dev = lambda a: jnp.transpose(a.reshape(n_seq, -1, N_KV, HEAD_DIM), (0, 2, 3, 1))
    o4, kst, vst = _s0_attn(q4, to_dev(cache_k_win), to_dev(cache_v_win), kn_b, vn_b, sk_rows)
    k_s, v_s = (jnp.transpose(a, (0, 3, 1, 2))[None] for a in (kst, vst))
    xs2, wout, wup0, wdn0 = _streamed_mlp(
        _s0_post_kernel, "s0_post", n_rows, [sds(n_rows, D_MODEL), sds(D_MODEL, D_MODEL, dtype=BF16)],
        [x_sample, ya, o4, _Stacked(w_out_even, 0), gmlp[0]], w_up, w_down, 0)
    buf_t = jnp.transpose(state_pool[0], (1, 0, 2))
    y_sample, pool_s, wpool, wup1, wdn1 = _streamed_mlp(
        _s1_kernel, "s1", n_rows,
        [sds(n_seq, n_t, D_MODEL), sds(POOL_MAX - 1, n_seq, D_MODEL), sds(*w_pool.shape[1:], dtype=BF16)],
        [xs2, buf_t, gmix[1], _Stacked(w_pool, 0), pool_scale, gmlp[1]], w_up, w_down, 1)

    x2, x2_meta, conv_p, k_p, v_p = _l0_prompt(x_prompt, sinks, meta_tokens, gmix[0], win, segq, segk, qg, kg, cw,
                                               wout, gmlp[0], wup0, wdn0)
    y_prompt, pool_p = _l1_prompt(x2, x2_meta, gmix[1], wpool, pool_scale, gmlp[1], wup1, wdn1)

    kv5 = lambda a: a.reshape(1, a.shape[0], WINDOW, N_KV, HEAD_DIM)
    return (y_prompt, y_sample, conv_p[None], conv_s[None], kv5(k_p), k_s, kv5(v_p), v_s,
            pool_p[None], jnp.transpose(pool_s, (1, 0, 2))[None])
```

```python
import functools
from typing import NamedTuple

import jax
import jax.numpy as jnp
import numpy as np
from jax import lax
from jax.experimental import pallas as pl
from jax.experimental.pallas import tpu as pltpu

F32 = jnp.float32
BF16 = jnp.bfloat16

D_MODEL = 1024
D_CONV = 512
CONV_W = 3
HEAD_DIM = 64
N_HEADS = 8
N_KV = 2
GROUP = N_HEADS // N_KV
WINDOW = 128
BLOCK = 128
Q_DIM = N_HEADS * HEAD_DIM
KV_DIM = N_KV * HEAD_DIM
POOL_WINDOWS = (2, 4, 8, 16)
POOL_GROUP_DIM = D_MODEL // len(POOL_WINDOWS)
POOL_MAX = 16
D_FF = 4 * D_MODEL
D_IN_EVEN = 3 * D_CONV + Q_DIM + 2 * KV_DIM
N_META = 16
EPS = 1e-6
NEG = -1e30

META_PAD = BLOCK - N_META
SEQ_TILE = 512
FF_CHUNK = 1024
FF_STREAM = 512
CONV_HDR = 8
SAMPLE_BB = 16
SAMPLE_NEW = 8
VMEM_LIMIT = 56 * 1024 * 1024
MLP_YIELDS = 2 * (D_FF // FF_CHUNK)
L0_MIX_YIELDS = 1 + 2 * N_KV * (SEQ_TILE // BLOCK)


def _dot(a, b):
    return jnp.dot(a, b, preferred_element_type=F32)


def _dot_nt(a, b):
    return lax.dot_general(a, b, (((1,), (1,)), ((), ())), preferred_element_type=F32)


def _dot_tn(a, b):
    return lax.dot_general(a, b, (((0,), (0,)), ((), ())), preferred_element_type=F32)


def _rms(x, g):
    ms = jnp.mean(x * x, axis=-1, keepdims=True)
    return x * lax.rsqrt(ms + EPS) * g


def _head_rms(x, seg, g):
    ms = _dot((x * x).astype(BF16), seg)
    return x * lax.rsqrt(ms + EPS) * g


def _interleave(*steppers, shares=None):
    shares = shares or [1] * len(steppers)
    results = [None] * len(steppers)
    done = [0] * len(steppers)
    live = set(range(len(steppers)))
    while live:
        i = min(live, key=lambda k: ((done[k] + 1) / shares[k], k))
        try:
            next(steppers[i])
            done[i] += 1
        except StopIteration as finished:
            results[i] = finished.value
            live.discard(i)
    return results


def _mlp_steps(x, g, wup_ref, wdn_ref):
    xn = _rms(x, g).astype(BF16)
    acc = x
    for c in range(D_FF // FF_CHUNK):
        cols = slice(c * FF_CHUNK, (c + 1) * FF_CHUNK)
        h = _dot(xn, wup_ref[:, cols])
        a = jnp.square(jnp.maximum(h, 0.0)).astype(BF16)
        yield
        acc = acc + _dot(a, wdn_ref[cols, :])
        yield
    return acc


def _in_proj(x, g, win_ref, segq_ref, segk_ref, qg, kg):
    hn = _rms(x, g).astype(BF16)
    kv_col = 3 * D_CONV + Q_DIM
    half = x.shape[0] // 2
    z = _dot(hn, win_ref[:, 0:kv_col])
    kv = jnp.concatenate([_dot(hn[0:half], win_ref[:, kv_col:]), _dot(hn[half:], win_ref[:, kv_col:])], axis=0)
    xa = z[:, 0:D_CONV]
    gc = z[:, D_CONV:2 * D_CONV]
    gb = z[:, 2 * D_CONV:3 * D_CONV]
    q = z[:, 3 * D_CONV:]
    k = kv[:, 0:KV_DIM]
    v = kv[:, KV_DIM:]
    qn = _head_rms(q, segq_ref[...], qg) * (HEAD_DIM ** -0.5)
    kn = _head_rms(k, segk_ref[...], kg)
    return gc * xa, gb, qn, kn, v


def _l0_mixer(x, rows, sinks_ref, gmix, win_ref, segq_ref, segk_ref, qg, kg, cw_ref, wout_ref,
              u_scr, k_scr, v_scr, first_scr, band_scr):
    u, gb, qn, kn, v = _in_proj(x, gmix, win_ref, segq_ref, segk_ref, qg, kg)

    u_scr[CONV_HDR:CONV_HDR + rows, :] = u
    u1 = u_scr[CONV_HDR - 1:CONV_HDR - 1 + rows, :]
    u2 = u_scr[CONV_HDR - 2:CONV_HDR - 2 + rows, :]
    ya = gb * (u2 * cw_ref[0:1, :] + u1 * cw_ref[1:2, :] + u * cw_ref[2:3, :])
    k_scr[BLOCK:BLOCK + rows, :] = kn
    v_scr[BLOCK:BLOCK + rows, :] = v
    yield

    yb_blocks = []
    for i in range(rows // BLOCK):
        qb = qn[i * BLOCK:(i + 1) * BLOCK, :]
        bias = first_scr[...] if i == 0 else band_scr[...]
        heads_t = []
        for j in range(N_KV):
            lanes = slice(j * HEAD_DIM, (j + 1) * HEAD_DIM)
            k2 = k_scr[i * BLOCK:(i + 2) * BLOCK, lanes].astype(BF16)
            v2 = v_scr[i * BLOCK:(i + 2) * BLOCK, lanes].astype(BF16)
            qg_t = qb[:, j * GROUP * HEAD_DIM:(j + 1) * GROUP * HEAD_DIM].T
            qs_t = jnp.concatenate(
                [qg_t[g * HEAD_DIM:(g + 1) * HEAD_DIM, :] for g in range(GROUP)], axis=1).astype(BF16)
            sk = jnp.concatenate(
                [jnp.full((1, BLOCK), sinks_ref[j * GROUP + g], F32) for g in range(GROUP)], axis=1)
            st = _dot(k2, qs_t) + bias
            m = jnp.maximum(jnp.max(st, axis=0, keepdims=True), sk)
            p = jnp.exp(st - m)
            den = jnp.sum(p, axis=0, keepdims=True) + jnp.exp(sk - m)
            yield
            ot = _dot_tn(v2, p.astype(BF16)) / den
            heads_t += [ot[:, g * BLOCK:(g + 1) * BLOCK] for g in range(GROUP)]
            yield
        yb_blocks.append(jnp.concatenate(heads_t, axis=0).T)
    yb = yb_blocks[0] if len(yb_blocks) == 1 else jnp.concatenate(yb_blocks, axis=0)

    u_scr[CONV_HDR - 2:CONV_HDR, :] = u_scr[CONV_HDR - 2 + rows:CONV_HDR + rows, :]
    k_scr[0:BLOCK, :] = k_scr[rows:rows + BLOCK, :]
    v_scr[0:BLOCK, :] = v_scr[rows:rows + BLOCK, :]

    mix = jnp.concatenate([ya, yb], axis=-1).astype(BF16)
    return x + _dot(mix, wout_ref[...])


def _pipeline_steps(s, x_ref, y_ref, x1_scr, mixer, mlp):
    slot = lax.rem(s, 2)

    def mix_tile():
        x1_scr[slot] = yield from mixer(x_ref[...], SEQ_TILE)

    def mlp_tile():
        y_ref[...] = yield from mlp(x1_scr[1 - slot])

    return mlp_tile, mix_tile


def _l0_prompt_kernel(sinks_ref, x_ref, meta_ref, gmix_ref, win_ref, segq_ref, segk_ref, qg_ref, kg_ref,
                      cw_ref, wout_ref, gmlp_ref, wup_ref, wdn_ref,
                      y_ref, ymeta_ref, conv_ref, kout_ref, vout_ref,
                      u_scr, k_scr, v_scr, first_scr, band_scr, x1_scr):
    s = pl.program_id(1)
    last = pl.num_programs(1) - 1
    mixer = functools.partial(
        _l0_mixer, sinks_ref=sinks_ref, gmix=gmix_ref[0:1, :], win_ref=win_ref, segq_ref=segq_ref,
        segk_ref=segk_ref, qg=qg_ref[...], kg=kg_ref[...], cw_ref=cw_ref, wout_ref=wout_ref,
        u_scr=u_scr, k_scr=k_scr, v_scr=v_scr, first_scr=first_scr, band_scr=band_scr)
    mlp = functools.partial(_mlp_steps, g=gmlp_ref[0:1, :], wup_ref=wup_ref, wdn_ref=wdn_ref)
    mlp_tile, mix_tile = _pipeline_steps(s, x_ref, y_ref, x1_scr, mixer, mlp)

    def band_bias(first_valid_key):
        c = lax.broadcasted_iota(jnp.int32, (2 * BLOCK, GROUP * BLOCK), 0)
        q = lax.broadcasted_iota(jnp.int32, (2 * BLOCK, GROUP * BLOCK), 1) & (BLOCK - 1)
        return jnp.where((c >= q) & (c <= q + WINDOW) & (c >= first_valid_key), 0.0, NEG)

    @pl.when(s == 0)
    def _start():
        band_scr[...] = band_bias(0)
        u_scr[0:CONV_HDR, :] = jnp.zeros((CONV_HDR, D_CONV), F32)
        k_scr[0:BLOCK, :] = jnp.zeros((BLOCK, KV_DIM), F32)
        v_scr[0:BLOCK, :] = jnp.zeros((BLOCK, KV_DIM), F32)
        first_scr[...] = band_bias(BLOCK + META_PAD)
        x_meta = jnp.concatenate([jnp.zeros((META_PAD, D_MODEL), F32), meta_ref[...]], axis=0)
        x1_meta, = _interleave(mixer(x_meta, BLOCK))
        first_scr[...] = band_bias(META_PAD)
        ymeta_ref[...], _ = _interleave(mlp(x1_meta), mix_tile(), shares=(MLP_YIELDS, L0_MIX_YIELDS))
        first_scr[...] = band_scr[...]

    @pl.when((s > 0) & (s < last))
    def _steady():
        _interleave(mlp_tile(), mix_tile(), shares=(MLP_YIELDS, L0_MIX_YIELDS))

    @pl.when(s == last)
    def _drain():
        _interleave(mlp_tile())

    @pl.when(s == last - 1)
    def _state():
        conv_ref[...] = u_scr[CONV_HDR - 2:CONV_HDR, :]
        kout_ref[...] = k_scr[0:BLOCK, :]
        vout_ref[...] = v_scr[0:BLOCK, :]


def _l1_mixer(x, rows, is_meta, gmix, wpool_ref, pscale, h_scr):
    h = _rms(x, gmix)
    h_scr[POOL_MAX:POOL_MAX + rows, :] = h
    ys = []
    for gi, w in enumerate(POOL_WINDOWS):
        cols = slice(gi * POOL_GROUP_DIM, (gi + 1) * POOL_GROUP_DIM)
        hg = h[:, cols]
        acc = hg
        for j in range(1, w):
            acc = acc + h_scr[POOL_MAX - j:POOL_MAX - j + rows, cols]
        if is_meta:
            r = lax.broadcasted_iota(jnp.int32, (rows, 1), 0)
            pooled = acc / jnp.clip(r - (META_PAD - 1), 1, w).astype(F32)
        else:
            pooled = acc * (1.0 / w)
        ys.append(_dot((pooled - hg).astype(BF16), wpool_ref[gi]))
        yield
    h_scr[0:POOL_MAX, :] = h_scr[rows:rows + POOL_MAX, :]
    return x + jnp.concatenate(ys, axis=-1) * pscale


def _l1_prompt_kernel(x_ref, meta_ref, gmix_ref, wpool_ref, pscale_ref, gmlp_ref, wup_ref, wdn_ref,
                      y_ref, pool_ref, h_scr, x1_scr):
    s = pl.program_id(1)
    last = pl.num_programs(1) - 1
    mixer = functools.partial(_l1_mixer, gmix=gmix_ref[1:2, :], wpool_ref=wpool_ref, pscale=pscale_ref[...],
                              h_scr=h_scr)
    mlp = functools.partial(_mlp_steps, g=gmlp_ref[1:2, :], wup_ref=wup_ref, wdn_ref=wdn_ref)
    mlp_tile, mix_tile = _pipeline_steps(s, x_ref, y_ref, x1_scr,
                                         functools.partial(mixer, is_meta=False), mlp)

    @pl.when(s == 0)
    def _start():
        h_scr[0:POOL_MAX, :] = jnp.zeros((POOL_MAX, D_MODEL), F32)
        _interleave(mixer(meta_ref[...], BLOCK, True))
        _interleave(mix_tile())

    @pl.when((s > 0) & (s < last))
    def _steady():
        _interleave(mlp_tile(), mix_tile(), shares=(MLP_YIELDS, len(POOL_WINDOWS)))

    @pl.when(s == last)
    def _drain():
        _interleave(mlp_tile())

    @pl.when(s == last - 1)
    def _state():
        pool_ref[...] = h_scr[1:POOL_MAX, :]


def _time_major(ref):
    return jnp.concatenate([ref[:, t, :] for t in range(ref.shape[1])], axis=0)


def _s0_pre_kernel(x_ref, st_ref, gmix_ref, win_ref, segq_ref, segk_ref, qg_ref, kg_ref, cw_ref,
                   ya_ref, q_ref, k_ref, v_ref, conv_ref, win_bf_ref):
    n_seq, n_t, _ = x_ref.shape
    win_bf_ref[...] = win_ref[...].astype(BF16)
    u, gb, qn, kn, v = _in_proj(_time_major(x_ref), gmix_ref[0:1, :], win_bf_ref, segq_ref, segk_ref, qg_ref[...],
                                kg_ref[...])
    rows = lambda a, t: a[t * n_seq:(t + 1) * n_seq, :]
    ue = [st_ref[:, i, :] for i in range(CONV_W - 1)] + [rows(u, t) for t in range(n_t)]
    for t in range(n_t):
        y = ue[t] * cw_ref[0:1, :] + ue[t + 1] * cw_ref[1:2, :] + ue[t + 2] * cw_ref[2:3, :]
        ya_ref[t * n_seq:(t + 1) * n_seq, :] = rows(gb, t) * y
        k_ref[:, t, :] = rows(kn, t)
        v_ref[:, t, :] = rows(v, t)
        for h in range(N_HEADS):
            q_ref[:, (h // GROUP) * n_t * GROUP + t * GROUP + h % GROUP, :] = (
                rows(qn, t)[:, h * HEAD_DIM:(h + 1) * HEAD_DIM])
    for i in range(CONV_W - 1):
        conv_ref[:, i, :] = ue[n_t + i]


def _s0_attn_kernel(q_ref, ckt_ref, cvt_ref, kn_ref, vn_ref, sk_ref, o_ref, kst_ref, vst_ref, kn_scr, vn_scr):
    bb, n_t = kn_ref.shape[0], kn_ref.shape[1]
    wb = ckt_ref.shape[3]
    rows = n_t * GROUP
    ext = []
    for old_ref, new_ref, scr, out_ref in ((ckt_ref, kn_ref, kn_scr, kst_ref), (cvt_ref, vn_ref, vn_scr, vst_ref)):
        scr[...] = jnp.zeros(scr.shape, F32)
        scr[:, 0:n_t, :] = new_ref[...]
        new_t = jnp.swapaxes(scr[...], 1, 2).reshape(bb, N_KV, HEAD_DIM, SAMPLE_NEW)
        ext.append(jnp.concatenate([old_ref[...], new_t], axis=-1))
        out_ref[...] = ext[-1][:, :, :, n_t:n_t + wb]
    t = lax.broadcasted_iota(jnp.int32, (rows, wb + SAMPLE_NEW), 0) // GROUP
    c = lax.broadcasted_iota(jnp.int32, (rows, wb + SAMPLE_NEW), 1)
    bias = jnp.where((c <= wb + t) & (c >= wb + t - WINDOW), 0.0, NEG)[None]
    for j in range(N_KV):
        q = q_ref[:, j * rows:(j + 1) * rows, :].astype(BF16)
        sk = sk_ref[j][None]
        s = jnp.einsum('bqd,bdk->bqk', q, ext[0][:, j].astype(BF16), preferred_element_type=F32) + bias
        m = jnp.maximum(jnp.max(s, axis=-1, keepdims=True), sk)
        p = jnp.exp(s - m)
        den = jnp.sum(p, axis=-1, keepdims=True) + jnp.exp(sk - m)
        o = jnp.einsum('bqk,bdk->bqd', p.astype(BF16), ext[1][:, j].astype(BF16), preferred_element_type=F32)
        o_ref[:, j * rows:(j + 1) * rows, :] = o / den


def _mlp_stream_step(c, wup_ref, wdn_ref, y_ref, wup_bf_ref, wdn_bf_ref, x1_scr, xn_scr):
    wup_bf_ref[...] = wup_ref[...].astype(BF16)
    wdn_bf_ref[...] = wdn_ref[...].astype(BF16)
    a = jnp.square(jnp.maximum(_dot(xn_scr[...], wup_bf_ref[...]), 0.0)).astype(BF16)
    x1_scr[...] += _dot(a, wdn_bf_ref[...])
    if y_ref is not None:
        @pl.when(c == pl.num_programs(0) - 1)
        def _done():
            y_ref[...] = x1_scr[...]


def _s0_post_kernel(x_ref, ya_ref, o_ref, wout_ref, gmlp_ref, wup_ref, wdn_ref,
                    y_ref, wout_bf_ref, wup_bf_ref, wdn_bf_ref, x1_scr, xn_scr):
    c = pl.program_id(0)

    @pl.when(c == 0)
    def _mix():
        n_t = x_ref.shape[1]
        wout_bf_ref[...] = wout_ref[...].astype(BF16)
        yb = jnp.concatenate(
            [jnp.concatenate([o_ref[:, (h // GROUP) * n_t * GROUP + t * GROUP + h % GROUP, :]
                              for h in range(N_HEADS)], axis=-1) for t in range(n_t)], axis=0)
        mix = jnp.concatenate([ya_ref[...], yb], axis=-1).astype(BF16)
        x1 = _time_major(x_ref) + _dot(mix, wout_bf_ref[...])
        x1_scr[...] = x1
        xn_scr[...] = _rms(x1, gmlp_ref[0:1, :]).astype(BF16)

    _mlp_stream_step(c, wup_ref, wdn_ref, y_ref, wup_bf_ref, wdn_bf_ref, x1_scr, xn_scr)


def _s1_kernel(x_ref, buf_ref, gmix_ref, wpool_ref, pscale_ref, gmlp_ref, wup_ref, wdn_ref,
               y_ref, pool_ref, wpool_bf_ref, wup_bf_ref, wdn_bf_ref, x1_scr, xn_scr):
    c = pl.program_id(0)
    n_buf, n_seq, _ = buf_ref.shape
    n_t = x_ref.shape[0] // n_seq

    @pl.when(c == 0)
    def _mix():
        wpool_bf_ref[...] = wpool_ref[...].astype(BF16)
        x = x_ref[...]
        h = _rms(x, gmix_ref[1:2, :])
        he = [buf_ref[i] for i in range(n_buf)] + [h[t * n_seq:(t + 1) * n_seq, :] for t in range(n_t)]
        for i in range(n_buf):
            pool_ref[i] = he[n_t + i]
        ys = []
        for gi, w in enumerate(POOL_WINDOWS):
            cols = slice(gi * POOL_GROUP_DIM, (gi + 1) * POOL_GROUP_DIM)
            dl = []
            for t in range(n_t):
                acc = he[n_buf + t][:, cols]
                for j in range(1, w):
                    acc = acc + he[n_buf + t - j][:, cols]
                dl.append(acc * (1.0 / w) - he[n_buf + t][:, cols])
            ys.append(_dot(jnp.concatenate(dl, axis=0).astype(BF16), wpool_bf_ref[gi]))
        x1 = x + jnp.concatenate(ys, axis=-1) * pscale_ref[...]
        x1_scr[...] = x1
        xn_scr[...] = _rms(x1, gmlp_ref[1:2, :]).astype(BF16)

    _mlp_stream_step(c, wup_ref, wdn_ref, None, wup_bf_ref, wdn_bf_ref, x1_scr, xn_scr)

    @pl.when(c == pl.num_programs(0) - 1)
    def _done():
        for t in range(n_t):
            y_ref[:, t, :] = x1_scr[t * n_seq:(t + 1) * n_seq, :]


class _Stacked(NamedTuple):
    array: jax.Array
    layer: int


def _operand(arg):
    return arg.array if isinstance(arg, _Stacked) else arg


def _resident(arg, n_grid):
    if isinstance(arg, _Stacked):
        shape, idx = (None,) + arg.array.shape[1:], (arg.layer,) + (0,) * (arg.array.ndim - 1)
    else:
        shape, idx = arg.shape, (0,) * arg.ndim
    index_map = {1: lambda i: idx, 2: lambda i, j: idx}[n_grid]
    return pl.BlockSpec(shape, index_map, pipeline_mode=pl.Buffered(1))


def _params(n_grid):
    return pltpu.CompilerParams(dimension_semantics=("arbitrary",) * n_grid, vmem_limit_bytes=VMEM_LIMIT)


def _prompt_specs(n_tiles):
    tile_in = pl.BlockSpec((None, SEQ_TILE, D_MODEL), lambda b, s: (b, jnp.minimum(s, n_tiles - 1), 0))
    tile_out = pl.BlockSpec((None, SEQ_TILE, D_MODEL), lambda b, s: (b, jnp.maximum(s - 1, 0), 0))
    per_b = lambda r, c: pl.BlockSpec((None, r, c), lambda b, s: (b, 0, 0))
    return tile_in, tile_out, per_b


def _l0_prompt(x, sinks, *resident):
    n_b, seq, _ = x.shape
    n_tiles = seq // SEQ_TILE
    tile_in, tile_out, per_b = _prompt_specs(n_tiles)
    return pl.pallas_call(
        _l0_prompt_kernel,
        grid=(n_b, n_tiles + 1),
        in_specs=[pl.BlockSpec(memory_space=pltpu.SMEM), tile_in] + [_resident(a, 2) for a in resident],
        out_specs=[tile_out, per_b(BLOCK, D_MODEL), per_b(CONV_W - 1, D_CONV), per_b(WINDOW, KV_DIM),
                   per_b(WINDOW, KV_DIM)],
        out_shape=[jax.ShapeDtypeStruct((n_b, seq, D_MODEL), F32),
                   jax.ShapeDtypeStruct((n_b, BLOCK, D_MODEL), F32),
                   jax.ShapeDtypeStruct((n_b, CONV_W - 1, D_CONV), F32),
                   jax.ShapeDtypeStruct((n_b, WINDOW, KV_DIM), F32),
                   jax.ShapeDtypeStruct((n_b, WINDOW, KV_DIM), F32)],
        scratch_shapes=[pltpu.VMEM((CONV_HDR + SEQ_TILE, D_CONV), F32),
                        pltpu.VMEM((BLOCK + SEQ_TILE, KV_DIM), F32),
                        pltpu.VMEM((BLOCK + SEQ_TILE, KV_DIM), F32),
                        pltpu.VMEM((2 * BLOCK, GROUP * BLOCK), F32),
                        pltpu.VMEM((2 * BLOCK, GROUP * BLOCK), F32),
                        pltpu.VMEM((2, SEQ_TILE, D_MODEL), F32)],
        compiler_params=_params(2),
        name="l0_prompt",
    )(sinks, x, *[_operand(a) for a in resident])


def _l1_prompt(x, xmeta, *resident):
    n_b, seq, _ = x.shape
    n_tiles = seq // SEQ_TILE
    tile_in, tile_out, per_b = _prompt_specs(n_tiles)
    return pl.pallas_call(
        _l1_prompt_kernel,
        grid=(n_b, n_tiles + 1),
        in_specs=[tile_in, per_b(BLOCK, D_MODEL)] + [_resident(a, 2) for a in resident],
        out_specs=[tile_out, per_b(POOL_MAX - 1, D_MODEL)],
        out_shape=[jax.ShapeDtypeStruct((n_b, seq, D_MODEL), F32),
                   jax.ShapeDtypeStruct((n_b, POOL_MAX - 1, D_MODEL), F32)],
        scratch_shapes=[pltpu.VMEM((POOL_MAX + SEQ_TILE, D_MODEL), F32),
                        pltpu.VMEM((2, SEQ_TILE, D_MODEL), F32)],
        compiler_params=_params(2),
        name="l1_prompt",
    )(x, xmeta, *[_operand(a) for a in resident])


def _single_step(kernel, name, out_shape, *args):
    return pl.pallas_call(
        kernel,
        grid=(1,),
        in_specs=[_resident(a, 1) for a in args],
        out_specs=[pl.BlockSpec(o.shape, lambda i, n=len(o.shape): (0,) * n) for o in out_shape],
        out_shape=out_shape,
        compiler_params=_params(1),
        name=name,
    )(*[_operand(a) for a in args])


def _streamed_mlp(kernel, name, n_rows, outs, consts, w_up, w_down, layer):
    const = lambda shape: pl.BlockSpec(shape, lambda c, n=len(shape): (0,) * n)
    return pl.pallas_call(
        kernel,
        grid=(D_FF // FF_STREAM,),
        in_specs=[_resident(a, 1) for a in consts]
        + [pl.BlockSpec((None, D_MODEL, FF_STREAM), lambda c: (layer, 0, c)),
           pl.BlockSpec((None, FF_STREAM, D_MODEL), lambda c: (layer, c, 0))],
        out_specs=[const(o.shape) for o in outs]
        + [pl.BlockSpec((D_MODEL, FF_STREAM), lambda c: (0, c)),
           pl.BlockSpec((FF_STREAM, D_MODEL), lambda c: (c, 0))],
        out_shape=list(outs) + [jax.ShapeDtypeStruct((D_MODEL, D_FF), BF16),
                                jax.ShapeDtypeStruct((D_FF, D_MODEL), BF16)],
        scratch_shapes=[pltpu.VMEM((n_rows, D_MODEL), F32), pltpu.VMEM((n_rows, D_MODEL), BF16)],
        compiler_params=_params(1),
        name=name,
    )(*[_operand(a) for a in consts], w_up, w_down)


def _s0_attn(q4, ckt, cvt, kn, vn, sk_rows):
    n_seq, _, _, wb = ckt.shape
    n_t = kn.shape[1]
    rows = n_t * GROUP
    blk = lambda *tail: pl.BlockSpec((SAMPLE_BB,) + tail, lambda i: (i,) + (0,) * len(tail))
    return pl.pallas_call(
        _s0_attn_kernel,
        grid=(n_seq // SAMPLE_BB,),
        in_specs=[blk(N_KV * rows, HEAD_DIM), blk(N_KV, HEAD_DIM, wb), blk(N_KV, HEAD_DIM, wb),
                  blk(n_t, KV_DIM), blk(n_t, KV_DIM), pl.BlockSpec((N_KV, rows, 1), lambda i: (0, 0, 0))],
        out_specs=[blk(N_KV * rows, HEAD_DIM), blk(N_KV, HEAD_DIM, wb), blk(N_KV, HEAD_DIM, wb)],
        out_shape=[jax.ShapeDtypeStruct((n_seq, N_KV * rows, HEAD_DIM), F32),
                   jax.ShapeDtypeStruct(ckt.shape, F32), jax.ShapeDtypeStruct(cvt.shape, F32)],
        scratch_shapes=[pltpu.VMEM((SAMPLE_BB, SAMPLE_NEW, KV_DIM), F32),
                        pltpu.VMEM((SAMPLE_BB, SAMPLE_NEW, KV_DIM), F32)],
        compiler_params=_params(1),
        name="s0_attn",
    )(q4, ckt, cvt, kn, vn, sk_rows)


def kernel(x_prompt, x_sample, state_conv, cache_k_win, cache_v_win, state_pool, meta_tokens, norm_mix, norm_mlp, w_in_even, conv_w, q_norm, k_norm, attn_sinks, w_out_even, w_pool, pool_scale, w_up, w_down):
    n_seq, n_t, _ = x_sample.shape
    wb = cache_k_win.shape[2]
    assert x_prompt.shape[1] % SEQ_TILE == 0 and x_prompt.shape[1] >= 2 * SEQ_TILE and n_seq % SAMPLE_BB == 0
    assert wb == WINDOW and n_t <= SAMPLE_NEW

    gmix = [norm_mix, norm_mix]
    gmlp = [norm_mlp, norm_mlp]
    qg = jnp.tile(q_norm[0], N_HEADS)[None, :]
    kg = jnp.tile(k_norm[0], N_KV)[None, :]
    seg = np.kron(np.eye(N_HEADS, dtype=np.float32), np.full((HEAD_DIM, HEAD_DIM), 1.0 / HEAD_DIM, np.float32))
    segq, segk = jnp.asarray(seg, BF16), jnp.asarray(seg[:KV_DIM, :KV_DIM], BF16)
    cw = _Stacked(conv_w, 0)
    sinks = attn_sinks[0]
    sds = lambda *shape, dtype=F32: jax.ShapeDtypeStruct(shape, dtype)

    n_rows = n_t * n_seq
    ya, q4, kn_b, vn_b, conv_s, win = _single_step(
        _s0_pre_kernel, "s0_pre",
        [sds(n_rows, D_CONV), sds(n_seq, N_HEADS * n_t, HEAD_DIM), sds(n_seq, n_t, KV_DIM),
         sds(n_seq, n_t, KV_DIM), sds(n_seq, CONV_W - 1, D_CONV), sds(D_MODEL, D_IN_EVEN, dtype=BF16)],
        x_sample, _Stacked(state_conv, 0), gmix[0], _Stacked(w_in_even, 0), segq, segk, qg, kg, cw)
    sk_rows = jnp.tile(sinks.reshape(N_KV, 1, GROUP), (1, n_t, 1)).reshape(N_KV, n_t * GROUP, 1)
    to_dev = lambda a: jnp.transpose(a.reshape(n_seq, -1, N_KV, HEAD_DIM), (0, 2, 3, 1))
    o4, kst, vst = _s0_attn(q4, to_dev(cache_k_win), to_dev(cache_v_win), kn_b, vn_b, sk_rows)
    k_s, v_s = (jnp.transpose(a, (0, 3, 1, 2))[None] for a in (kst, vst))
    xs2, wout, wup0, wdn0 = _streamed_mlp(
        _s0_post_kernel, "s0_post", n_rows, [sds(n_rows, D_MODEL), sds(D_MODEL, D_MODEL, dtype=BF16)],
        [x_sample, ya, o4, _Stacked(w_out_even, 0), gmlp[0]], w_up, w_down, 0)
    buf_t = jnp.transpose(state_pool[0], (1, 0, 2))
    y_sample, pool_s, wpool, wup1, wdn1 = _streamed_mlp(
        _s1_kernel, "s1", n_rows,
        [sds(n_seq, n_t, D_MODEL), sds(POOL_MAX - 1, n_seq, D_MODEL), sds(*w_pool.shape[1:], dtype=BF16)],
        [xs2, buf_t, gmix[1], _Stacked(w_pool, 0), pool_scale, gmlp[1]], w_up, w_down, 1)

    x2, x2_meta, conv_p, k_p, v_p = _l0_prompt(x_prompt, sinks, meta_tokens, gmix[0], win, segq, segk, qg, kg, cw,
                                               wout, gmlp[0], wup0, wdn0)
    y_prompt, pool_p = _l1_prompt(x2, x2_meta, gmix[1], wpool, pool_scale, gmlp[1], wup1, wdn1)

    kv5 = lambda a: a.reshape(1, a.shape[0], WINDOW, N_KV, HEAD_DIM)
    return (y_prompt, y_sample, conv_p[None], conv_s[None], kv5(k_p), k_s, kv5(v_p), v_s,
            pool_p[None], jnp.transpose(pool_s, (1, 0, 2))[None])
```

```python
import functools
from typing import NamedTuple

import jax
import jax.numpy as jnp
import numpy as np
from jax import lax
from jax.experimental import pallas as pl
from jax.experimental.pallas import tpu as pltpu

F32 = jnp.float32
BF16 = jnp.bfloat16

D_MODEL = 1024
D_CONV = 512
CONV_W = 3
HEAD_DIM = 64
N_HEADS = 8
N_KV = 2
GROUP = N_HEADS // N_KV
WINDOW = 128
BLOCK = 128
Q_DIM = N_HEADS * HEAD_DIM
KV_DIM = N_KV * HEAD_DIM
POOL_WINDOWS = (2, 4, 8, 16)
POOL_GROUP_DIM = D_MODEL // len(POOL_WINDOWS)
POOL_MAX = 16
D_FF = 4 * D_MODEL
D_IN_EVEN = 3 * D_CONV + Q_DIM + 2 * KV_DIM
N_META = 16
EPS = 1e-6
NEG = -1e30

META_PAD = BLOCK - N_META
SEQ_TILE = 512
FF_CHUNK = 1024
FF_STREAM = 512
CONV_HDR = 8
SAMPLE_BB = 16
SAMPLE_NEW = 8
VMEM_LIMIT = 56 * 1024 * 1024
MLP_YIELDS = 2 * (D_FF // FF_CHUNK)
L0_MIX_YIELDS = 1 + 2 * N_KV * (SEQ_TILE // BLOCK)


def _dot(a, b):
    return jnp.dot(a, b, preferred_element_type=F32)


def _rms(x, g):
    ms = jnp.mean(x * x, axis=-1, keepdims=True)
    return x * lax.rsqrt(ms + EPS) * g


def _head_rms(x, seg, g):
    ms = _dot((x * x).astype(BF16), seg)
    return x * lax.rsqrt(ms + EPS) * g


def _interleave(*steppers, shares=None):
    shares = shares or [1] * len(steppers)
    results = [None] * len(steppers)
    done = [0] * len(steppers)
    live = set(range(len(steppers)))
    while live:
        i = min(live, key=lambda k: ((done[k] + 1) / shares[k], k))
        try:
            next(steppers[i])
            done[i] += 1
        except StopIteration as finished:
            results[i] = finished.value
            live.discard(i)
    return results


def _mlp_steps(x, g, wup_ref, wdn_ref):
    xn = _rms(x, g).astype(BF16)
    acc = x
    for c in range(D_FF // FF_CHUNK):
        cols = slice(c * FF_CHUNK, (c + 1) * FF_CHUNK)
        h = _dot(xn, wup_ref[:, cols])
        a = jnp.square(jnp.maximum(h, 0.0)).astype(BF16)
        yield
        acc = acc + _dot(a, wdn_ref[cols, :])
        yield
    return acc


def _in_proj(x, g, win_ref, segq_ref, segk_ref, qg, kg):
    hn = _rms(x, g).astype(BF16)
    kv_col = 3 * D_CONV + Q_DIM
    half = x.shape[0] // 2
    z = _dot(hn, win_ref[:, 0:kv_col])
    kv = jnp.concatenate([_dot(hn[0:half], win_ref[:, kv_col:]), _dot(hn[half:], win_ref[:, kv_col:])], axis=0)
    xa = z[:, 0:D_CONV]
    gc = z[:, D_CONV:2 * D_CONV]
    gb = z[:, 2 * D_CONV:3 * D_CONV]
    q = z[:, 3 * D_CONV:]
    k = kv[:, 0:KV_DIM]
    v = kv[:, KV_DIM:]
    qn = _head_rms(q, segq_ref[...], qg) * (HEAD_DIM ** -0.5)
    kn = _head_rms(k, segk_ref[...], kg)
    return gc * xa, gb, qn, kn, v


def _l0_mixer(x, rows, sinks_ref, gmix, win_ref, segq_ref, segk_ref, qg, kg, cw_ref, wout_ref,
              u_scr, k_scr, v_scr, vt_scr, first_scr, band_scr):
    u, gb, qn, kn, v = _in_proj(x, gmix, win_ref, segq_ref, segk_ref, qg, kg)

    u_scr[CONV_HDR:CONV_HDR + rows, :] = u
    u1 = u_scr[CONV_HDR - 1:CONV_HDR - 1 + rows, :]
    u2 = u_scr[CONV_HDR - 2:CONV_HDR - 2 + rows, :]
    ya = gb * (u2 * cw_ref[0:1, :] + u1 * cw_ref[1:2, :] + u * cw_ref[2:3, :])
    k_scr[BLOCK:BLOCK + rows, :] = kn
    v_scr[BLOCK:BLOCK + rows, :] = v
    vt_scr[:, BLOCK:BLOCK + rows] = v.T
    yield

    yb_blocks = []
    for i in range(rows // BLOCK):
        qb = qn[i * BLOCK:(i + 1) * BLOCK, :]
        bias = first_scr[...] if i == 0 else band_scr[...]
        heads_t = []
        for j in range(N_KV):
            lanes = slice(j * HEAD_DIM, (j + 1) * HEAD_DIM)
            k2 = k_scr[i * BLOCK:(i + 2) * BLOCK, lanes].astype(BF16)
            v2_t = vt_scr[j * HEAD_DIM:(j + 1) * HEAD_DIM, i * BLOCK:(i + 2) * BLOCK].astype(BF16)
            qg_t = qb[:, j * GROUP * HEAD_DIM:(j + 1) * GROUP * HEAD_DIM].T
            qs_t = jnp.concatenate(
                [qg_t[g * HEAD_DIM:(g + 1) * HEAD_DIM, :] for g in range(GROUP)], axis=1).astype(BF16)
            sk = jnp.concatenate(
                [jnp.full((1, BLOCK), sinks_ref[j * GROUP + g], F32) for g in range(GROUP)], axis=1)
            st = _dot(k2, qs_t) + bias
            m = jnp.maximum(jnp.max(st, axis=0, keepdims=True), sk)
            p = jnp.exp(st - m)
            den = jnp.sum(p, axis=0, keepdims=True) + jnp.exp(sk - m)
            yield
            ot = _dot(v2_t, p.astype(BF16)) / den
            heads_t += [ot[:, g * BLOCK:(g + 1) * BLOCK] for g in range(GROUP)]
            yield
        yb_blocks.append(jnp.concatenate(heads_t, axis=0).T)
    yb = yb_blocks[0] if len(yb_blocks) == 1 else jnp.concatenate(yb_blocks, axis=0)

    u_scr[CONV_HDR - 2:CONV_HDR, :] = u_scr[CONV_HDR - 2 + rows:CONV_HDR + rows, :]
    k_scr[0:BLOCK, :] = k_scr[rows:rows + BLOCK, :]
    v_scr[0:BLOCK, :] = v_scr[rows:rows + BLOCK, :]
    vt_scr[:, 0:BLOCK] = vt_scr[:, rows:rows + BLOCK]

    mix =jnp.concatenate([ya, yb], axis=-1).astype(BF16)
    return x + _dot(mix, wout_ref[...])


def _pipeline_steps(s, x_ref, y_ref, x1_scr, mixer, mlp):
    slot = lax.rem(s, 2)

    def mix_tile():
        x1_scr[slot] = yield from mixer(x_ref[...], SEQ_TILE)

    def mlp_tile():
        y_ref[...] = yield from mlp(x1_scr[1 - slot])

    return mlp_tile, mix_tile


def _l0_prompt_kernel(sinks_ref, x_ref, meta_ref, gmix_ref, win_ref, segq_ref, segk_ref, qg_ref, kg_ref,
                      cw_ref, wout_ref, gmlp_ref, wup_ref, wdn_ref,
                      y_ref, ymeta_ref, conv_ref, kout_ref, vout_ref,
                      u_scr, k_scr, v_scr, vt_scr, first_scr, band_scr, x1_scr):
    s = pl.program_id(1)
    last = pl.num_programs(1) - 1
    mixer = functools.partial(
        _l0_mixer, sinks_ref=sinks_ref, gmix=gmix_ref[0:1, :], win_ref=win_ref, segq_ref=segq_ref,
        segk_ref=segk_ref, qg=qg_ref[...], kg=kg_ref[...], cw_ref=cw_ref, wout_ref=wout_ref,
        u_scr=u_scr, k_scr=k_scr, v_scr=v_scr, vt_scr=vt_scr, first_scr=first_scr, band_scr=band_scr)
    mlp = functools.partial(_mlp_steps, g=gmlp_ref[0:1, :], wup_ref=wup_ref, wdn_ref=wdn_ref)
    mlp_tile, mix_tile = _pipeline_steps(s, x_ref, y_ref, x1_scr, mixer, mlp)

    def band_bias(first_valid_key):
        c = lax.broadcasted_iota(jnp.int32, (2 * BLOCK, GROUP * BLOCK), 0)
        q = lax.broadcasted_iota(jnp.int32, (2 * BLOCK, GROUP * BLOCK), 1) & (BLOCK - 1)
        return jnp.where((c >= q) & (c <= q + WINDOW) & (c >= first_valid_key), 0.0, NEG)

    @pl.when(s == 0)
    def _start():
        band_scr[...] = band_bias(0)
        u_scr[0:CONV_HDR, :] = jnp.zeros((CONV_HDR, D_CONV), F32)
        k_scr[0:BLOCK, :] = jnp.zeros((BLOCK, KV_DIM), F32)
        v_scr[0:BLOCK, :] = jnp.zeros((BLOCK, KV_DIM), F32)
        vt_scr[:, 0:BLOCK] = jnp.zeros((KV_DIM, BLOCK), F32)
        first_scr[...] = band_bias(BLOCK + META_PAD)
        x_meta = jnp.concatenate([jnp.zeros((META_PAD, D_MODEL), F32), meta_ref[...]], axis=0)
        x1_meta, = _interleave(mixer(x_meta, BLOCK))
        first_scr[...] = band_bias(META_PAD)
        ymeta_ref[...], _ = _interleave(mlp(x1_meta), mix_tile(), shares=(MLP_YIELDS, L0_MIX_YIELDS))
        first_scr[...] = band_scr[...]

    @pl.when((s > 0) & (s < last))
    def _steady():
        _interleave(mlp_tile(), mix_tile(), shares=(MLP_YIELDS, L0_MIX_YIELDS))

    @pl.when(s == last)
    def _drain():
        _interleave(mlp_tile())

    @pl.when(s == last - 1)
    def _state():
        conv_ref[...] = u_scr[CONV_HDR - 2:CONV_HDR, :]
        kout_ref[...] = k_scr[0:BLOCK, :]
        vout_ref[...] = v_scr[0:BLOCK, :]


def _l1_mixer(x, rows, is_meta, gmix, wpool_ref, pscale, h_scr):
    h = _rms(x, gmix)
    h_scr[POOL_MAX:POOL_MAX + rows, :] = h
    ys = []
    for gi, w in enumerate(POOL_WINDOWS):
        cols = slice(gi * POOL_GROUP_DIM, (gi + 1) * POOL_GROUP_DIM)
        hg = h[:, cols]
        acc = hg
        for j in range(1, w):
            acc = acc + h_scr[POOL_MAX - j:POOL_MAX - j + rows, cols]
        if is_meta:
            r = lax.broadcasted_iota(jnp.int32, (rows, 1), 0)
            pooled = acc / jnp.clip(r - (META_PAD - 1), 1, w).astype(F32)
        else:
            pooled = acc * (1.0 / w)
        ys.append(_dot((pooled - hg).astype(BF16), wpool_ref[gi]))
        yield
    h_scr[0:POOL_MAX, :] = h_scr[rows:rows + POOL_MAX, :]
    return x + jnp.concatenate(ys, axis=-1) * pscale


def _l1_prompt_kernel(x_ref, meta_ref, gmix_ref, wpool_ref, pscale_ref, gmlp_ref, wup_ref, wdn_ref,
                      y_ref, pool_ref, h_scr, x1_scr):
    s = pl.program_id(1)
    last = pl.num_programs(1) - 1
    mixer = functools.partial(_l1_mixer, gmix=gmix_ref[1:2, :], wpool_ref=wpool_ref, pscale=pscale_ref[...],
                              h_scr=h_scr)
    mlp = functools.partial(_mlp_steps, g=gmlp_ref[1:2, :], wup_ref=wup_ref, wdn_ref=wdn_ref)
    mlp_tile, mix_tile = _pipeline_steps(s, x_ref, y_ref, x1_scr,
                                         functools.partial(mixer, is_meta=False), mlp)

    @pl.when(s == 0)
    def _start():
        h_scr[0:POOL_MAX, :] = jnp.zeros((POOL_MAX, D_MODEL), F32)
        _interleave(mixer(meta_ref[...], BLOCK, True))
        _interleave(mix_tile())

    @pl.when((s > 0) & (s < last))
    def _steady():
        _interleave(mlp_tile(), mix_tile(), shares=(MLP_YIELDS, len(POOL_WINDOWS)))

    @pl.when(s == last)
    def _drain():
        _interleave(mlp_tile())

    @pl.when(s == last - 1)
    def _state():
        pool_ref[...] = h_scr[1:POOL_MAX, :]


def _time_major(ref):
    return jnp.concatenate([ref[:, t, :] for t in range(ref.shape[1])], axis=0)


def _s0_pre_kernel(x_ref, st_ref, gmix_ref, win_ref, segq_ref, segk_ref, qg_ref, kg_ref, cw_ref,
                   ya_ref, q_ref, k_ref, v_ref, conv_ref, win_bf_ref):
    n_seq, n_t, _ = x_ref.shape
    win_bf_ref[...] = win_ref[...].astype(BF16)
    u, gb, qn, kn, v = _in_proj(_time_major(x_ref), gmix_ref[0:1, :], win_bf_ref, segq_ref, segk_ref, qg_ref[...],
                                kg_ref[...])
    rows = lambda a, t: a[t * n_seq:(t + 1) * n_seq, :]
    ue = [st_ref[:, i, :] for i in range(CONV_W - 1)] + [rows(u, t) for t in range(n_t)]
    for t in range(n_t):
        y = ue[t] * cw_ref[0:1, :] + ue[t + 1] * cw_ref[1:2, :] + ue[t + 2] * cw_ref[2:3, :]
        ya_ref[t * n_seq:(t + 1) * n_seq, :] = rows(gb, t) * y
        k_ref[:, t, :] = rows(kn, t)
        v_ref[:, t, :] = rows(v, t)
        for h in range(N_HEADS):
            q_ref[:, (h // GROUP) * n_t * GROUP + t * GROUP + h % GROUP, :] = (
                rows(qn, t)[:, h * HEAD_DIM:(h + 1) * HEAD_DIM])
    for i in range(CONV_W - 1):
        conv_ref[:, i, :] = ue[n_t + i]


def _s0_attn_kernel(q_ref, ckt_ref, cvt_ref, kn_ref, vn_ref, sk_ref, o_ref, kst_ref, vst_ref, kn_scr, vn_scr):
    bb, n_t = kn_ref.shape[0], kn_ref.shape[1]
    wb = ckt_ref.shape[3]
    rows = n_t * GROUP
    ext = []
    for old_ref, new_ref, scr, out_ref in ((ckt_ref, kn_ref, kn_scr, kst_ref), (cvt_ref, vn_ref, vn_scr, vst_ref)):
        scr[...] = jnp.zeros(scr.shape, F32)
        scr[:, 0:n_t, :] = new_ref[...]
        new_t = jnp.swapaxes(scr[...], 1, 2).reshape(bb, N_KV, HEAD_DIM, SAMPLE_NEW)
        ext.append(jnp.concatenate([old_ref[...], new_t], axis=-1))
        out_ref[...] = ext[-1][:, :, :, n_t:n_t + wb]
    t = lax.broadcasted_iota(jnp.int32, (rows, wb + SAMPLE_NEW), 0) // GROUP
    c = lax.broadcasted_iota(jnp.int32, (rows, wb + SAMPLE_NEW), 1)
    bias = jnp.where((c <= wb + t) & (c >= wb + t - WINDOW), 0.0, NEG)[None]
    for j in range(N_KV):
        q = q_ref[:, j * rows:(j + 1) * rows, :].astype(BF16)
        sk = sk_ref[j][None]
        s = jnp.einsum('bqd,bdk->bqk', q, ext[0][:, j].astype(BF16), preferred_element_type=F32) + bias
        m = jnp.maximum(jnp.max(s, axis=-1, keepdims=True), sk)
        p = jnp.exp(s - m)
        den = jnp.sum(p, axis=-1, keepdims=True) + jnp.exp(sk - m)
        o = jnp.einsum('bqk,bdk->bqd', p.astype(BF16), ext[1][:, j].astype(BF16), preferred_element_type=F32)
        o_ref[:, j * rows:(j + 1) * rows, :] = o / den


def _mlp_stream_step(c, wup_ref, wdn_ref, y_ref, wup_bf_ref, wdn_bf_ref, x1_scr, xn_scr):
    wup_bf_ref[...] = wup_ref[...].astype(BF16)
    wdn_bf_ref[...] = wdn_ref[...].astype(BF16)
    a = jnp.square(jnp.maximum(_dot(xn_scr[...], wup_bf_ref[...]), 0.0)).astype(BF16)
    x1_scr[...] += _dot(a, wdn_bf_ref[...])
    if y_ref is not None:
        @pl.when(c == pl.num_programs(0) - 1)
        def _done():
            y_ref[...] = x1_scr[...]


def _s0_post_kernel(x_ref, ya_ref, o_ref, wout_ref, gmlp_ref, wup_ref, wdn_ref,
                    y_ref, wout_bf_ref, wup_bf_ref, wdn_bf_ref, x1_scr, xn_scr):
    c = pl.program_id(0)

    @pl.when(c == 0)
    def _mix():
        n_t = x_ref.shape[1]
        wout_bf_ref[...] = wout_ref[...].astype(BF16)
        yb = jnp.concatenate(
            [jnp.concatenate([o_ref[:, (h // GROUP) * n_t * GROUP + t * GROUP + h % GROUP, :]
                              for h in range(N_HEADS)], axis=-1) for t in range(n_t)], axis=0)
        mix = jnp.concatenate([ya_ref[...], yb], axis=-1).astype(BF16)
        x1 = _time_major(x_ref) + _dot(mix, wout_bf_ref[...])
        x1_scr[...] = x1
        xn_scr[...] = _rms(x1, gmlp_ref[0:1, :]).astype(BF16)

    _mlp_stream_step(c, wup_ref, wdn_ref, y_ref, wup_bf_ref, wdn_bf_ref, x1_scr, xn_scr)


def _s1_kernel(x_ref, buf_ref, gmix_ref, wpool_ref, pscale_ref, gmlp_ref, wup_ref, wdn_ref,
               y_ref, pool_ref, wpool_bf_ref, wup_bf_ref, wdn_bf_ref, x1_scr, xn_scr):
    c = pl.program_id(0)
    n_buf, n_seq, _ = buf_ref.shape
    n_t = x_ref.shape[0] // n_seq

    @pl.when(c == 0)
    def _mix():
        wpool_bf_ref[...] = wpool_ref[...].astype(BF16)
        x = x_ref[...]
        h = _rms(x, gmix_ref[1:2, :])
        he = [buf_ref[i] for i in range(n_buf)] + [h[t * n_seq:(t + 1) * n_seq, :] for t in range(n_t)]
        for i in range(n_buf):
            pool_ref[i] = he[n_t + i]
        ys = []
        for gi, w in enumerate(POOL_WINDOWS):
            cols = slice(gi * POOL_GROUP_DIM, (gi + 1) * POOL_GROUP_DIM)
            dl = []
            for t in range(n_t):
                acc = he[n_buf + t][:, cols]
                for j in range(1, w):
                    acc = acc + he[n_buf + t - j][:, cols]
                dl.append(acc * (1.0 / w) - he[n_buf + t][:, cols])
            ys.append(_dot(jnp.concatenate(dl, axis=0).astype(BF16), wpool_bf_ref[gi]))
        x1 = x + jnp.concatenate(ys, axis=-1) * pscale_ref[...]
        x1_scr[...] = x1
        xn_scr[...] = _rms(x1, gmlp_ref[1:2, :]).astype(BF16)

    _mlp_stream_step(c, wup_ref, wdn_ref, None, wup_bf_ref, wdn_bf_ref, x1_scr, xn_scr)

    @pl.when(c == pl.num_programs(0) - 1)
    def _done():
        for t in range(n_t):
            y_ref[:, t, :] = x1_scr[t * n_seq:(t + 1) * n_seq, :]


class _Stacked(NamedTuple):
    array: jax.Array
    layer: int


def _operand(arg):
    return arg.array if isinstance(arg, _Stacked) else arg


def _resident(arg, n_grid):
    if isinstance(arg, _Stacked):
        shape, idx = (None,) + arg.array.shape[1:], (arg.layer,) + (0,) * (arg.array.ndim - 1)
    else:
        shape, idx = arg.shape, (0,) * arg.ndim
    index_map = {1: lambda i: idx, 2: lambda i, j: idx}[n_grid]
    return pl.BlockSpec(shape, index_map, pipeline_mode=pl.Buffered(1))


def _params(n_grid):
    return pltpu.CompilerParams(dimension_semantics=("arbitrary",) * n_grid, vmem_limit_bytes=VMEM_LIMIT)


def _prompt_specs(n_tiles):
    tile_in = pl.BlockSpec((None, SEQ_TILE, D_MODEL), lambda b, s: (b, jnp.minimum(s, n_tiles - 1), 0))
    tile_out = pl.BlockSpec((None, SEQ_TILE, D_MODEL), lambda b, s: (b, jnp.maximum(s - 1, 0), 0))
    per_b = lambda r, c: pl.BlockSpec((None, r, c), lambda b, s: (b, 0, 0))
    return tile_in, tile_out, per_b


def _l0_prompt(x, sinks, *resident):
    n_b, seq, _ = x.shape
    n_tiles = seq // SEQ_TILE
    tile_in, tile_out, per_b = _prompt_specs(n_tiles)
    return pl.pallas_call(
        _l0_prompt_kernel,
        grid=(n_b, n_tiles + 1),
        in_specs=[pl.BlockSpec(memory_space=pltpu.SMEM), tile_in] + [_resident(a, 2) for a in resident],
        out_specs=[tile_out, per_b(BLOCK, D_MODEL), per_b(CONV_W - 1, D_CONV), per_b(WINDOW, KV_DIM),
                   per_b(WINDOW, KV_DIM)],
        out_shape=[jax.ShapeDtypeStruct((n_b, seq, D_MODEL), F32),
                   jax.ShapeDtypeStruct((n_b, BLOCK, D_MODEL), F32),
                   jax.ShapeDtypeStruct((n_b, CONV_W - 1, D_CONV), F32),
                   jax.ShapeDtypeStruct((n_b, WINDOW, KV_DIM), F32),
                   jax.ShapeDtypeStruct((n_b, WINDOW, KV_DIM), F32)],
        scratch_shapes=[pltpu.VMEM((CONV_HDR + SEQ_TILE, D_CONV), F32),
                        pltpu.VMEM((BLOCK + SEQ_TILE, KV_DIM), F32),
                        pltpu.VMEM((BLOCK + SEQ_TILE, KV_DIM), F32),
                        pltpu.VMEM((KV_DIM, BLOCK + SEQ_TILE), F32),
                        pltpu.VMEM((2 * BLOCK, GROUP * BLOCK), F32),
                        pltpu.VMEM((2 * BLOCK, GROUP * BLOCK), F32),
                        pltpu.VMEM((2, SEQ_TILE, D_MODEL), F32)],
        compiler_params=_params(2),
        name="l0_prompt",
    )(sinks, x, *[_operand(a) for a in resident])


def _l1_prompt(x, xmeta, *resident):
    n_b, seq, _ = x.shape
    n_tiles = seq // SEQ_TILE
    tile_in, tile_out, per_b = _prompt_specs(n_tiles)
    return pl.pallas_call(
        _l1_prompt_kernel,
        grid=(n_b, n_tiles + 1),
        in_specs=[tile_in, per_b(BLOCK, D_MODEL)] + [_resident(a, 2) for a in resident],
        out_specs=[tile_out, per_b(POOL_MAX - 1, D_MODEL)],
        out_shape=[jax.ShapeDtypeStruct((n_b, seq, D_MODEL), F32),
                   jax.ShapeDtypeStruct((n_b, POOL_MAX - 1, D_MODEL), F32)],
        scratch_shapes=[pltpu.VMEM((POOL_MAX + SEQ_TILE, D_MODEL), F32),
                        pltpu.VMEM((2, SEQ_TILE, D_MODEL), F32)],
        compiler_params=_params(2),
        name="l1_prompt",
    )(x, xmeta, *[_operand(a) for a in resident])


def _single_step(kernel, name, out_shape, *args):
    return pl.pallas_call(
        kernel,
        grid=(1,),
        in_specs=[_resident(a, 1) for a in args],
        out_specs=[pl.BlockSpec(o.shape, lambda i, n=len(o.shape): (0,) * n) for o in out_shape],
        out_shape=out_shape,
        compiler_params=_params(1),
        name=name,
    )(*[_operand(a) for a in args])


def _streamed_mlp(kernel, name, n_rows, outs, consts, w_up, w_down, layer):
    const = lambda shape: pl.BlockSpec(shape, lambda c, n=len(shape): (0,) * n)
    return pl.pallas_call(
        kernel,
        grid=(D_FF // FF_STREAM,),
        in_specs=[_resident(a, 1) for a in consts]
        + [pl.BlockSpec((None, D_MODEL, FF_STREAM), lambda c: (layer, 0, c)),
           pl.BlockSpec((None, FF_STREAM, D_MODEL), lambda c: (layer, c, 0))],
        out_specs=[const(o.shape) for o in outs]
        + [pl.BlockSpec((D_MODEL, FF_STREAM), lambda c: (0, c)),
           pl.BlockSpec((FF_STREAM, D_MODEL), lambda c: (c, 0))],
        out_shape=list(outs) + [jax.ShapeDtypeStruct((D_MODEL, D_FF), BF16),
                                jax.ShapeDtypeStruct((D_FF, D_MODEL), BF16)],
        scratch_shapes=[pltpu.VMEM((n_rows, D_MODEL), F32), pltpu.VMEM((n_rows, D_MODEL), BF16)],
        compiler_params=_params(1),
        name=name,
    )(*[_operand(a) for a in consts], w_up, w_down)


def _s0_attn(q4, ckt, cvt, kn, vn, sk_rows):
    n_seq, _, _, wb = ckt.shape
    n_t = kn.shape[1]
    rows = n_t * GROUP
    blk = lambda *tail: pl.BlockSpec((SAMPLE_BB,) + tail, lambda i: (i,) + (0,) * len(tail))
    return pl.pallas_call(
        _s0_attn_kernel,
        grid=(n_seq // SAMPLE_BB,),
        in_specs=[blk(N_KV * rows, HEAD_DIM), blk(N_KV, HEAD_DIM, wb), blk(N_KV, HEAD_DIM, wb),
                  blk(n_t, KV_DIM), blk(n_t, KV_DIM), pl.BlockSpec((N_KV, rows, 1), lambda i: (0, 0, 0))],
        out_specs=[blk(N_KV * rows, HEAD_DIM), blk(N_KV, HEAD_DIM, wb), blk(N_KV, HEAD_DIM, wb)],
        out_shape=[jax.ShapeDtypeStruct((n_seq, N_KV * rows, HEAD_DIM), F32),
                   jax.ShapeDtypeStruct(ckt.shape, F32), jax.ShapeDtypeStruct(cvt.shape, F32)],
        scratch_shapes=[pltpu.VMEM((SAMPLE_BB, SAMPLE_NEW, KV_DIM), F32),
                        pltpu.VMEM((SAMPLE_BB, SAMPLE_NEW, KV_DIM), F32)],
        compiler_params=_params(1),
        name="s0_attn",
    )(q4, ckt, cvt, kn, vn, sk_rows)


def kernel(x_prompt, x_sample, state_conv, cache_k_win, cache_v_win, state_pool, meta_tokens, norm_mix, norm_mlp, w_in_even, conv_w, q_norm, k_norm, attn_sinks, w_out_even, w_pool, pool_scale, w_up, w_down):
    n_seq, n_t, _ = x_sample.shape
    wb = cache_k_win.shape[2]
    assert x_prompt.shape[1] % SEQ_TILE == 0 and x_prompt.shape[1] >= 2 * SEQ_TILE and n_seq % SAMPLE_BB == 0
    assert wb == WINDOW and n_t <= SAMPLE_NEW

    gmix = [norm_mix, norm_mix]
    gmlp = [norm_mlp, norm_mlp]
    qg = jnp.tile(q_norm[0], N_HEADS)[None, :]
    kg = jnp.tile(k_norm[0], N_KV)[None, :]
    seg = np.kron(np.eye(N_HEADS, dtype=np.float32), np.full((HEAD_DIM, HEAD_DIM), 1.0 / HEAD_DIM, np.float32))
    segq, segk = jnp.asarray(seg, BF16), jnp.asarray(seg[:KV_DIM, :KV_DIM], BF16)
    cw = _Stacked(conv_w, 0)
    sinks = attn_sinks[0]
    sds = lambda *shape, dtype=F32: jax.ShapeDtypeStruct(shape, dtype)

    n_rows = n_t * n_seq
    ya, q4, kn_b, vn_b, conv_s, win = _single_step(
        _s0_pre_kernel, "s0_pre",
        [sds(n_rows, D_CONV), sds(n_seq, N_HEADS * n_t, HEAD_DIM), sds(n_seq, n_t, KV_DIM),
         sds(n_seq, n_t, KV_DIM), sds(n_seq, CONV_W - 1, D_CONV), sds(D_MODEL, D_IN_EVEN, dtype=BF16)],
        x_sample, _Stacked(state_conv, 0), gmix[0], _Stacked(w_in_even, 0), segq, segk, qg, kg, cw)
    sk_rows = jnp.tile(sinks.reshape(N_KV, 1, GROUP), (1, n_t, 1)).reshape(N_KV, n_t * GROUP, 1)
    to_dev = lambda a: jnp.transpose(a.reshape(n_seq, -1, N_KV, HEAD_DIM), (0, 2, 3, 1))
    o4, kst, vst = _s0_attn(q4, to_dev(cache_k_win), to_dev(cache_v_win), kn_b, vn_b, sk_rows)
    k_s, v_s = (jnp.transpose(a, (0, 3, 1, 2))[None] for a in (kst, vst))
    xs2, wout, wup0, wdn0 = _streamed_mlp(
        _s0_post_kernel, "s0_post", n_rows, [sds(n_rows, D_MODEL), sds(D_MODEL, D_MODEL, dtype=BF16)],
        [x_sample, ya, o4, _Stacked(w_out_even, 0), gmlp[0]], w_up, w_down, 0)
    buf_t = jnp.transpose(state_pool[0], (1, 0, 2))
    y_sample, pool_s, wpool, wup1, wdn1 = _streamed_mlp(
        _s1_kernel, "s1", n_rows,
        [sds(n_seq, n_t, D_MODEL), sds(POOL_MAX - 1, n_seq, D_MODEL), sds(*w_pool.shape[1:], dtype=BF16)],
        [xs2, buf_t, gmix[1], _Stacked(w_pool, 0), pool_scale, gmlp[1]], w_up, w_down, 1)

    x2, x2_meta, conv_p, k_p, v_p = _l0_prompt(x_prompt, sinks, meta_tokens, gmix[0], win, segq, segk, qg, kg, cw,
                                               wout, gmlp[0], wup0, wdn0)
    y_prompt, pool_p = _l1_prompt(x2, x2_meta, gmix[1], wpool, pool_scale, gmlp[1], wup1, wdn1)

    kv5 = lambda a: a.reshape(1, a.shape[0], WINDOW, N_KV, HEAD_DIM)
    return (y_prompt, y_sample, conv_p[None], conv_s[None], kv5(k_p), k_s, kv5(v_p), v_s,
            pool_p[None], jnp.transpose(pool_s, (1, 0, 2))[None])
```

```python
import functools
from typing import NamedTuple

import jax
import jax.numpy as jnp
import numpy as np
from jax import lax
from jax.experimental import pallas as pl
from jax.experimental.pallas import tpu as pltpu

F32 = jnp.float32
BF16 = jnp.bfloat16

D_MODEL = 1024
D_CONV = 512
CONV_W = 3
HEAD_DIM = 64
N_HEADS = 8
N_KV = 2
GROUP = N_HEADS // N_KV
WINDOW = 128
BLOCK = 128
Q_DIM = N_HEADS * HEAD_DIM
KV_DIM = N_KV * HEAD_DIM
POOL_WINDOWS = (2, 4, 8, 16)
POOL_GROUP_DIM = D_MODEL // len(POOL_WINDOWS)
POOL_MAX = 16
D_FF = 4 * D_MODEL
D_IN_EVEN = 3 * D_CONV + Q_DIM + 2 * KV_DIM
N_META = 16
EPS = 1e-6
NEG = -1e30

META_PAD = BLOCK - N_META
SEQ_TILE = 512
FF_CHUNK = 1024
FF_STREAM = 512
CONV_HDR = 8
SAMPLE_BB = 16
SAMPLE_NEW = 8
VMEM_LIMIT = 56 * 1024 * 1024
MLP_YIELDS = 2 * (D_FF // FF_CHUNK)
L0_MIX_YIELDS = 1 + 2 * N_KV * (SEQ_TILE // BLOCK)


def _dot(a, b):
    return jnp.dot(a, b, preferred_element_type=F32)


def _dot_nt(a, b):
    return lax.dot_general(a, b, (((1,), (1,)), ((), ())), preferred_element_type=F32)


def _dot_tn(a, b):
    return lax.dot_general(a, b, (((0,), (0,)), ((), ())), preferred_element_type=F32)


def _rms(x, g):
    ms = jnp.mean(x * x, axis=-1, keepdims=True)
    return x * lax.rsqrt(ms + EPS) * g


def _head_rms(x, seg, g):
    ms = _dot((x * x).astype(BF16), seg)
    return x * lax.rsqrt(ms + EPS) * g


def _interleave(*steppers, shares=None):
    shares = shares or [1] * len(steppers)
    results = [None] * len(steppers)
    done = [0] * len(steppers)
    live = set(range(len(steppers)))
    while live:
        i = min(live, key=lambda k: ((done[k] + 1) / shares[k], k))
        try:
            next(steppers[i])
            done[i] += 1
        except StopIteration as finished:
            results[i] = finished.value
            live.discard(i)
    return results


def _mlp_steps(x, g, wup_ref, wdn_ref):
    xn = _rms(x, g).astype(BF16)
    acc = x
    for c in range(D_FF // FF_CHUNK):
        cols = slice(c * FF_CHUNK, (c + 1) * FF_CHUNK)
        h = _dot(xn, wup_ref[:, cols])
        a = jnp.square(jnp.maximum(h, 0.0)).astype(BF16)
        yield
        acc = acc + _dot(a, wdn_ref[cols, :])
        yield
    return acc


def _in_proj(x, g, win_ref, segq_ref, segk_ref, qg, kg):
    hn = _rms(x, g).astype(BF16)
    kv_col = 3 * D_CONV + Q_DIM
    half = x.shape[0] // 2
    z = _dot(hn, win_ref[:, 0:kv_col])
    kv = jnp.concatenate([_dot(hn[0:half], win_ref[:, kv_col:]), _dot(hn[half:], win_ref[:, kv_col:])], axis=0)
    xa = z[:, 0:D_CONV]
    gc = z[:, D_CONV:2 * D_CONV]
    gb = z[:, 2 * D_CONV:3 * D_CONV]
    q = z[:, 3 * D_CONV:]
    k = kv[:, 0:KV_DIM]
    v = kv[:, KV_DIM:]
    qn = _head_rms(q, segq_ref[...], qg) * (HEAD_DIM ** -0.5)
    kn = _head_rms(k, segk_ref[...], kg)
    return gc * xa, gb, qn, kn, v


def _l0_mixer(x, rows, sinks_ref, gmix, win_ref, segq_ref, segk_ref, qg, kg, cw_ref, wout_ref,
              u_scr, k_scr, v_scr, kb_scr, vb_scr, first_scr, band_scr):
    u, gb, qn, kn, v = _in_proj(x, gmix, win_ref, segq_ref, segk_ref, qg, kg)

    u_scr[CONV_HDR:CONV_HDR + rows, :] = u
    u1 = u_scr[CONV_HDR - 1:CONV_HDR - 1 + rows, :]
    u2 = u_scr[CONV_HDR - 2:CONV_HDR - 2 + rows, :]
    ya = gb * (u2 * cw_ref[0:1, :] + u1 * cw_ref[1:2, :] + u * cw_ref[2:3, :])
    k_scr[BLOCK:BLOCK + rows, :] = kn
    v_scr[BLOCK:BLOCK + rows, :] = v
    kb_scr[BLOCK:BLOCK + rows, :] = kn.astype(BF16)
    vb_scr[BLOCK:BLOCK + rows, :] = v.astype(BF16)
    yield

    yb_blocks = []
    for i in range(rows // BLOCK):
        qb = qn[i * BLOCK:(i + 1) * BLOCK, :]
        bias = first_scr[...] if i == 0 else band_scr[...]
        heads_t = []
        for j in range(N_KV):
            lanes = slice(j * HEAD_DIM, (j + 1) * HEAD_DIM)
            k2 = kb_scr[i * BLOCK:(i + 2) * BLOCK, lanes]
            v2 = vb_scr[i * BLOCK:(i + 2) * BLOCK, lanes]
            qg_t = qb[:, j * GROUP * HEAD_DIM:(j + 1) * GROUP * HEAD_DIM].T
            qs_t = jnp.concatenate(
                [qg_t[g * HEAD_DIM:(g + 1) * HEAD_DIM, :] for g in range(GROUP)], axis=1).astype(BF16)
            sk = jnp.concatenate(
                [jnp.full((1, BLOCK), sinks_ref[j * GROUP + g], F32) for g in range(GROUP)], axis=1)
            st = _dot(k2, qs_t) + bias
            m = jnp.maximum(jnp.max(st, axis=0, keepdims=True), sk)
            p = jnp.exp(st - m)
            den = jnp.sum(p, axis=0, keepdims=True) + jnp.exp(sk - m)
            yield
            ot = _dot_tn(v2, p.astype(BF16)) / den
            heads_t += [ot[:, g * BLOCK:(g + 1) * BLOCK] for g in range(GROUP)]
            yield
        yb_blocks.append(jnp.concatenate(heads_t, axis=0).T)
    yb = yb_blocks[0] if len(yb_blocks) == 1 else jnp.concatenate(yb_blocks, axis=0)

    u_scr[CONV_HDR - 2:CONV_HDR, :] = u_scr[CONV_HDR - 2 + rows:CONV_HDR + rows, :]
    k_scr[0:BLOCK, :] = k_scr[rows:rows + BLOCK, :]
    v_scr[0:BLOCK, :] = v_scr[rows:rows + BLOCK, :]
    kb_scr[0:BLOCK, :] = kb_scr[rows:rows + BLOCK, :]
    vb_scr[0:BLOCK, :] = vb_scr[rows:rows + BLOCK, :]

    mix =jnp.concatenate([ya, yb], axis=-1).astype(BF16)
    return x + _dot(mix, wout_ref[...])


def _pipeline_steps(s, x_ref, y_ref, x1_scr, mixer, mlp):
    slot = lax.rem(s, 2)

    def mix_tile():
        x1_scr[slot] = yield from mixer(x_ref[...], SEQ_TILE)

    def mlp_tile():
        y_ref[...] = yield from mlp(x1_scr[1 - slot])

    return mlp_tile, mix_tile


def _l0_prompt_kernel(sinks_ref, x_ref, meta_ref, gmix_ref, win_ref, segq_ref, segk_ref, qg_ref, kg_ref,
                      cw_ref, wout_ref, gmlp_ref, wup_ref, wdn_ref,
                      y_ref, ymeta_ref, conv_ref, kout_ref, vout_ref,
                      u_scr, k_scr, v_scr, kb_scr, vb_scr, first_scr, band_scr, x1_scr):
    s = pl.program_id(1)
    last = pl.num_programs(1) - 1
    mixer = functools.partial(
        _l0_mixer, sinks_ref=sinks_ref, gmix=gmix_ref[0:1, :], win_ref=win_ref, segq_ref=segq_ref,
        segk_ref=segk_ref, qg=qg_ref[...], kg=kg_ref[...], cw_ref=cw_ref, wout_ref=wout_ref,
        u_scr=u_scr, k_scr=k_scr, v_scr=v_scr, kb_scr=kb_scr, vb_scr=vb_scr, first_scr=first_scr,
        band_scr=band_scr)
    mlp = functools.partial(_mlp_steps, g=gmlp_ref[0:1, :], wup_ref=wup_ref, wdn_ref=wdn_ref)
    mlp_tile, mix_tile = _pipeline_steps(s, x_ref, y_ref, x1_scr, mixer, mlp)

    def band_bias(first_valid_key):
        c = lax.broadcasted_iota(jnp.int32, (2 * BLOCK, GROUP * BLOCK), 0)
        q = lax.broadcasted_iota(jnp.int32, (2 * BLOCK, GROUP * BLOCK), 1) & (BLOCK - 1)
        return jnp.where((c >= q) & (c <= q + WINDOW) & (c >= first_valid_key), 0.0, NEG)

    @pl.when(s == 0)
    def _start():
        band_scr[...] = band_bias(0)
        u_scr[0:CONV_HDR, :] = jnp.zeros((CONV_HDR, D_CONV), F32)
        k_scr[0:BLOCK, :] = jnp.zeros((BLOCK, KV_DIM), F32)
        v_scr[0:BLOCK, :] = jnp.zeros((BLOCK, KV_DIM), F32)
        kb_scr[0:BLOCK, :] = jnp.zeros((BLOCK, KV_DIM), BF16)
        vb_scr[0:BLOCK, :] = jnp.zeros((BLOCK, KV_DIM), BF16)
        first_scr[...] = band_bias(BLOCK + META_PAD)
        x_meta = jnp.concatenate([jnp.zeros((META_PAD, D_MODEL), F32), meta_ref[...]], axis=0)
        x1_meta, = _interleave(mixer(x_meta, BLOCK))
        first_scr[...] = band_bias(META_PAD)
        ymeta_ref[...], _ = _interleave(mlp(x1_meta), mix_tile(), shares=(MLP_YIELDS, L0_MIX_YIELDS))
        first_scr[...] = band_scr[...]

    @pl.when((s > 0) & (s < last))
    def _steady():
        _interleave(mlp_tile(), mix_tile(), shares=(MLP_YIELDS, L0_MIX_YIELDS))

    @pl.when(s == last)
    def _drain():
        _interleave(mlp_tile())

    @pl.when(s == last - 1)
    def _state():
        conv_ref[...] = u_scr[CONV_HDR - 2:CONV_HDR, :]
        kout_ref[...] = k_scr[0:BLOCK, :]
        vout_ref[...] = v_scr[0:BLOCK, :]


def _l1_mixer(x, rows, is_meta, gmix, wpool_ref, pscale, h_scr):
    h = _rms(x, gmix)
    h_scr[POOL_MAX:POOL_MAX + rows, :] = h
    ys = []
    for gi, w in enumerate(POOL_WINDOWS):
        cols = slice(gi * POOL_GROUP_DIM, (gi + 1) * POOL_GROUP_DIM)
        hg = h[:, cols]
        acc = hg
        for j in range(1, w):
            acc = acc + h_scr[POOL_MAX - j:POOL_MAX - j + rows, cols]
        if is_meta:
            r = lax.broadcasted_iota(jnp.int32, (rows, 1), 0)
            pooled = acc / jnp.clip(r - (META_PAD - 1), 1, w).astype(F32)
        else:
            pooled = acc * (1.0 / w)
        ys.append(_dot((pooled - hg).astype(BF16), wpool_ref[gi]))
        yield
    h_scr[0:POOL_MAX, :] = h_scr[rows:rows + POOL_MAX, :]
    return x + jnp.concatenate(ys, axis=-1) * pscale


def _l1_prompt_kernel(x_ref, meta_ref, gmix_ref, wpool_ref, pscale_ref, gmlp_ref, wup_ref, wdn_ref,
                      y_ref, pool_ref, h_scr, x1_scr):
    s = pl.program_id(1)
    last = pl.num_programs(1) - 1
    mixer = functools.partial(_l1_mixer, gmix=gmix_ref[1:2, :], wpool_ref=wpool_ref, pscale=pscale_ref[...],
                              h_scr=h_scr)
    mlp = functools.partial(_mlp_steps, g=gmlp_ref[1:2, :], wup_ref=wup_ref, wdn_ref=wdn_ref)
    mlp_tile, mix_tile = _pipeline_steps(s, x_ref, y_ref, x1_scr,
                                         functools.partial(mixer, is_meta=False), mlp)

    @pl.when(s == 0)
    def _start():
        h_scr[0:POOL_MAX, :] = jnp.zeros((POOL_MAX, D_MODEL), F32)
        _interleave(mixer(meta_ref[...], BLOCK, True))
        _interleave(mix_tile())

    @pl.when((s > 0) & (s < last))
    def _steady():
        _interleave(mlp_tile(), mix_tile(), shares=(MLP_YIELDS, len(POOL_WINDOWS)))

    @pl.when(s == last)
    def _drain():
        _interleave(mlp_tile())

    @pl.when(s == last - 1)
    def _state():
        pool_ref[...] = h_scr[1:POOL_MAX, :]


def _time_major(ref):
    return jnp.concatenate([ref[:, t, :] for t in range(ref.shape[1])], axis=0)


def _s0_pre_kernel(x_ref, st_ref, gmix_ref, win_ref, segq_ref, segk_ref, qg_ref, kg_ref, cw_ref,
                   ya_ref, q_ref, k_ref, v_ref, conv_ref, win_bf_ref):
    n_seq, n_t, _ = x_ref.shape
    win_bf_ref[...] = win_ref[...].astype(BF16)
    u, gb, qn, kn, v = _in_proj(_time_major(x_ref), gmix_ref[0:1, :], win_bf_ref, segq_ref, segk_ref, qg_ref[...],
                                kg_ref[...])
    rows = lambda a, t: a[t * n_seq:(t + 1) * n_seq, :]
    ue = [st_ref[:, i, :] for i in range(CONV_W - 1)] + [rows(u, t) for t in range(n_t)]
    for t in range(n_t):
        y = ue[t] * cw_ref[0:1, :] + ue[t + 1] * cw_ref[1:2, :] + ue[t + 2] * cw_ref[2:3, :]
        ya_ref[t * n_seq:(t + 1) * n_seq, :] = rows(gb, t) * y
        k_ref[:, t, :] = rows(kn, t)
        v_ref[:, t, :] = rows(v, t)
        for h in range(N_HEADS):
            q_ref[:, (h // GROUP) * n_t * GROUP + t * GROUP + h % GROUP, :] = (
                rows(qn, t)[:, h * HEAD_DIM:(h + 1) * HEAD_DIM])
    for i in range(CONV_W - 1):
        conv_ref[:, i, :] = ue[n_t + i]


def _s0_attn_kernel(q_ref, ckt_ref, cvt_ref, kn_ref, vn_ref, sk_ref, o_ref, kst_ref, vst_ref, kn_scr, vn_scr):
    bb, n_t = kn_ref.shape[0], kn_ref.shape[1]
    wb = ckt_ref.shape[3]
    rows = n_t * GROUP
    ext = []
    for old_ref, new_ref, scr, out_ref in ((ckt_ref, kn_ref, kn_scr, kst_ref), (cvt_ref, vn_ref, vn_scr, vst_ref)):
        scr[...] = jnp.zeros(scr.shape, F32)
        scr[:, 0:n_t, :] = new_ref[...]
        new_t = jnp.swapaxes(scr[...], 1, 2).reshape(bb, N_KV, HEAD_DIM, SAMPLE_NEW)
        ext.append(jnp.concatenate([old_ref[...], new_t], axis=-1))
        out_ref[...] = ext[-1][:, :, :, n_t:n_t + wb]
    t = lax.broadcasted_iota(jnp.int32, (rows, wb + SAMPLE_NEW), 0) // GROUP
    c = lax.broadcasted_iota(jnp.int32, (rows, wb + SAMPLE_NEW), 1)
    bias = jnp.where((c <= wb + t) & (c >= wb + t - WINDOW), 0.0, NEG)[None]
    for j in range(N_KV):
        q = q_ref[:, j * rows:(j + 1) * rows, :].astype(BF16)
        sk = sk_ref[j][None]
        s = jnp.einsum('bqd,bdk->bqk', q, ext[0][:, j].astype(BF16), preferred_element_type=F32) + bias
        m = jnp.maximum(jnp.max(s, axis=-1, keepdims=True), sk)
        p = jnp.exp(s - m)
        den = jnp.sum(p, axis=-1, keepdims=True) + jnp.exp(sk - m)
        o = jnp.einsum('bqk,bdk->bqd', p.astype(BF16), ext[1][:, j].astype(BF16), preferred_element_type=F32)
        o_ref[:, j * rows:(j + 1) * rows, :] = o / den


def _mlp_stream_step(c, wup_ref, wdn_ref, y_ref, wup_bf_ref, wdn_bf_ref, x1_scr, xn_scr):
    wup_bf_ref[...] = wup_ref[...].astype(BF16)
    wdn_bf_ref[...] = wdn_ref[...].astype(BF16)
    a = jnp.square(jnp.maximum(_dot(xn_scr[...], wup_bf_ref[...]), 0.0)).astype(BF16)
    x1_scr[...] += _dot(a, wdn_bf_ref[...])
    if y_ref is not None:
        @pl.when(c == pl.num_programs(0) - 1)
        def _done():
            y_ref[...] = x1_scr[...]


def _s0_post_kernel(x_ref, ya_ref, o_ref, wout_ref, gmlp_ref, wup_ref, wdn_ref,
                    y_ref, wout_bf_ref, wup_bf_ref, wdn_bf_ref, x1_scr, xn_scr):
    c = pl.program_id(0)

    @pl.when(c == 0)
    def _mix():
        n_t = x_ref.shape[1]
        wout_bf_ref[...] = wout_ref[...].astype(BF16)
        yb = jnp.concatenate(
            [jnp.concatenate([o_ref[:, (h // GROUP) * n_t * GROUP + t * GROUP + h % GROUP, :]
                              for h in range(N_HEADS)], axis=-1) for t in range(n_t)], axis=0)
        mix = jnp.concatenate([ya_ref[...], yb], axis=-1).astype(BF16)
        x1 = _time_major(x_ref) + _dot(mix, wout_bf_ref[...])
        x1_scr[...] = x1
        xn_scr[...] = _rms(x1, gmlp_ref[0:1, :]).astype(BF16)

    _mlp_stream_step(c, wup_ref, wdn_ref, y_ref, wup_bf_ref, wdn_bf_ref, x1_scr, xn_scr)


def _s1_kernel(x_ref, buf_ref, gmix_ref, wpool_ref, pscale_ref, gmlp_ref, wup_ref, wdn_ref,
               y_ref, pool_ref, wpool_bf_ref, wup_bf_ref, wdn_bf_ref, x1_scr, xn_scr):
    c = pl.program_id(0)
    n_buf, n_seq, _ = buf_ref.shape
    n_t = x_ref.shape[0] // n_seq

    @pl.when(c == 0)
    def _mix():
        wpool_bf_ref[...] = wpool_ref[...].astype(BF16)
        x = x_ref[...]
        h = _rms(x, gmix_ref[1:2, :])
        he = [buf_ref[i] for i in range(n_buf)] + [h[t * n_seq:(t + 1) * n_seq, :] for t in range(n_t)]
        for i in range(n_buf):
            pool_ref[i] = he[n_t + i]
        ys = []
        for gi, w in enumerate(POOL_WINDOWS):
            cols = slice(gi * POOL_GROUP_DIM, (gi + 1) * POOL_GROUP_DIM)
            dl = []
            for t in range(n_t):
                acc = he[n_buf + t][:, cols]
                for j in range(1, w):
                    acc = acc + he[n_buf + t - j][:, cols]
                dl.append(acc * (1.0 / w) - he[n_buf + t][:, cols])
            ys.append(_dot(jnp.concatenate(dl, axis=0).astype(BF16), wpool_bf_ref[gi]))
        x1 = x + jnp.concatenate(ys, axis=-1) * pscale_ref[...]
        x1_scr[...] = x1
        xn_scr[...] = _rms(x1, gmlp_ref[1:2, :]).astype(BF16)

    _mlp_stream_step(c, wup_ref, wdn_ref, None, wup_bf_ref, wdn_bf_ref, x1_scr, xn_scr)

    @pl.when(c == pl.num_programs(0) - 1)
    def _done():
        for t in range(n_t):
            y_ref[:, t, :] = x1_scr[t * n_seq:(t + 1) * n_seq, :]


class _Stacked(NamedTuple):
    array: jax.Array
    layer: int


def _operand(arg):
    return arg.array if isinstance(arg, _Stacked) else arg


def _resident(arg, n_grid):
    if isinstance(arg, _Stacked):
        shape, idx = (None,) + arg.array.shape[1:], (arg.layer,) + (0,) * (arg.array.ndim - 1)
    else:
        shape, idx = arg.shape, (0,) * arg.ndim
    index_map = {1: lambda i: idx, 2: lambda i, j: idx}[n_grid]
    return pl.BlockSpec(shape, index_map, pipeline_mode=pl.Buffered(1))


def _params(n_grid):
    return pltpu.CompilerParams(dimension_semantics=("arbitrary",) * n_grid, vmem_limit_bytes=VMEM_LIMIT)


def _prompt_specs(n_tiles):
    tile_in = pl.BlockSpec((None, SEQ_TILE, D_MODEL), lambda b, s: (b, jnp.minimum(s, n_tiles - 1), 0))
    tile_out = pl.BlockSpec((None, SEQ_TILE, D_MODEL), lambda b, s: (b, jnp.maximum(s - 1, 0), 0))
    per_b = lambda r, c: pl.BlockSpec((None, r, c), lambda b, s: (b, 0, 0))
    return tile_in, tile_out, per_b


def _l0_prompt(x, sinks, *resident):
    n_b, seq, _ = x.shape
    n_tiles = seq // SEQ_TILE
    tile_in, tile_out, per_b = _prompt_specs(n_tiles)
    return pl.pallas_call(
        _l0_prompt_kernel,
        grid=(n_b, n_tiles + 1),
        in_specs=[pl.BlockSpec(memory_space=pltpu.SMEM), tile_in] + [_resident(a, 2) for a in resident],
        out_specs=[tile_out, per_b(BLOCK, D_MODEL), per_b(CONV_W - 1, D_CONV), per_b(WINDOW, KV_DIM),
                   per_b(WINDOW, KV_DIM)],
        out_shape=[jax.ShapeDtypeStruct((n_b, seq, D_MODEL), F32),
                   jax.ShapeDtypeStruct((n_b, BLOCK, D_MODEL), F32),
                   jax.ShapeDtypeStruct((n_b, CONV_W - 1, D_CONV), F32),
                   jax.ShapeDtypeStruct((n_b, WINDOW, KV_DIM), F32),
                   jax.ShapeDtypeStruct((n_b, WINDOW, KV_DIM), F32)],
        scratch_shapes=[pltpu.VMEM((CONV_HDR + SEQ_TILE, D_CONV), F32),
                        pltpu.VMEM((BLOCK + SEQ_TILE, KV_DIM), F32),
                        pltpu.VMEM((BLOCK + SEQ_TILE, KV_DIM), F32),
                        pltpu.VMEM((BLOCK + SEQ_TILE, KV_DIM), BF16),
                        pltpu.VMEM((BLOCK + SEQ_TILE, KV_DIM), BF16),
                        pltpu.VMEM((2 * BLOCK, GROUP * BLOCK), F32),
                        pltpu.VMEM((2 * BLOCK, GROUP * BLOCK), F32),
                        pltpu.VMEM((2, SEQ_TILE, D_MODEL), F32)],
        compiler_params=_params(2),
        name="l0_prompt",
    )(sinks, x, *[_operand(a) for a in resident])


def _l1_prompt(x, xmeta, *resident):
    n_b, seq, _ = x.shape
    n_tiles = seq // SEQ_TILE
    tile_in, tile_out, per_b = _prompt_specs(n_tiles)
    return pl.pallas_call(
        _l1_prompt_kernel,
        grid=(n_b, n_tiles + 1),
        in_specs=[tile_in, per_b(BLOCK, D_MODEL)] + [_resident(a, 2) for a in resident],
        out_specs=[tile_out, per_b(POOL_MAX - 1, D_MODEL)],
        out_shape=[jax.ShapeDtypeStruct((n_b, seq, D_MODEL), F32),
                   jax.ShapeDtypeStruct((n_b, POOL_MAX - 1, D_MODEL), F32)],
        scratch_shapes=[pltpu.VMEM((POOL_MAX + SEQ_TILE, D_MODEL), F32),
                        pltpu.VMEM((2, SEQ_TILE, D_MODEL), F32)],
        compiler_params=_params(2),
        name="l1_prompt",
    )(x, xmeta, *[_operand(a) for a in resident])


def _single_step(kernel, name, out_shape, *args):
    return pl.pallas_call(
        kernel,
        grid=(1,),
        in_specs=[_resident(a, 1) for a in args],
        out_specs=[pl.BlockSpec(o.shape, lambda i, n=len(o.shape): (0,) * n) for o in out_shape],
        out_shape=out_shape,
        compiler_params=_params(1),
        name=name,
    )(*[_operand(a) for a in args])


def _streamed_mlp(kernel, name, n_rows, outs, consts, w_up, w_down, layer):
    const = lambda shape: pl.BlockSpec(shape, lambda c, n=len(shape): (0,) * n)
    return pl.pallas_call(
        kernel,
        grid=(D_FF // FF_STREAM,),
        in_specs=[_resident(a, 1) for a in consts]
        + [pl.BlockSpec((None, D_MODEL, FF_STREAM), lambda c: (layer, 0, c)),
           pl.BlockSpec((None, FF_STREAM, D_MODEL), lambda c: (layer, c, 0))],
        out_specs=[const(o.shape) for o in outs]
        + [pl.BlockSpec((D_MODEL, FF_STREAM), lambda c: (0, c)),
           pl.BlockSpec((FF_STREAM, D_MODEL), lambda c: (c, 0))],
        out_shape=list(outs) + [jax.ShapeDtypeStruct((D_MODEL, D_FF), BF16),
                                jax.ShapeDtypeStruct((D_FF, D_MODEL), BF16)],
        scratch_shapes=[pltpu.VMEM((n_rows, D_MODEL), F32), pltpu.VMEM((n_rows, D_MODEL), BF16)],
        compiler_params=_params(1),
        name=name,
    )(*[_operand(a) for a in consts], w_up, w_down)


def _s0_attn(q4, ckt, cvt, kn, vn, sk_rows):
    n_seq, _, _, wb = ckt.shape
    n_t = kn.shape[1]
    rows = n_t * GROUP
    blk = lambda *tail: pl.BlockSpec((SAMPLE_BB,) + tail, lambda i: (i,) + (0,) * len(tail))
    return pl.pallas_call(
        _s0_attn_kernel,
        grid=(n_seq // SAMPLE_BB,),
        in_specs=[blk(N_KV * rows, HEAD_DIM), blk(N_KV, HEAD_DIM, wb), blk(N_KV, HEAD_DIM, wb),
                  blk(n_t, KV_DIM), blk(n_t, KV_DIM), pl.BlockSpec((N_KV, rows, 1), lambda i: (0, 0, 0))],
        out_specs=[blk(N_KV * rows, HEAD_DIM), blk(N_KV, HEAD_DIM, wb), blk(N_KV, HEAD_DIM, wb)],
        out_shape=[jax.ShapeDtypeStruct((n_seq, N_KV * rows, HEAD_DIM), F32),
                   jax.ShapeDtypeStruct(ckt.shape, F32), jax.ShapeDtypeStruct(cvt.shape, F32)],
        scratch_shapes=[pltpu.VMEM((SAMPLE_BB, SAMPLE_NEW, KV_DIM), F32),
                        pltpu.VMEM((SAMPLE_BB, SAMPLE_NEW, KV_DIM), F32)],
        compiler_params=_params(1),
        name="s0_attn",
    )(q4, ckt, cvt, kn, vn, sk_rows)


def kernel(x_prompt, x_sample, state_conv, cache_k_win, cache_v_win, state_pool, meta_tokens, norm_mix, norm_mlp, w_in_even, conv_w, q_norm, k_norm, attn_sinks, w_out_even, w_pool, pool_scale, w_up, w_down):
    n_seq, n_t, _ = x_sample.shape
    wb = cache_k_win.shape[2]
    assert x_prompt.shape[1] % SEQ_TILE == 0 and x_prompt.shape[1] >= 2 * SEQ_TILE and n_seq % SAMPLE_BB == 0
    assert wb == WINDOW and n_t <= SAMPLE_NEW

    gmix = [norm_mix, norm_mix]
    gmlp = [norm_mlp, norm_mlp]
    qg = jnp.tile(q_norm[0], N_HEADS)[None, :]
    kg = jnp.tile(k_norm[0], N_KV)[None, :]
    seg = np.kron(np.eye(N_HEADS, dtype=np.float32), np.full((HEAD_DIM, HEAD_DIM), 1.0 / HEAD_DIM, np.float32))
    segq, segk = jnp.asarray(seg, BF16), jnp.asarray(seg[:KV_DIM, :KV_DIM], BF16)
    cw = _Stacked(conv_w, 0)
    sinks = attn_sinks[0]
    sds = lambda *shape, dtype=F32: jax.ShapeDtypeStruct(shape, dtype)

    n_rows = n_t * n_seq
    ya, q4, kn_b, vn_b, conv_s, win = _single_step(
        _s0_pre_kernel, "s0_pre",
        [sds(n_rows, D_CONV), sds(n_seq, N_HEADS * n_t, HEAD_DIM), sds(n_seq, n_t, KV_DIM),
         sds(n_seq, n_t, KV_DIM), sds(n_seq, CONV_W - 1, D_CONV), sds(D_MODEL, D_IN_EVEN, dtype=BF16)],
        x_sample, _Stacked(state_conv, 0), gmix[0], _Stacked(w_in_even, 0), segq, segk, qg, kg, cw)
    sk_rows = jnp.tile(sinks.reshape(N_KV, 1, GROUP), (1, n_t, 1)).reshape(N_KV, n_t * GROUP, 1)
    to_dev = lambda a: jnp.transpose(a.reshape(n_seq, -1, N_KV, HEAD_DIM), (0, 2, 3, 1))
    o4, kst, vst = _s0_attn(q4, to_dev(cache_k_win), to_dev(cache_v_win), kn_b, vn_b, sk_rows)
    k_s, v_s = (jnp.transpose(a, (0, 3, 1, 2))[None] for a in (kst, vst))
    xs2, wout, wup0, wdn0 = _streamed_mlp(
        _s0_post_kernel, "s0_post", n_rows, [sds(n_rows, D_MODEL), sds(D_MODEL, D_MODEL, dtype=BF16)],
        [x_sample, ya, o4, _Stacked(w_out_even, 0), gmlp[0]], w_up, w_down, 0)
    buf_t = jnp.transpose(state_pool[0], (1, 0, 2))
    y_sample, pool_s, wpool, wup1, wdn1 = _streamed_mlp(
        _s1_kernel, "s1", n_rows,
        [sds(n_seq, n_t, D_MODEL), sds(POOL_MAX - 1, n_seq, D_MODEL), sds(*w_pool.shape[1:], dtype=BF16)],
        [xs2, buf_t, gmix[1], _Stacked(w_pool, 0), pool_scale, gmlp[1]], w_up, w_down, 1)

    x2, x2_meta, conv_p, k_p, v_p = _l0_prompt(x_prompt, sinks, meta_tokens, gmix[0], win, segq, segk, qg, kg, cw,
                                               wout, gmlp[0], wup0, wdn0)
    y_prompt, pool_p = _l1_prompt(x2, x2_meta, gmix[1], wpool, pool_scale, gmlp[1], wup1, wdn1)

    kv5 = lambda a: a.reshape(1, a.shape[0], WINDOW, N_KV, HEAD_DIM)
    return (y_prompt, y_sample, conv_p[None], conv_s[None], kv5(k_p), k_s, kv5(v_p), v_s,
            pool_p[None], jnp.transpose(pool_s, (1, 0, 2))[None])
```
